```python
import math
import jax, jax.numpy as jnp
from jax import lax
import numpy as np

D_MODEL = 2048
BATCH = 16
SEQ = 2048
DEPTH = 2

CHUNK = 64
Q_BLOCK = 128
NORM_EPS = 1e-6
D_FF = ((8 * D_MODEL // 3 + 127) // 128) * 128

RET_WIDTH = D_MODEL // 2
RET_HEADS = 4
RET_HEAD_DIM = RET_WIDTH // RET_HEADS
RET_THETA = 10000.0

SSM_WIDTH = D_MODEL - RET_WIDTH
SSM_GROUP = 16
SSM_GROUPS = SSM_WIDTH // SSM_GROUP
SSM_STATE = 64
DT_MIN = 1e-3
DT_MAX = 1e-1

DIFF_HEAD_DIM = 128
DIFF_HEADS = D_MODEL // (2 * DIFF_HEAD_DIM)
ROPE_THETA = 500000.0
ROPE_FRAC = 4

N_EVEN = (DEPTH + 1) // 2
N_ODD = DEPTH // 2

kernel_name = "chunk_causal_retention_s5_diffattn_macaron"


def rms_norm(x, g):
    xf = x.astype(jnp.float32)
    y = xf * lax.rsqrt(jnp.mean(xf * xf, axis=-1, keepdims=True) + NORM_EPS)
    return (y * g.astype(jnp.float32)).astype(x.dtype)


def head_rms(x):
    xf = x.astype(jnp.float32)
    return (xf * lax.rsqrt(jnp.mean(xf * xf, axis=-1, keepdims=True) + NORM_EPS)).astype(x.dtype)


def swiglu(x, w_gate, w_up, w_down):
    return (jax.nn.silu(x @ w_gate) * (x @ w_up)) @ w_down


def rope_tables(seq, rot_dim, theta):
    inv = 1.0 / (theta ** (jnp.arange(0, rot_dim, 2, dtype=jnp.float32) / rot_dim))
    ang = jnp.arange(seq, dtype=jnp.float32)[:, None] * inv[None, :]
    return jnp.cos(ang), jnp.sin(ang)


def apply_rotary(x, cos, sin):
    half = cos.shape[-1]
    c = cos[:, None, :].astype(x.dtype)
    s = sin[:, None, :].astype(x.dtype)
    x1 = x[..., :half]
    x2 = x[..., half:2 * half]
    return jnp.concatenate([x1 * c - x2 * s, x2 * c + x1 * s, x[..., 2 * half:]], axis=-1)


def retention(q, k, v):
    b, s, h, d = q.shape
    nc = s // CHUNK
    dt = q.dtype
    log_g = jnp.log(1.0 - 2.0 ** (-5.0 - jnp.arange(h, dtype=jnp.float32)))
    idx = jnp.arange(CHUNK, dtype=jnp.float32)
    intra_decay = jnp.exp(log_g[:, None, None] * jnp.abs(idx[:, None] - idx[None, :])).astype(dt)
    q_decay = jnp.exp(log_g[None, :] * (idx[:, None] + 1.0)).astype(dt)[None, :, :, None]
    k_decay = jnp.exp(log_g[None, :] * (CHUNK - 1.0 - idx[:, None])).astype(dt)[None, :, :, None]
    chunk_decay = jnp.exp(log_g * CHUNK).astype(dt)[None, :, None, None]
    k = k * (d ** -0.5)
    qc = q.reshape(b, nc, CHUNK, h, d)
    kc = k.reshape(b, nc, CHUNK, h, d)
    vc = v.reshape(b, nc, CHUNK, h, d)
    scores = jnp.einsum('bnihd,bnjhd->bnhij', qc, kc) * intra_decay
    intra = jnp.einsum('bnhij,bnjhd->bnihd', scores, vc)

    def step(state, xs):
        q_i, k_i, v_i = xs
        inter = jnp.einsum('bihd,bhde->bihe', q_i * q_decay, state)
        state = state * chunk_decay + jnp.einsum('bjhd,bjhe->bhde', k_i * k_decay, v_i)
        return state, inter

    xs = (jnp.moveaxis(qc, 1, 0), jnp.moveaxis(kc, 1, 0), jnp.moveaxis(vc, 1, 0))
    _, inter = lax.scan(step, jnp.zeros((b, h, d, d), dt), xs)
    out = intra + jnp.moveaxis(inter, 0, 1)
    return out.reshape(b, s, h, d)


def s5_block(u, lam_re, lam_im, log_step, b_re, b_im, c_re, c_im, d_skip, w_glu, b_glu):
    bsz, s, _ = u.shape
    dt = u.dtype
    ug = u.reshape(bsz, s, SSM_GROUPS, SSM_GROUP)
    f32 = jnp.float32
    step = jnp.exp(log_step.astype(f32))[:, None]
    lr = lam_re.astype(f32)
    li = lam_im.astype(f32)
    mag = jnp.exp(lr * step)
    a_re = mag * jnp.cos(li * step)
    a_im = mag * jnp.sin(li * step)
    den = lr * lr + li * li
    nr = a_re - 1.0
    f_re = (nr * lr + a_im * li) / den
    f_im = (a_im * lr - nr * li) / den
    br = b_re.astype(f32)
    bi = b_im.astype(f32)
    bb_re = (f_re[..., None] * br - f_im[..., None] * bi).astype(dt)
    bb_im = (f_re[..., None] * bi + f_im[..., None] * br).astype(dt)
    x_re = jnp.einsum('bsgp,gnp->bsgn', ug, bb_re)
    x_im = jnp.einsum('bsgp,gnp->bsgn', ug, bb_im)
    shape = (1, s, SSM_GROUPS, SSM_STATE)
    a_re_s = jnp.broadcast_to(a_re.astype(dt)[None, None], shape)
    a_im_s = jnp.broadcast_to(a_im.astype(dt)[None, None], shape)

    def combine(left, right):
        ar1, ai1, br1, bi1 = left
        ar2, ai2, br2, bi2 = right
        return (ar1 * ar2 - ai1 * ai2,
                ar1 * ai2 + ai1 * ar2,
                ar2 * br1 - ai2 * bi1 + br2,
                ar2 * bi1 + ai2 * br1 + bi2)

    _, _, h_re, h_im = lax.associative_scan(combine, (a_re_s, a_im_s, x_re, x_im), axis=1)
    y = (jnp.einsum('bsgn,gpn->bsgp', h_re, c_re) - jnp.einsum('bsgn,gpn->bsgp', h_im, c_im))
    y = y.reshape(bsz, s, SSM_WIDTH) + d_skip * u
    z = jax.nn.gelu(y)
    return z * jax.nn.sigmoid(z @ w_glu + b_glu)


def retention_s5_mixer(h, w_in, w_out, lam_re, lam_im, log_step, b_re, b_im, c_re, c_im,
                       d_skip, w_glu, b_glu, ret_cos, ret_sin):
    b, s, _ = h.shape
    proj = h @ w_in
    q, k, v, g, u = jnp.split(proj, [RET_WIDTH, 2 * RET_WIDTH, 3 * RET_WIDTH, 4 * RET_WIDTH], axis=-1)
    q = apply_rotary(q.reshape(b, s, RET_HEADS, RET_HEAD_DIM), ret_cos, ret_sin)
    k = apply_rotary(k.reshape(b, s, RET_HEADS, RET_HEAD_DIM), ret_cos, ret_sin)
    v = v.reshape(b, s, RET_HEADS, RET_HEAD_DIM)
    y_a = head_rms(retention(q, k, v)).reshape(b, s, RET_WIDTH) * jax.nn.silu(g)
    y_b = s5_block(u, lam_re, lam_im, log_step, b_re, b_im, c_re, c_im, d_skip, w_glu, b_glu)
    return jnp.concatenate([y_a, y_b], axis=-1) @ w_out


def diff_attention(h, w_qkv, w_out, lq1, lk1, lq2, lk2, subln, cos, sin, lambda_init):
    b, s, _ = h.shape
    q, k, v = jnp.split(h @ w_qkv, 3, axis=-1)
    q = apply_rotary(q.reshape(b, s, 2 * DIFF_HEADS, DIFF_HEAD_DIM), cos, sin)
    k = apply_rotary(k.reshape(b, s, 2 * DIFF_HEADS, DIFF_HEAD_DIM), cos, sin)
    q = q.reshape(b, s, DIFF_HEADS, 2, DIFF_HEAD_DIM)
    k = k.reshape(b, s, DIFF_HEADS, 2, DIFF_HEAD_DIM)
    v = v.reshape(b, s, DIFF_HEADS, 2 * DIFF_HEAD_DIM)
    f32 = jnp.float32
    lam = (jnp.exp(jnp.sum(lq1.astype(f32) * lk1.astype(f32)))
           - jnp.exp(jnp.sum(lq2.astype(f32) * lk2.astype(f32))) + lambda_init)
    scale = DIFF_HEAD_DIM ** -0.5
    neg = jnp.finfo(f32).min
    outs = []
    for blk in range(s // Q_BLOCK):
        q0 = blk * Q_BLOCK
        kv_end = q0 + Q_BLOCK
        qb = q[:, q0:kv_end]
        kb = k[:, :kv_end]
        vb = v[:, :kv_end]
        sc = jnp.einsum('bqhcd,bkhcd->bhcqk', qb, kb).astype(f32) * scale
        q_chunk = jnp.arange(q0, kv_end) // CHUNK
        k_chunk = jnp.arange(kv_end) // CHUNK
        sc = jnp.where(q_chunk[:, None] >= k_chunk[None, :], sc, neg)
        p = jax.nn.softmax(sc, axis=-1)
        attn = p[:, :, 0] - lam * p[:, :, 1]
        outs.append(jnp.einsum('bhqk,bkhe->bqhe', attn.astype(v.dtype), vb))
    o = jnp.concatenate(outs, axis=1)
    o = rms_norm(o, subln) * (1.0 - lambda_init)
    return o.reshape(b, s, D_MODEL) @ w_out


def setup_inputs(seed: int = 0) -> dict:
    key = jax.random.key(seed)
    ks = jax.random.split(key, 32)
    f32 = jnp.float32
    D = D_MODEL

    def nrm(k, shape, scale):
        return jax.random.normal(k, shape, f32) * scale

    n_idx = jnp.arange(SSM_STATE, dtype=f32)
    return {
        "x": jax.random.normal(ks[0], (BATCH, SEQ, D), f32),
        "ffn_norm": 1.0 + nrm(ks[1], (DEPTH, 2, D), 0.01),
        "ffn_w_gate": nrm(ks[2], (DEPTH, 2, D, D_FF), D ** -0.5),
        "ffn_w_up": nrm(ks[3], (DEPTH, 2, D, D_FF), D ** -0.5),
        "ffn_w_down": nrm(ks[4], (DEPTH, 2, D_FF, D), D_FF ** -0.5),
        "mix_norm": 1.0 + nrm(ks[5], (DEPTH, D), 0.01),
        "ab_w_in": nrm(ks[6], (N_EVEN, D, 4 * RET_WIDTH + SSM_WIDTH), D ** -0.5),
        "ab_w_out": nrm(ks[7], (N_EVEN, RET_WIDTH + SSM_WIDTH, D), (RET_WIDTH + SSM_WIDTH) ** -0.5),
        "ssm_lambda_re": -0.5 + nrm(ks[8], (N_EVEN, SSM_GROUPS, SSM_STATE), 0.01),
        "ssm_lambda_im": math.pi * n_idx + nrm(ks[9], (N_EVEN, SSM_GROUPS, SSM_STATE), 0.01),
        "ssm_log_step": jax.random.uniform(ks[10], (N_EVEN, SSM_GROUPS), f32,
                                           math.log(DT_MIN), math.log(DT_MAX)),
        "ssm_b_re": nrm(ks[11], (N_EVEN, SSM_GROUPS, SSM_STATE, SSM_GROUP), (2 * SSM_GROUP) ** -0.5),
        "ssm_b_im": nrm(ks[12], (N_EVEN, SSM_GROUPS, SSM_STATE, SSM_GROUP), (2 * SSM_GROUP) ** -0.5),
        "ssm_c_re": nrm(ks[13], (N_EVEN, SSM_GROUPS, SSM_GROUP, SSM_STATE), SSM_STATE ** -0.5),
        "ssm_c_im": nrm(ks[14], (N_EVEN, SSM_GROUPS, SSM_GROUP, SSM_STATE), SSM_STATE ** -0.5),
        "ssm_d": nrm(ks[15], (N_EVEN, SSM_WIDTH), 1.0),
        "ssm_w_glu": nrm(ks[16], (N_EVEN, SSM_WIDTH, SSM_WIDTH), SSM_WIDTH ** -0.5),
        "ssm_b_glu": nrm(ks[17], (N_EVEN, SSM_WIDTH), 0.01),
        "c_w_qkv": nrm(ks[18], (N_ODD, D, 3 * D), D ** -0.5),
        "c_w_out": nrm(ks[19], (N_ODD, D, D), D ** -0.5),
        "c_lambda_q1": nrm(ks[20], (N_ODD, DIFF_HEAD_DIM), 0.1),
        "c_lambda_k1": nrm(ks[21], (N_ODD, DIFF_HEAD_DIM), 0.1),
        "c_lambda_q2": nrm(ks[22], (N_ODD, DIFF_HEAD_DIM), 0.1),
        "c_lambda_k2": nrm(ks[23], (N_ODD, DIFF_HEAD_DIM), 0.1),
        "c_subln": 1.0 + nrm(ks[24], (N_ODD, 2 * DIFF_HEAD_DIM), 0.01),
        "final_norm": 1.0 + nrm(ks[25], (D,), 0.01),
    }


def reference(x, ffn_norm, ffn_w_gate, ffn_w_up, ffn_w_down, mix_norm, ab_w_in, ab_w_out,
              ssm_lambda_re, ssm_lambda_im, ssm_log_step, ssm_b_re, ssm_b_im, ssm_c_re, ssm_c_im,
              ssm_d, ssm_w_glu, ssm_b_glu, c_w_qkv, c_w_out, c_lambda_q1, c_lambda_k1,
              c_lambda_q2, c_lambda_k2, c_subln, final_norm):
    s = x.shape[1]
    ret_cos, ret_sin = rope_tables(s, RET_HEAD_DIM, RET_THETA)
    att_cos, att_sin = rope_tables(s, DIFF_HEAD_DIM // ROPE_FRAC, ROPE_THETA)
    h = x
    for layer in range(DEPTH):
        i = layer // 2
        h = h + 0.5 * swiglu(rms_norm(h, ffn_norm[layer, 0]), ffn_w_gate[layer, 0],
                             ffn_w_up[layer, 0], ffn_w_down[layer, 0])
        hn = rms_norm(h, mix_norm[layer])
        if layer % 2 == 0:
            mix = retention_s5_mixer(hn, ab_w_in[i], ab_w_out[i], ssm_lambda_re[i], ssm_lambda_im[i],
                                     ssm_log_step[i], ssm_b_re[i], ssm_b_im[i], ssm_c_re[i],
                                     ssm_c_im[i], ssm_d[i], ssm_w_glu[i], ssm_b_glu[i],
                                     ret_cos, ret_sin)
        else:
            lambda_init = 0.8 - 0.6 * math.exp(-0.3 * layer)
            mix = diff_attention(hn, c_w_qkv[i], c_w_out[i], c_lambda_q1[i], c_lambda_k1[i],
                                 c_lambda_q2[i], c_lambda_k2[i], c_subln[i], att_cos, att_sin,
                                 lambda_init)
        h = h + mix
        h = h + 0.5 * swiglu(rms_norm(h, ffn_norm[layer, 1]), ffn_w_gate[layer, 1],
                             ffn_w_up[layer, 1], ffn_w_down[layer, 1])
    return rms_norm(h, final_norm)
```

```python
import functools
import math

import jax
import jax.numpy as jnp
from jax import lax
from jax.experimental import pallas as pl
from jax.experimental.pallas import tpu as pltpu

F32 = jnp.float32
BF16 = jnp.bfloat16

D_MODEL = 2048
D_FF = 5504
CHUNK = 64
NORM_EPS = 1e-6

RET_WIDTH = 1024
RET_HEADS = 4
RET_HEAD_DIM = 256
RET_THETA = 10000.0
RET_BLOCK = 256

SSM_WIDTH = 1024
SSM_GROUP = 16
SSM_GROUPS = 64
SSM_STATE = 64
SSM_T = 16
SSM_GPB = 128 // SSM_GROUP
SSM_NJ = SSM_WIDTH // 128

DIFF_HEAD_DIM = 128
DIFF_HEADS = 8
ROPE_THETA = 500000.0
ROT_HALF = DIFF_HEAD_DIM // 4 // 2
ATT_QBLOCK = 256

LANES = 128
FF_TILE = 512
D_FF_PAD = ((D_FF + FF_TILE - 1) // FF_TILE) * FF_TILE
VMEM_LIMIT = 56 * 1024 * 1024


def _params(*sem):
    return pltpu.CompilerParams(dimension_semantics=sem, vmem_limit_bytes=VMEM_LIMIT)


def _rms(x, g):
    y = x * lax.rsqrt(jnp.mean(x * x, axis=-1, keepdims=True) + NORM_EPS)
    return y * g


def _dot(a, b):
    return jnp.dot(a, b, preferred_element_type=F32)


def _dot_nt(a, b):
    return lax.dot_general(a, b, (((1,), (1,)), ((), ())), preferred_element_type=F32)


def _dot_tn(a, b):
    return lax.dot_general(a, b, (((0,), (0,)), ((), ())), preferred_element_type=F32)


def _ffn_body(x_ref, g_ref, wg_ref, wu_ref, wd_ref, fn_ref, o_ref, xn_ref, acc_ref, *, final):
    j = pl.program_id(1)

    @pl.when(j == 0)
    def _():
        xn_ref[...] = _rms(x_ref[...], g_ref[...]).astype(BF16)
        acc_ref[...] = jnp.zeros_like(acc_ref)

    xn = xn_ref[...]
    gate = _dot(xn, wg_ref[...])
    up = _dot(xn, wu_ref[...])
    act = (gate * jax.nn.sigmoid(gate) * up).astype(BF16)
    acc_ref[...] += _dot(act, wd_ref[...])

    @pl.when(j == pl.num_programs(1) - 1)
    def _():
        h = x_ref[...] + 0.5 * acc_ref[...]
        if final:
            h = _rms(h, fn_ref[...])
        o_ref[...] = h


def _ffn(h, norm_g, wg, wu, wd, final_g, *, final):
    n, d = h.shape
    tm = min(512, n)
    fp = wg.shape[1]
    grid = (n // tm, fp // FF_TILE)
    return pl.pallas_call(
        functools.partial(_ffn_body, final=final),
        grid=grid,
        in_specs=[
            pl.BlockSpec((tm, d), lambda i, j: (i, 0)),
            pl.BlockSpec((1, d), lambda i, j: (0, 0)),
            pl.BlockSpec((d, FF_TILE), lambda i, j: (0, j)),
            pl.BlockSpec((d, FF_TILE), lambda i, j: (0, j)),
            pl.BlockSpec((FF_TILE, d), lambda i, j: (j, 0)),
            pl.BlockSpec((1, d), lambda i, j: (0, 0)),
        ],
        out_specs=pl.BlockSpec((tm, d), lambda i, j: (i, 0)),
        out_shape=jax.ShapeDtypeStruct((n, d), F32),
        scratch_shapes=[pltpu.VMEM((tm, d), BF16), pltpu.VMEM((tm, d), F32)],
        compiler_params=_params("parallel", "arbitrary"),
        name="ffn",
    )(h, norm_g.reshape(1, d), wg, wu, wd, final_g.reshape(1, d))


PROJ_TN = 1024


def _proj_ab_body(x_ref, g_ref, w_ref, cos_ref, sin_ref, o_ref, xn_ref):
    j = pl.program_id(1)

    @pl.when(j == 0)
    def _():
        xn_ref[...] = _rms(x_ref[...], g_ref[...]).astype(BF16)

    y = _dot(xn_ref[...], w_ref[...])

    @pl.when(j < 2)
    def _():
        c = cos_ref[...]
        s = sin_ref[...]
        sc = jnp.where(j == 1, RET_HEAD_DIM ** -0.5, 1.0).astype(F32)
        half = RET_HEAD_DIM // 2
        for hd in range(PROJ_TN // RET_HEAD_DIM):
            lo = hd * RET_HEAD_DIM
            x1 = y[:, lo:lo + half]
            x2 = y[:, lo + half:lo + 2 * half]
            o_ref[:, lo:lo + half] = (x1 * c - x2 * s) * sc
            o_ref[:, lo + half:lo + 2 * half] = (x2 * c + x1 * s) * sc

    @pl.when(j >= 2)
    def _():
        o_ref[...] = y


def _proj_ab(h, norm_g, w, cos, sin, seq):
    n, d = h.shape
    nout = w.shape[1]
    tm = min(1024, seq)
    per_seq = seq // tm
    return pl.pallas_call(
        _proj_ab_body,
        grid=(n // tm, nout // PROJ_TN),
        in_specs=[
            pl.BlockSpec((tm, d), lambda i, j: (i, 0)),
            pl.BlockSpec((1, d), lambda i, j: (0, 0)),
            pl.BlockSpec((d, PROJ_TN), lambda i, j: (0, j)),
            pl.BlockSpec((tm, RET_HEAD_DIM // 2), lambda i, j: (i % per_seq, 0)),
            pl.BlockSpec((tm, RET_HEAD_DIM // 2), lambda i, j: (i % per_seq, 0)),
        ],
        out_specs=pl.BlockSpec((tm, PROJ_TN), lambda i, j: (i, j)),
        out_shape=jax.ShapeDtypeStruct((n, nout), F32),
        scratch_shapes=[pltpu.VMEM((tm, d), BF16)],
        compiler_params=_params("parallel", "arbitrary"),
        name="proj_ab",
    )(h, norm_g.reshape(1, d), w, cos, sin)


def _proj_c_body(x_ref, g_ref, w_ref, cf_ref, sa_ref, sb_ref, o_ref, xn_ref):
    j = pl.program_id(1)

    @pl.when(j == 0)
    def _():
        xn_ref[...] = _rms(x_ref[...], g_ref[...]).astype(BF16)

    y = _dot(xn_ref[...], w_ref[...])

    @pl.when(j < 2 * D_MODEL // PROJ_TN)
    def _():
        cf = cf_ref[...]
        sa = sa_ref[...]
        sb = sb_ref[...]
        for hd in range(PROJ_TN // DIFF_HEAD_DIM):
            lo = hd * DIFF_HEAD_DIM
            yh = y[:, lo:lo + DIFF_HEAD_DIM]
            r = (yh * cf + pltpu.roll(yh, ROT_HALF, axis=1) * sa
                 + pltpu.roll(yh, DIFF_HEAD_DIM - ROT_HALF, axis=1) * sb)
            o_ref[:, lo:lo + DIFF_HEAD_DIM] = r.astype(BF16)

    @pl.when(j >= 2 * D_MODEL // PROJ_TN)
    def _():
        o_ref[...] = y.astype(BF16)


def _proj_c(h, norm_g, w, cf, sa, sb, seq):
    n, d = h.shape
    nout = w.shape[1]
    tm = min(1024, seq)
    per_seq = seq // tm
    tab = pl.BlockSpec((tm, DIFF_HEAD_DIM), lambda i, j: (i % per_seq, 0))
    return pl.pallas_call(
        _proj_c_body,
        grid=(n // tm, nout // PROJ_TN),
        in_specs=[
            pl.BlockSpec((tm, d), lambda i, j: (i, 0)),
            pl.BlockSpec((1, d), lambda i, j: (0, 0)),
            pl.BlockSpec((d, PROJ_TN), lambda i, j: (0, j)),
            tab, tab, tab,
        ],
        out_specs=pl.BlockSpec((tm, PROJ_TN), lambda i, j: (i, j)),
        out_shape=jax.ShapeDtypeStruct((n, nout), BF16),
        scratch_shapes=[pltpu.VMEM((tm, d), BF16)],
        compiler_params=_params("parallel", "arbitrary"),
        name="proj_c",
    )(h, norm_g.reshape(1, d), w, cf, sa, sb)


def _ret_body(q_ref, k_ref, v_ref, g_ref, dm_ref, qd_ref, kd_ref, cd_ref, o_ref, *, nblk):
    t = RET_BLOCK
    dmask = dm_ref[...]
    qdec = qd_ref[...]
    kdec = kd_ref[...]
    cdec = cd_ref[...]

    def step(n, state):
        r0 = pl.multiple_of(n * t, t)
        q = q_ref[pl.ds(r0, t), :]
        k = k_ref[pl.ds(r0, t), :]
        vb = v_ref[pl.ds(r0, t), :].astype(BF16)
        scores = _dot_nt(q.astype(BF16), k.astype(BF16)) * dmask
        intra = _dot(scores.astype(BF16), vb)
        inter = _dot((q * qdec).astype(BF16), state.astype(BF16))
        new_state = state * cdec + _dot_tn((k * kdec).astype(BF16), vb)
        o = intra + inter
        o = o * lax.rsqrt(jnp.mean(o * o, axis=-1, keepdims=True) + NORM_EPS)
        gate = g_ref[pl.ds(r0, t), :]
        o_ref[pl.ds(r0, t), :] = (o * (gate * jax.nn.sigmoid(gate))).astype(BF16)
        return new_state

    lax.fori_loop(0, nblk, step, jnp.zeros((RET_HEAD_DIM, RET_HEAD_DIM), F32))


def _retention(proj, batch, seq):
    n = proj.shape[0]
    t = RET_BLOCK
    hd = RET_HEAD_DIM
    log_g = jnp.log(1.0 - 2.0 ** (-5.0 - jnp.arange(RET_HEADS, dtype=F32)))
    idx = jnp.arange(t, dtype=F32)
    chunk = jnp.arange(t) // CHUNK
    visible = chunk[:, None] >= chunk[None, :]
    dist = jnp.abs(idx[:, None] - idx[None, :])
    dmask = jnp.where(visible[None], jnp.exp(log_g[:, None, None] * dist[None]), 0.0)
    qdec = jnp.broadcast_to(jnp.exp(log_g[:, None] * (idx[None] + 1.0))[:, :, None], (RET_HEADS, t, hd))
    kdec = jnp.broadcast_to(jnp.exp(log_g[:, None] * (t - 1.0 - idx[None]))[:, :, None], (RET_HEADS, t, hd))
    cdec = jnp.broadcast_to(jnp.exp(log_g * t)[:, None, None], (RET_HEADS, 1, hd))

    def col(off):
        return pl.BlockSpec((seq, hd), lambda b, h: (b, off + h))

    def tab(rows, cols):
        return pl.BlockSpec((None, rows, cols), lambda b, h: (h, 0, 0))

    return pl.pallas_call(
        functools.partial(_ret_body, nblk=seq // t),
        grid=(batch, RET_HEADS),
        in_specs=[col(0), col(RET_HEADS), col(2 * RET_HEADS), col(3 * RET_HEADS),
                  tab(t, t), tab(t, hd), tab(t, hd), tab(1, hd)],
        out_specs=pl.BlockSpec((seq, hd), lambda b, h: (b, h)),
        out_shape=jax.ShapeDtypeStruct((n, RET_WIDTH), BF16),
        compiler_params=_params("parallel", "parallel"),
        name="retention",
    )(proj, proj, proj, proj, dmask, qdec, kdec, cdec)


def _s5_weights(lam_re, lam_im, log_step, b_re, b_im, c_re, c_im):
    hp = lax.Precision.HIGHEST
    step = jnp.exp(log_step)[:, None]
    mag = jnp.exp(lam_re * step)
    a_re = mag * jnp.cos(lam_im * step)
    a_im = mag * jnp.sin(lam_im * step)
    den = lam_re * lam_re + lam_im * lam_im
    nr = a_re - 1.0
    f_re = (nr * lam_re + a_im * lam_im) / den
    f_im = (a_im * lam_re - nr * lam_im) / den
    bb_re = f_re[..., None] * b_re - f_im[..., None] * b_im
    bb_im = f_re[..., None] * b_im + f_im[..., None] * b_re
    prs, pis = [jnp.ones_like(a_re)], [jnp.zeros_like(a_im)]
    for _ in range(SSM_T):
        prs.append(prs[-1] * a_re - pis[-1] * a_im)
        pis.append(prs[-2] * a_im + pis[-1] * a_re)
    pr = jnp.stack(prs)
    pi = jnp.stack(pis)
    ca_re = c_re[None] * pr[:, :, None, :] - c_im[None] * pi[:, :, None, :]
    ca_im = c_re[None] * pi[:, :, None, :] + c_im[None] * pr[:, :, None, :]
    kern = (jnp.einsum('kgpn,gnq->gkpq', ca_re[:SSM_T], bb_re, precision=hp)
            - jnp.einsum('kgpn,gnq->gkpq', ca_im[:SSM_T], bb_im, precision=hp))
    lag = jnp.arange(SSM_T)[None, :] - jnp.arange(SSM_T)[:, None]
    toe = jnp.where((lag >= 0)[None, :, :, None, None], kern[:, jnp.clip(lag, 0)], 0.0)
    eye = jnp.eye(SSM_GPB, dtype=F32)
    nj = SSM_NJ
    toe = toe.reshape(nj, SSM_GPB, SSM_T, SSM_T, SSM_GROUP, SSM_GROUP)
    m_intra = jnp.einsum('jastpq,ab->jsaqtbp', toe, eye).reshape(nj, SSM_T * 128, SSM_T * 128)
    ro_re = jnp.moveaxis(ca_re[1:], 0, 2)
    ro_im = -jnp.moveaxis(ca_im[1:], 0, 2)
    ro = jnp.stack([ro_re, ro_im]).reshape(2, nj, SSM_GPB, SSM_GROUP, SSM_T, SSM_STATE)
    m_out = jnp.einsum('cjaptn,ab->jcantbp', ro, eye).reshape(nj, 2 * SSM_GPB * SSM_STATE, SSM_T * 128)
    w_y = jnp.concatenate([m_intra, m_out], axis=1).astype(BF16)
    rev_re = jnp.stack([prs[SSM_T - 1 - s] for s in range(SSM_T)])
    rev_im = jnp.stack([pis[SSM_T - 1 - s] for s in range(SSM_T)])
    in_re = rev_re[..., None] * bb_re[None] - rev_im[..., None] * bb_im[None]
    in_im = rev_re[..., None] * bb_im[None] + rev_im[..., None] * bb_re[None]
    win = jnp.stack([in_re, in_im]).reshape(2, SSM_T, nj, SSM_GPB, SSM_STATE, SSM_GROUP)
    w_in = jnp.einsum('csjanq,ab->jsaqcbn', win, eye).reshape(nj, SSM_T * 128, 2 * SSM_GPB * SSM_STATE)
    w_in = w_in.astype(BF16)
    a_t = jnp.stack([pr[SSM_T], pi[SSM_T]]).reshape(2, nj, SSM_GPB * SSM_STATE)
    a_t = jnp.moveaxis(a_t, 0, 1)
    return w_y, w_in, a_t


def _s5_body(u_ref, wy_ref, win_ref, at_ref, y_ref, s_scr, hp_scr, *, nb, nchunk):
    half = SSM_GPB * SSM_STATE
    u = jnp.concatenate([u_ref[:, s, :] for s in range(SSM_T)], axis=1).astype(BF16)
    s_scr[...] = _dot(u, win_ref[...]).reshape(nb, nchunk, 2 * half)
    a_re = at_ref[0:1, :]
    a_im = at_ref[1:2, :]
    h_re = jnp.zeros((nb, half), F32)
    h_im = jnp.zeros((nb, half), F32)
    for c in range(nchunk):
        hp_scr[:, c, :half] = h_re
        hp_scr[:, c, half:] = h_im
        s_re = s_scr[:, c, :half]
        s_im = s_scr[:, c, half:]
        h_re, h_im = (a_re * h_re - a_im * h_im + s_re, a_re * h_im + a_im * h_re + s_im)
    hp = hp_scr[...].reshape(nb * nchunk, 2 * half).astype(BF16)
    y = _dot(jnp.concatenate([u, hp], axis=1), wy_ref[...])
    for t in range(SSM_T):
        y_ref[:, t, :] = y[:, t * 128:(t + 1) * 128]


def _s5(proj, w_y, w_in, a_t, batch, seq):
    n, width = proj.shape
    rows = n // SSM_T
    nchunk = seq // SSM_T
    nb = min(4, batch)
    r = nb * nchunk
    u3 = proj.reshape(rows, SSM_T, width)
    ucol0 = (width - SSM_WIDTH) // 128
    half2 = 2 * SSM_GPB * SSM_STATE
    once = pl.Buffered(1)
    y = pl.pallas_call(
        functools.partial(_s5_body, nb=nb, nchunk=nchunk),
        grid=(SSM_NJ, rows // r),
        in_specs=[
            pl.BlockSpec((r, SSM_T, 128), lambda j, i: (i, 0, ucol0 + j)),
            pl.BlockSpec((None, SSM_T * 128 + half2, SSM_T * 128), lambda j, i: (j, 0, 0), pipeline_mode=once),
            pl.BlockSpec((None, SSM_T * 128, half2), lambda j, i: (j, 0, 0), pipeline_mode=once),
            pl.BlockSpec((None, 2, half2 // 2), lambda j, i: (j, 0, 0)),
        ],
        out_specs=pl.BlockSpec((r, SSM_T, 128), lambda j, i: (i, 0, j)),
        out_shape=jax.ShapeDtypeStruct((rows, SSM_T, SSM_WIDTH), F32),
        scratch_shapes=[pltpu.VMEM((nb, nchunk, half2), F32), pltpu.VMEM((nb, nchunk, half2), F32)],
        compiler_params=_params("arbitrary", "arbitrary"),
        name="s5",
    )(u3, w_y, w_in, a_t)
    return y.reshape(n, SSM_WIDTH)


def _gelu(x):
    return 0.5 * x * (1.0 + jnp.tanh(math.sqrt(2.0 / math.pi) * (x + 0.044715 * (x * x * x))))


def _ab_out_body(ya_ref, ys_ref, u_ref, d_ref, wglu_ref, bglu_ref, woa_ref, wob_ref, h_ref, o_ref):
    y = ys_ref[...] + d_ref[...] * u_ref[...]
    z = _gelu(y)
    gate = jax.nn.sigmoid(_dot(z.astype(BF16), wglu_ref[...]) + bglu_ref[...])
    yb = (z * gate).astype(BF16)
    o_ref[...] = h_ref[...] + (_dot(ya_ref[...], woa_ref[...]) + _dot(yb, wob_ref[...]))


def _ab_out(y_a, y_ssm, proj, d_skip, w_glu, b_glu, w_out, h):
    n, d = h.shape
    tm = min(512, n)
    ucol = (proj.shape[1] - SSM_WIDTH) // SSM_WIDTH
    row = lambda i: (i, 0)
    fixed = lambda i: (0, 0)
    return pl.pallas_call(
        _ab_out_body,
        grid=(n // tm,),
        in_specs=[
            pl.BlockSpec((tm, RET_WIDTH), row),
            pl.BlockSpec((tm, SSM_WIDTH), row),
            pl.BlockSpec((tm, SSM_WIDTH), lambda i: (i, ucol)),
            pl.BlockSpec((1, SSM_WIDTH), fixed),
            pl.BlockSpec((SSM_WIDTH, SSM_WIDTH), fixed),
            pl.BlockSpec((1, SSM_WIDTH), fixed),
            pl.BlockSpec((RET_WIDTH, d), fixed),
            pl.BlockSpec((SSM_WIDTH, d), lambda i: (1, 0)),
            pl.BlockSpec((tm, d), row),
        ],
        out_specs=pl.BlockSpec((tm, d), row),
        out_shape=jax.ShapeDtypeStruct((n, d), F32),
        compiler_params=_params("parallel"),
        name="ab_out",
    )(y_a, y_ssm, proj, d_skip.reshape(1, -1), w_glu, b_glu.reshape(1, -1), w_out, w_out, h)


def _att_body(lam_ref, q_ref, k_ref, v_ref, sub_ref, o_ref, *, seq, lambda_init):
    tq = min(ATT_QBLOCK, seq)
    dh = DIFF_HEAD_DIM
    scale = dh ** -0.5
    lam = lam_ref[0]
    neg = jnp.finfo(F32).min
    rc = lax.broadcasted_iota(jnp.int32, (tq, tq), 0) // CHUNK
    cc = lax.broadcasted_iota(jnp.int32, (tq, tq), 1) // CHUNK
    visible = rc >= cc
    for qb in range(seq // tq):
        q0 = qb * tq
        q = q_ref[q0:q0 + tq, :]
        v_diag = v_ref[q0:q0 + tq, :]
        comps = []
        for c in range(2):
            qc = q[:, c * dh:(c + 1) * dh]
            s_diag = _dot_nt(qc, k_ref[q0:q0 + tq, c * dh:(c + 1) * dh]) * scale
            s_diag = jnp.where(visible, s_diag, neg)
            m = jnp.max(s_diag, axis=-1, keepdims=True)
            if qb > 0:
                s_off = _dot_nt(qc, k_ref[0:q0, c * dh:(c + 1) * dh]) * scale
                m = jnp.maximum(m, jnp.max(s_off, axis=-1, keepdims=True))
            p_diag = jnp.exp(s_diag - m)
            l = jnp.sum(p_diag, axis=-1, keepdims=True)
            acc = _dot(p_diag.astype(BF16), v_diag)
            if qb > 0:
                p_off = jnp.exp(s_off - m)
                l = l + jnp.sum(p_off, axis=-1, keepdims=True)
                acc = acc + _dot(p_off.astype(BF16), v_ref[0:q0, :])
            comps.append(acc / l)
        o = comps[0] - lam * comps[1]
        o = _rms(o, sub_ref[...]) * (1.0 - lambda_init)
        o_ref[q0:q0 + tq, :] = o.astype(BF16)


def _attention(qkv, lam, subln, batch, seq, lambda_init):
    n = qkv.shape[0]
    w = 2 * DIFF_HEAD_DIM

    def col(off):
        return pl.BlockSpec((seq, w), lambda b, h: (b, off + h))

    return pl.pallas_call(
        functools.partial(_att_body, seq=seq, lambda_init=lambda_init),
        grid=(batch, DIFF_HEADS),
        in_specs=[
            pl.BlockSpec(memory_space=pltpu.SMEM),
            col(0), col(DIFF_HEADS), col(2 * DIFF_HEADS),
            pl.BlockSpec((1, w), lambda b, h: (0, 0)),
        ],
        out_specs=pl.BlockSpec((seq, w), lambda b, h: (b, h)),
        out_shape=jax.ShapeDtypeStruct((n, D_MODEL), BF16),
        compiler_params=_params("parallel", "parallel"),
        name="diff_attention",
    )(lam.reshape(1), qkv, qkv, qkv, subln.reshape(1, w))


def _out_proj_body(o_ref, w_ref, h_ref, y_ref):
    y_ref[...] = h_ref[...] + _dot(o_ref[...], w_ref[...])


def _out_proj(o, w, h):
    n, d = h.shape
    tm = min(512, n)
    return pl.pallas_call(
        _out_proj_body,
        grid=(n // tm,),
        in_specs=[
            pl.BlockSpec((tm, d), lambda i: (i, 0)),
            pl.BlockSpec((d, d), lambda i: (0, 0)),
            pl.BlockSpec((tm, d), lambda i: (i, 0)),
        ],
        out_specs=pl.BlockSpec((tm, d), lambda i: (i, 0)),
        out_shape=jax.ShapeDtypeStruct((n, d), F32),
        compiler_params=_params("parallel"),
        name="out_proj",
    )(o, w, h)


def _rope_tables(seq, rot_dim, theta):
    inv = 1.0 / (theta ** (jnp.arange(0, rot_dim, 2, dtype=F32) / rot_dim))
    ang = jnp.arange(seq, dtype=F32)[:, None] * inv[None, :]
    return jnp.cos(ang), jnp.sin(ang)


def _pad_ff(w, axis):
    pad = [(0, 0), (0, 0)]
    pad[axis] = (0, D_FF_PAD - D_FF)
    return jnp.pad(w.astype(BF16), pad)


def kernel(x, ffn_norm, ffn_w_gate, ffn_w_up, ffn_w_down, mix_norm, ab_w_in, ab_w_out, ssm_lambda_re, ssm_lambda_im, ssm_log_step, ssm_b_re, ssm_b_im, ssm_c_re, ssm_c_im, ssm_d, ssm_w_glu, ssm_b_glu, c_w_qkv, c_w_out, c_lambda_q1, c_lambda_k1, c_lambda_q2, c_lambda_k2, c_subln, final_norm):
    batch, seq, d = x.shape
    n = batch * seq
    h = x.reshape(n, d)

    def ffn(h, layer, half, final=False):
        return _ffn(h, ffn_norm[layer, half], _pad_ff(ffn_w_gate[layer, half], 1),
                    _pad_ff(ffn_w_up[layer, half], 1), _pad_ff(ffn_w_down[layer, half], 0),
                    final_norm, final=final)

    h = ffn(h, 0, 0)
    ret_cos, ret_sin = _rope_tables(seq, RET_HEAD_DIM, RET_THETA)
    proj = _proj_ab(h, mix_norm[0], ab_w_in[0].astype(BF16), ret_cos, ret_sin, seq)
    y_a = _retention(proj, batch, seq)
    w_y, w_in, a_t = _s5_weights(ssm_lambda_re[0], ssm_lambda_im[0], ssm_log_step[0], ssm_b_re[0],
                                 ssm_b_im[0], ssm_c_re[0], ssm_c_im[0])
    y_ssm = _s5(proj, w_y, w_in, a_t, batch, seq)
    h = _ab_out(y_a, y_ssm, proj, ssm_d[0], ssm_w_glu[0].astype(BF16), ssm_b_glu[0],
                ab_w_out[0].astype(BF16), h)
    h = ffn(h, 0, 1)

    h = ffn(h, 1, 0)
    att_cos, att_sin = _rope_tables(seq, 2 * ROT_HALF, ROPE_THETA)
    zeros = jnp.zeros((seq, DIFF_HEAD_DIM - 2 * ROT_HALF), F32)
    zhalf = jnp.zeros((seq, ROT_HALF), F32)
    cf = jnp.concatenate([att_cos, att_cos, zeros + 1.0], axis=1)
    sa = jnp.concatenate([zhalf, att_sin, zeros], axis=1)
    sb = jnp.concatenate([-att_sin, zhalf, zeros], axis=1)
    qkv = _proj_c(h, mix_norm[1], c_w_qkv[0].astype(BF16), cf, sa, sb, seq)
    lambda_init = 0.8 - 0.6 * math.exp(-0.3 * 1)
    lam = (jnp.exp(jnp.sum(c_lambda_q1[0] * c_lambda_k1[0]))
           - jnp.exp(jnp.sum(c_lambda_q2[0] * c_lambda_k2[0])) + lambda_init)
    o = _attention(qkv, lam, c_subln[0], batch, seq, lambda_init)
    h = _out_proj(o, c_w_out[0].astype(BF16), h)
    h = ffn(h, 1, 1, final=True)
    return h.reshape(batch, seq, d)
```

```python
import functools
import math

import jax
import jax.numpy as jnp
from jax import lax
from jax.experimental import pallas as pl
from jax.experimental.pallas import tpu as pltpu

F32 = jnp.float32
BF16 = jnp.bfloat16

D_MODEL = 2048
CHUNK = 64
NORM_EPS = 1e-6

RET_WIDTH = 1024
RET_HEADS = 4
RET_HEAD_DIM = 256
RET_THETA = 10000.0
RET_BLOCK = 256

SSM_WIDTH = 1024
SSM_GROUP = 16
SSM_GROUPS = 64
SSM_STATE = 64
SSM_T = 16
SSM_GPB = 128 // SSM_GROUP
SSM_NJ = SSM_WIDTH // 128

DIFF_HEAD_DIM = 128
DIFF_HEADS = 8
ROPE_THETA = 500000.0
ROT_HALF = DIFF_HEAD_DIM // 4 // 2
ATT_QBLOCK = 256

FF_TILE = 512
CAST_ROWS = 512
PROJ_TN = 1024
VMEM_LIMIT = 56 * 1024 * 1024


def _params(*sem):
    return pltpu.CompilerParams(dimension_semantics=sem, vmem_limit_bytes=VMEM_LIMIT)


def _rms(x, g):
    y = x * lax.rsqrt(jnp.mean(x * x, axis=-1, keepdims=True) + NORM_EPS)
    return y * g


def _dot(a, b):
    return jnp.dot(a, b, preferred_element_type=F32)


def _dot_nt(a, b):
    return lax.dot_general(a, b, (((1,), (1,)), ((), ())), preferred_element_type=F32)


def _dot_tn(a, b):
    return lax.dot_general(a, b, (((0,), (0,)), ((), ())), preferred_element_type=F32)


def _cast_body(w_ref, o_ref):
    o_ref[...] = w_ref[...].astype(BF16)


def _cast_stack(w):
    a, b, r, c = w.shape
    spec = pl.BlockSpec((None, None, CAST_ROWS, c), lambda s, i: (s // b, s % b, i, 0))
    return pl.pallas_call(
        _cast_body,
        grid=(a * b, pl.cdiv(r, CAST_ROWS)),
        in_specs=[spec],
        out_specs=spec,
        out_shape=jax.ShapeDtypeStruct(w.shape, BF16),
        compiler_params=_params("parallel", "parallel"),
        name="cast_bf16",
    )(w)


def _ffn_body(x_ref, g_ref, wg_ref, wu_ref, wd_ref, fn_ref, o_ref, xn_ref, acc_ref, *, final, tail):
    j = pl.program_id(1)
    last = pl.num_programs(1) - 1

    def tile(width):
        xn = xn_ref[...]
        gate = _dot(xn, wg_ref[:, :width])
        up = _dot(xn, wu_ref[:, :width])
        act = (gate * jax.nn.sigmoid(gate) * up).astype(BF16)
        return _dot(act, wd_ref[:width, :])

    @pl.when(j == 0)
    def _():
        xn_ref[...] = _rms(x_ref[...], g_ref[...]).astype(BF16)
        acc_ref[...] = tile(FF_TILE)

    @pl.when((j > 0) & (j < last))
    def _():
        acc_ref[...] += tile(FF_TILE)

    @pl.when(j == last)
    def _():
        h = x_ref[...] + 0.5 * (acc_ref[...] + tile(tail))
        if final:
            h = _rms(h, fn_ref[...])
        o_ref[...] = h


def _ffn(h, norm_g, wg, wu, wd, final_g, layer, half, *, final):
    n, d = h.shape
    tm = min(512, n)
    ff = wg.shape[-1]
    steps = pl.cdiv(ff, FF_TILE)
    tail = ff - (steps - 1) * FF_TILE
    return pl.pallas_call(
        functools.partial(_ffn_body, final=final, tail=tail),
        grid=(n // tm, steps),
        in_specs=[
            pl.BlockSpec((tm, d), lambda i, j: (i, 0)),
            pl.BlockSpec((1, d), lambda i, j: (0, 0)),
            pl.BlockSpec((None, None, d, FF_TILE), lambda i, j: (layer, half, 0, j)),
            pl.BlockSpec((None, None, d, FF_TILE), lambda i, j: (layer, half, 0, j)),
            pl.BlockSpec((None, None, FF_TILE, d), lambda i, j: (layer, half, j, 0)),
            pl.BlockSpec((1, d), lambda i, j: (0, 0)),
        ],
        out_specs=pl.BlockSpec((tm, d), lambda i, j: (i, 0)),
        out_shape=jax.ShapeDtypeStruct((n, d), F32),
        scratch_shapes=[pltpu.VMEM((tm, d), BF16), pltpu.VMEM((tm, d), F32)],
        compiler_params=_params("parallel", "arbitrary"),
        name="ffn",
    )(h, norm_g.reshape(1, d), wg, wu, wd, final_g.reshape(1, d))


def _proj_ab_body(x_ref, g_ref, w_ref, cos_ref, sin_ref, o_ref, xn_ref):
    @pl.when(pl.program_id(1) == 0)
    def _():
        xn_ref[...] = _rms(x_ref[...], g_ref[...]).astype(BF16)

    y = _dot(xn_ref[...], w_ref[...])
    c = cos_ref[...]
    s = sin_ref[...]
    half = RET_HEAD_DIM // 2
    parts = []
    for hd in range(PROJ_TN // RET_HEAD_DIM):
        lo = hd * RET_HEAD_DIM
        x1 = y[:, lo:lo + half]
        x2 = y[:, lo + half:lo + 2 * half]
        parts += [x1 * c - x2 * s, x2 * c + x1 * s]
    o_ref[...] = jnp.concatenate(parts, axis=1).reshape(o_ref.shape)


def _proj_ab(h, norm_g, w, cos_tab, sin_tab, seq):
    n, d = h.shape
    nout = w.shape[1]
    tm = min(1024, seq)
    per_seq = seq // tm
    tab = pl.BlockSpec((None, tm, RET_HEAD_DIM // 2), lambda i, j: (jnp.minimum(j, 2), i % per_seq, 0))
    return pl.pallas_call(
        _proj_ab_body,
        grid=(n // tm, nout // PROJ_TN),
        in_specs=[
            pl.BlockSpec((tm, d), lambda i, j: (i, 0)),
            pl.BlockSpec((1, d), lambda i, j: (0, 0)),
            pl.BlockSpec((d, PROJ_TN), lambda i, j: (0, j)),
            tab, tab,
        ],
        out_specs=pl.BlockSpec((tm // SSM_T, SSM_T, PROJ_TN), lambda i, j: (i, 0, j)),
        out_shape=jax.ShapeDtypeStruct((n // SSM_T, SSM_T, nout), F32),
        scratch_shapes=[pltpu.VMEM((tm, d), BF16)],
        compiler_params=_params("parallel", "arbitrary"),
        name="proj_ab",
    )(h, norm_g.reshape(1, d), w, cos_tab, sin_tab)


def _proj_c_body(x_ref, g_ref, w_ref, cf_ref, sa_ref, sb_ref, o_ref, xn_ref):
    @pl.when(pl.program_id(1) == 0)
    def _():
        xn_ref[...] = _rms(x_ref[...], g_ref[...]).astype(BF16)

    y = _dot(xn_ref[...], w_ref[...])
    cf = cf_ref[...]
    sa = sa_ref[...]
    sb = sb_ref[...]
    parts = []
    for hd in range(PROJ_TN // DIFF_HEAD_DIM):
        lo = hd * DIFF_HEAD_DIM
        yh = y[:, lo:lo + DIFF_HEAD_DIM]
        r = (yh * cf + pltpu.roll(yh, ROT_HALF, axis=1) * sa
             + pltpu.roll(yh, DIFF_HEAD_DIM - ROT_HALF, axis=1) * sb)
        parts.append(r.astype(BF16))
    o_ref[...] = jnp.concatenate(parts, axis=1)


def _proj_c(h, norm_g, w, cf, sa, sb, seq):
    n, d = h.shape
    nout = w.shape[1]
    tm = min(1024, seq)
    per_seq = seq // tm
    qk_tiles = 2 * D_MODEL // PROJ_TN
    tab = pl.BlockSpec((None, tm, DIFF_HEAD_DIM), lambda i, j: (j // qk_tiles, i % per_seq, 0))
    return pl.pallas_call(
        _proj_c_body,
        grid=(n // tm, nout // PROJ_TN),
        in_specs=[
            pl.BlockSpec((tm, d), lambda i, j: (i, 0)),
            pl.BlockSpec((1, d), lambda i, j: (0, 0)),
            pl.BlockSpec((d, PROJ_TN), lambda i, j: (0, j)),
            tab, tab, tab,
        ],
        out_specs=pl.BlockSpec((tm, PROJ_TN), lambda i, j: (i, j)),
        out_shape=jax.ShapeDtypeStruct((n, nout), BF16),
        scratch_shapes=[pltpu.VMEM((tm, d), BF16)],
        compiler_params=_params("parallel", "arbitrary"),
        name="proj_c",
    )(h, norm_g.reshape(1, d), w, cf, sa, sb)


def _ret_body(q_ref, k_ref, v_ref, g_ref, dm_ref, qd_ref, kd_ref, cd_ref, o_ref, *, nblk):
    t = RET_BLOCK
    t3 = t // SSM_T
    dmask = dm_ref[...]
    qdec = qd_ref[...]
    kdec = kd_ref[...]
    cdec = cd_ref[...]

    def rows(ref, n):
        r0 = pl.multiple_of(n * t3, t3)
        return ref[pl.ds(r0, t3), :, :].reshape(t, RET_HEAD_DIM)

    def step(n, state):
        q = rows(q_ref, n)
        k = rows(k_ref, n)
        vb = rows(v_ref, n).astype(BF16)
        scores = _dot_nt(q.astype(BF16), k.astype(BF16)) * dmask
        intra = _dot(scores.astype(BF16), vb)
        inter = _dot((q * qdec).astype(BF16), state.astype(BF16))
        new_state = state * cdec + _dot_tn((k * kdec).astype(BF16), vb)
        o = intra + inter
        o = o * lax.rsqrt(jnp.mean(o * o, axis=-1, keepdims=True) + NORM_EPS)
        gate = rows(g_ref, n)
        r0 = pl.multiple_of(n * t, t)
        o_ref[pl.ds(r0, t), :] = (o * (gate * jax.nn.sigmoid(gate))).astype(BF16)
        return new_state

    lax.fori_loop(0, nblk, step, jnp.zeros((RET_HEAD_DIM, RET_HEAD_DIM), F32))


def _retention(proj3, batch, seq):
    n = proj3.shape[0] * SSM_T
    t = RET_BLOCK
    hd = RET_HEAD_DIM
    log_g = jnp.log(1.0 - 2.0 ** (-5.0 - jnp.arange(RET_HEADS, dtype=F32)))
    idx = jnp.arange(t, dtype=F32)
    chunk = jnp.arange(t) // CHUNK
    visible = chunk[:, None] >= chunk[None, :]
    dist = jnp.abs(idx[:, None] - idx[None, :])
    dmask = jnp.where(visible[None], jnp.exp(log_g[:, None, None] * dist[None]), 0.0)
    qdec = jnp.broadcast_to(jnp.exp(log_g[:, None] * (idx[None] + 1.0))[:, :, None], (RET_HEADS, t, hd))
    kdec = jnp.broadcast_to(jnp.exp(log_g[:, None] * (t - 1.0 - idx[None]))[:, :, None], (RET_HEADS, t, hd))
    cdec = jnp.broadcast_to(jnp.exp(log_g * t)[:, None, None], (RET_HEADS, 1, hd))

    def col(off):
        return pl.BlockSpec((seq // SSM_T, SSM_T, hd), lambda b, h: (b, 0, off + h))

    def tab(rows, cols):
        return pl.BlockSpec((None, rows, cols), lambda b, h: (h, 0, 0))

    return pl.pallas_call(
        functools.partial(_ret_body, nblk=seq // t),
        grid=(batch, RET_HEADS),
        in_specs=[col(0), col(RET_HEADS), col(2 * RET_HEADS), col(3 * RET_HEADS),
                  tab(t, t), tab(t, hd), tab(t, hd), tab(1, hd)],
        out_specs=pl.BlockSpec((seq, hd), lambda b, h: (b, h)),
        out_shape=jax.ShapeDtypeStruct((n, RET_WIDTH), BF16),
        compiler_params=_params("parallel", "parallel"),
        name="retention",
    )(proj3, proj3, proj3, proj3, dmask, qdec, kdec, cdec)


def _s5_weights(lam_re, lam_im, log_step, b_re, b_im, c_re, c_im):
    hp = lax.Precision.HIGHEST
    step = jnp.exp(log_step)[:, None]
    mag = jnp.exp(lam_re * step)
    a_re = mag * jnp.cos(lam_im * step)
    a_im = mag * jnp.sin(lam_im * step)
    den = lam_re * lam_re + lam_im * lam_im
    nr = a_re - 1.0
    f_re = (nr * lam_re + a_im * lam_im) / den
    f_im = (a_im * lam_re - nr * lam_im) / den
    bb_re = f_re[..., None] * b_re - f_im[..., None] * b_im
    bb_im = f_re[..., None] * b_im + f_im[..., None] * b_re
    prs, pis = [jnp.ones_like(a_re)], [jnp.zeros_like(a_im)]
    for _ in range(SSM_T):
        prs.append(prs[-1] * a_re - pis[-1] * a_im)
        pis.append(prs[-2] * a_im + pis[-1] * a_re)
    pr = jnp.stack(prs)
    pi = jnp.stack(pis)
    ca_re = c_re[None] * pr[:, :, None, :] - c_im[None] * pi[:, :, None, :]
    ca_im = c_re[None] * pi[:, :, None, :] + c_im[None] * pr[:, :, None, :]
    kern = (jnp.einsum('kgpn,gnq->gkpq', ca_re[:SSM_T], bb_re, precision=hp)
            - jnp.einsum('kgpn,gnq->gkpq', ca_im[:SSM_T], bb_im, precision=hp))
    lag = jnp.arange(SSM_T)[None, :] - jnp.arange(SSM_T)[:, None]
    toe = jnp.where((lag >= 0)[None, :, :, None, None], kern[:, jnp.clip(lag, 0)], 0.0)
    eye = jnp.eye(SSM_GPB, dtype=F32)
    nj = SSM_NJ
    toe = toe.reshape(nj, SSM_GPB, SSM_T, SSM_T, SSM_GROUP, SSM_GROUP)
    m_intra = jnp.einsum('jastpq,ab->jsaqtbp', toe, eye).reshape(nj, SSM_T * 128, SSM_T * 128)
    ro_re = jnp.moveaxis(ca_re[1:], 0, 2)
    ro_im = -jnp.moveaxis(ca_im[1:], 0, 2)
    ro = jnp.stack([ro_re, ro_im]).reshape(2, nj, SSM_GPB, SSM_GROUP, SSM_T, SSM_STATE)
    m_out = jnp.einsum('cjaptn,ab->jcantbp', ro, eye).reshape(nj, 2 * SSM_GPB * SSM_STATE, SSM_T * 128)
    w_y = jnp.concatenate([m_intra, m_out], axis=1).astype(BF16)
    rev_re = jnp.stack([prs[SSM_T - 1 - s] for s in range(SSM_T)])
    rev_im = jnp.stack([pis[SSM_T - 1 - s] for s in range(SSM_T)])
    in_re = rev_re[..., None] * bb_re[None] - rev_im[..., None] * bb_im[None]
    in_im = rev_re[..., None] * bb_im[None] + rev_im[..., None] * bb_re[None]
    win = jnp.stack([in_re, in_im]).reshape(2, SSM_T, nj, SSM_GPB, SSM_STATE, SSM_GROUP)
    w_in = jnp.einsum('csjanq,ab->jsaqcbn', win, eye).reshape(nj, SSM_T * 128, 2 * SSM_GPB * SSM_STATE)
    w_in = w_in.astype(BF16)
    a_t = jnp.stack([pr[SSM_T], pi[SSM_T]]).reshape(2, nj, SSM_GPB * SSM_STATE)
    a_t = jnp.moveaxis(a_t, 0, 1)
    return w_y, w_in, a_t


def _s5_body(u_ref, wy_ref, win_ref, at_ref, y_ref, s_scr, hp_scr, *, nb, nchunk):
    half = SSM_GPB * SSM_STATE
    u = jnp.concatenate([u_ref[:, s, :] for s in range(SSM_T)], axis=1).astype(BF16)
    s_scr[...] = _dot(u, win_ref[...]).reshape(nb, nchunk, 2 * half)
    a_re = at_ref[0:1, :]
    a_im = at_ref[1:2, :]
    h_re = jnp.zeros((nb, half), F32)
    h_im = jnp.zeros((nb, half), F32)
    for c in range(nchunk):
        hp_scr[:, c, :half] = h_re
        hp_scr[:, c, half:] = h_im
        s_re = s_scr[:, c, :half]
        s_im = s_scr[:, c, half:]
        h_re, h_im = (a_re * h_re - a_im * h_im + s_re, a_re * h_im + a_im * h_re + s_im)
    hp = hp_scr[...].reshape(nb * nchunk, 2 * half).astype(BF16)
    y = _dot(jnp.concatenate([u, hp], axis=1), wy_ref[...])
    for t in range(SSM_T):
        y_ref[:, t, :] = y[:, t * 128:(t + 1) * 128]


def _s5(proj3, w_y, w_in, a_t, batch, seq):
    rows, _, width = proj3.shape
    nchunk = seq // SSM_T
    nb = min(4, batch)
    r = nb * nchunk
    ucol0 = (width - SSM_WIDTH) // 128
    half2 = 2 * SSM_GPB * SSM_STATE
    once = pl.Buffered(1)
    return pl.pallas_call(
        functools.partial(_s5_body, nb=nb, nchunk=nchunk),
        grid=(SSM_NJ, rows // r),
        in_specs=[
            pl.BlockSpec((r, SSM_T, 128), lambda j, i: (i, 0, ucol0 + j)),
            pl.BlockSpec((None, SSM_T * 128 + half2, SSM_T * 128), lambda j, i: (j, 0, 0), pipeline_mode=once),
            pl.BlockSpec((None, SSM_T * 128, half2), lambda j, i: (j, 0, 0), pipeline_mode=once),
            pl.BlockSpec((None, 2, half2 // 2), lambda j, i: (j, 0, 0)),
        ],
        out_specs=pl.BlockSpec((r, SSM_T, 128), lambda j, i: (i, 0, j)),
        out_shape=jax.ShapeDtypeStruct((rows, SSM_T, SSM_WIDTH), F32),
        scratch_shapes=[pltpu.VMEM((nb, nchunk, half2), F32), pltpu.VMEM((nb, nchunk, half2), F32)],
        compiler_params=_params("arbitrary", "arbitrary"),
        name="s5",
    )(proj3, w_y, w_in, a_t)


def _gelu(x):
    return 0.5 * x * (1.0 + jnp.tanh(math.sqrt(2.0 / math.pi) * (x + 0.044715 * (x * x * x))))


def _ab_out_body(ya_ref, ys_ref, u_ref, d_ref, wglu_ref, bglu_ref, woa_ref, wob_ref, h_ref, o_ref):
    tm = h_ref.shape[0]
    y = (ys_ref[...] + d_ref[...] * u_ref[...]).reshape(tm, SSM_WIDTH)
    z = _gelu(y)
    gate = jax.nn.sigmoid(_dot(z.astype(BF16), wglu_ref[...]) + bglu_ref[...])
    yb = (z * gate).astype(BF16)
    o_ref[...] = h_ref[...] + (_dot(ya_ref[...], woa_ref[...]) + _dot(yb, wob_ref[...]))


def _ab_out(y_a, y_ssm3, proj3, d_skip, w_glu, b_glu, w_out, h):
    n, d = h.shape
    tm = min(512, n)
    ucol = (proj3.shape[2] - SSM_WIDTH) // SSM_WIDTH
    row = lambda i: (i, 0)
    fixed = lambda i: (0, 0)
    return pl.pallas_call(
        _ab_out_body,
        grid=(n // tm,),
        in_specs=[
            pl.BlockSpec((tm, RET_WIDTH), row),
            pl.BlockSpec((tm // SSM_T, SSM_T, SSM_WIDTH), lambda i: (i, 0, 0)),
            pl.BlockSpec((tm // SSM_T, SSM_T, SSM_WIDTH), lambda i: (i, 0, ucol)),
            pl.BlockSpec((1, 1, SSM_WIDTH), lambda i: (0, 0, 0)),
            pl.BlockSpec((SSM_WIDTH, SSM_WIDTH), fixed),
            pl.BlockSpec((1, SSM_WIDTH), fixed),
            pl.BlockSpec((RET_WIDTH, d), fixed),
            pl.BlockSpec((SSM_WIDTH, d), lambda i: (1, 0)),
            pl.BlockSpec((tm, d), row),
        ],
        out_specs=pl.BlockSpec((tm, d), row),
        out_shape=jax.ShapeDtypeStruct((n, d), F32),
        compiler_params=_params("parallel"),
        name="ab_out",
    )(y_a, y_ssm3, proj3, d_skip.reshape(1, 1, -1), w_glu, b_glu.reshape(1, -1), w_out, w_out, h)


def _att_body(lam_ref, q_ref, k_ref, v_ref, sub_ref, o_ref, *, seq, lambda_init):
    tq = min(ATT_QBLOCK, seq)
    dh = DIFF_HEAD_DIM
    scale = dh ** -0.5
    lam = lam_ref[0]
    neg = jnp.finfo(F32).min
    rc = lax.broadcasted_iota(jnp.int32, (tq, tq), 0) // CHUNK
    cc = lax.broadcasted_iota(jnp.int32, (tq, tq), 1) // CHUNK
    visible = rc >= cc
    for qb in range(seq // tq):
        q0 = qb * tq
        q = q_ref[q0:q0 + tq, :]
        v_diag = v_ref[q0:q0 + tq, :]
        comps = []
        for c in range(2):
            qc = q[:, c * dh:(c + 1) * dh]
            s_diag = _dot_nt(qc, k_ref[q0:q0 + tq, c * dh:(c + 1) * dh]) * scale
            s_diag = jnp.where(visible, s_diag, neg)
            m = jnp.max(s_diag, axis=-1, keepdims=True)
            if qb > 0:
                s_off = _dot_nt(qc, k_ref[0:q0, c * dh:(c + 1) * dh]) * scale
                m = jnp.maximum(m, jnp.max(s_off, axis=-1, keepdims=True))
            p_diag = jnp.exp(s_diag - m)
            l = jnp.sum(p_diag, axis=-1, keepdims=True)
            acc = _dot(p_diag.astype(BF16), v_diag)
            if qb > 0:
                p_off = jnp.exp(s_off - m)
                l = l + jnp.sum(p_off, axis=-1, keepdims=True)
                acc = acc + _dot(p_off.astype(BF16), v_ref[0:q0, :])
            comps.append(acc / l)
        o = comps[0] - lam * comps[1]
        o = _rms(o, sub_ref[...]) * (1.0 - lambda_init)
        o_ref[q0:q0 + tq, :] = o.astype(BF16)


def _attention(qkv, lam, subln, batch, seq, lambda_init):
    n = qkv.shape[0]
    w = 2 * DIFF_HEAD_DIM

    def col(off):
        return pl.BlockSpec((seq, w), lambda b, h: (b, off + h))

    return pl.pallas_call(
        functools.partial(_att_body, seq=seq, lambda_init=lambda_init),
        grid=(batch, DIFF_HEADS),
        in_specs=[
            pl.BlockSpec(memory_space=pltpu.SMEM),
            col(0), col(DIFF_HEADS), col(2 * DIFF_HEADS),
            pl.BlockSpec((1, w), lambda b, h: (0, 0)),
        ],
        out_specs=pl.BlockSpec((seq, w), lambda b, h: (b, h)),
        out_shape=jax.ShapeDtypeStruct((n, D_MODEL), BF16),
        compiler_params=_params("parallel", "parallel"),
        name="diff_attention",
    )(lam.reshape(1), qkv, qkv, qkv, subln.reshape(1, w))


def _out_proj_body(o_ref, w_ref, h_ref, y_ref):
    y_ref[...] = h_ref[...] + _dot(o_ref[...], w_ref[...])


def _out_proj(o, w, h):
    n, d = h.shape
    tm = min(512, n)
    return pl.pallas_call(
        _out_proj_body,
        grid=(n // tm,),
        in_specs=[
            pl.BlockSpec((tm, d), lambda i: (i, 0)),
            pl.BlockSpec((d, d), lambda i: (0, 0)),
            pl.BlockSpec((tm, d), lambda i: (i, 0)),
        ],
        out_specs=pl.BlockSpec((tm, d), lambda i: (i, 0)),
        out_shape=jax.ShapeDtypeStruct((n, d), F32),
        compiler_params=_params("parallel"),
        name="out_proj",
    )(o, w, h)


def _rope_tables(seq, rot_dim, theta):
    inv = 1.0 / (theta ** (jnp.arange(0, rot_dim, 2, dtype=F32) / rot_dim))
    ang = jnp.arange(seq, dtype=F32)[:, None] * inv[None, :]
    return jnp.cos(ang), jnp.sin(ang)


def kernel(x, ffn_norm, ffn_w_gate, ffn_w_up, ffn_w_down, mix_norm, ab_w_in, ab_w_out, ssm_lambda_re, ssm_lambda_im, ssm_log_step, ssm_b_re, ssm_b_im, ssm_c_re, ssm_c_im, ssm_d, ssm_w_glu, ssm_b_glu, c_w_qkv, c_w_out, c_lambda_q1, c_lambda_k1, c_lambda_q2, c_lambda_k2, c_subln, final_norm):
    batch, seq, d = x.shape
    n = batch * seq
    h = x.reshape(n, d)
    wg = _cast_stack(ffn_w_gate)
    wu = _cast_stack(ffn_w_up)
    wd = _cast_stack(ffn_w_down)

    def ffn(h, layer, half, final=False):
        return _ffn(h, ffn_norm[layer, half], wg, wu, wd, final_norm, layer, half, final=final)

    h = ffn(h, 0, 0)
    ret_cos, ret_sin = _rope_tables(seq, RET_HEAD_DIM, RET_THETA)
    k_scale = RET_HEAD_DIM ** -0.5
    cos_tab = jnp.stack([ret_cos, ret_cos * k_scale, jnp.ones_like(ret_cos)])
    sin_tab = jnp.stack([ret_sin, ret_sin * k_scale, jnp.zeros_like(ret_sin)])
    proj3 = _proj_ab(h, mix_norm[0], ab_w_in[0].astype(BF16), cos_tab, sin_tab, seq)
    y_a = _retention(proj3, batch, seq)
    w_y, w_in, a_t = _s5_weights(ssm_lambda_re[0], ssm_lambda_im[0], ssm_log_step[0], ssm_b_re[0],
                                 ssm_b_im[0], ssm_c_re[0], ssm_c_im[0])
    y_ssm3 = _s5(proj3, w_y, w_in, a_t, batch, seq)
    h = _ab_out(y_a, y_ssm3, proj3, ssm_d[0], ssm_w_glu[0].astype(BF16), ssm_b_glu[0],
                ab_w_out[0].astype(BF16), h)
    h = ffn(h, 0, 1)

    h = ffn(h, 1, 0)
    att_cos, att_sin = _rope_tables(seq, 2 * ROT_HALF, ROPE_THETA)
    rest = jnp.zeros((seq, DIFF_HEAD_DIM - 2 * ROT_HALF), F32)
    zhalf = jnp.zeros((seq, ROT_HALF), F32)
    ident = jnp.zeros((seq, DIFF_HEAD_DIM), F32)
    cf = jnp.stack([jnp.concatenate([att_cos, att_cos, rest + 1.0], axis=1), ident + 1.0])
    sa = jnp.stack([jnp.concatenate([zhalf, att_sin, rest], axis=1), ident])
    sb = jnp.stack([jnp.concatenate([-att_sin, zhalf, rest], axis=1), ident])
    qkv = _proj_c(h, mix_norm[1], c_w_qkv[0].astype(BF16), cf, sa, sb, seq)
    lambda_init = 0.8 - 0.6 * math.exp(-0.3 * 1)
    lam = (jnp.exp(jnp.sum(c_lambda_q1[0] * c_lambda_k1[0]))
           - jnp.exp(jnp.sum(c_lambda_q2[0] * c_lambda_k2[0])) + lambda_init)
    o = _attention(qkv, lam, c_subln[0], batch, seq, lambda_init)
    h = _out_proj(o, c_w_out[0].astype(BF16), h)
    h = ffn(h, 1, 1, final=True)
    return h.reshape(batch, seq, d)
```

```python
import functools
import math

import jax
import jax.numpy as jnp
from jax import lax
from jax.experimental import pallas as pl
from jax.experimental.pallas import tpu as pltpu

F32 = jnp.float32
BF16 = jnp.bfloat16

D_MODEL = 2048
CHUNK = 64
NORM_EPS = 1e-6

RET_WIDTH = 1024
RET_HEADS = 4
RET_HEAD_DIM = 256
RET_THETA = 10000.0
RET_BLOCK = 256

SSM_WIDTH = 1024
SSM_GROUP = 16
SSM_GROUPS = 64
SSM_STATE = 64
SSM_T = 16
SSM_GPB = 128 // SSM_GROUP
SSM_NJ = SSM_WIDTH // 128

DIFF_HEAD_DIM = 128
DIFF_HEADS = 8
ROPE_THETA = 500000.0
ROT_HALF = DIFF_HEAD_DIM // 4 // 2
ATT_QBLOCK = 256

FF_TILE = 512
CAST_ROWS = 512
PROJ_TN = 1024
VMEM_LIMIT = 56 * 1024 * 1024


def _params(*sem):
    return pltpu.CompilerParams(dimension_semantics=sem, vmem_limit_bytes=VMEM_LIMIT)


def _rms(x, g):
    y = x * lax.rsqrt(jnp.mean(x * x, axis=-1, keepdims=True) + NORM_EPS)
    return y * g


def _dot(a, b):
    return jnp.dot(a, b, preferred_element_type=F32)


def _dot_nt(a, b):
    return lax.dot_general(a, b, (((1,), (1,)), ((), ())), preferred_element_type=F32)


def _dot_tn(a, b):
    return lax.dot_general(a, b, (((0,), (0,)), ((), ())), preferred_element_type=F32)


def _cast_body(w_ref, o_ref):
    o_ref[...] = w_ref[...].astype(BF16)


def _cast_stack(w):
    a, b, r, c = w.shape
    spec = pl.BlockSpec((None, None, CAST_ROWS, c), lambda s, i: (s // b, s % b, i, 0))
    return pl.pallas_call(
        _cast_body,
        grid=(a * b, pl.cdiv(r, CAST_ROWS)),
        in_specs=[spec],
        out_specs=spec,
        out_shape=jax.ShapeDtypeStruct(w.shape, BF16),
        compiler_params=_params("parallel", "parallel"),
        name="cast_bf16",
    )(w)


def _ffn_body(x_ref, g_ref, wg_ref, wu_ref, wd_ref, fn_ref, o_ref, xn_ref, acc_ref, *, final, tail):
    j = pl.program_id(1)
    last = pl.num_programs(1) - 1

    def tile(width):
        xn = xn_ref[...]
        gate = _dot(xn, wg_ref[:, :width])
        up = _dot(xn, wu_ref[:, :width])
        act = (gate * jax.nn.sigmoid(gate) * up).astype(BF16)
        return _dot(act, wd_ref[:width, :])

    @pl.when(j == 0)
    def _():
        xn_ref[...] = _rms(x_ref[...], g_ref[...]).astype(BF16)
        acc_ref[...] = tile(FF_TILE)

    @pl.when((j > 0) & (j < last))
    def _():
        acc_ref[...] += tile(FF_TILE)

    @pl.when(j == last)
    def _():
        h = x_ref[...] + 0.5 * (acc_ref[...] + tile(tail))
        if final:
            h = _rms(h, fn_ref[...])
        o_ref[...] = h


def _ffn(h, norm_g, wg, wu, wd, final_g, layer, half, *, final):
    n, d = h.shape
    tm = min(512, n)
    ff = wg.shape[-1]
    steps = pl.cdiv(ff, FF_TILE)
    tail = ff - (steps - 1) * FF_TILE
    return pl.pallas_call(
        functools.partial(_ffn_body, final=final, tail=tail),
        grid=(n // tm, steps),
        in_specs=[
            pl.BlockSpec((tm, d), lambda i, j: (i, 0)),
            pl.BlockSpec((1, d), lambda i, j: (0, 0)),
            pl.BlockSpec((None, None, d, FF_TILE), lambda i, j: (layer, half, 0, j)),
            pl.BlockSpec((None, None, d, FF_TILE), lambda i, j: (layer, half, 0, j)),
            pl.BlockSpec((None, None, FF_TILE, d), lambda i, j: (layer, half, j, 0)),
            pl.BlockSpec((1, d), lambda i, j: (0, 0)),
        ],
        out_specs=pl.BlockSpec((tm, d), lambda i, j: (i, 0)),
        out_shape=jax.ShapeDtypeStruct((n, d), F32),
        scratch_shapes=[pltpu.VMEM((tm, d), BF16), pltpu.VMEM((tm, d), F32)],
        compiler_params=_params("parallel", "arbitrary"),
        name="ffn",
    )(h, norm_g.reshape(1, d), wg, wu, wd, final_g.reshape(1, d))


def _proj_ab_body(x_ref, g_ref, w_ref, cos_ref, sin_ref, o_ref, xn_ref):
    @pl.when(pl.program_id(1) == 0)
    def _():
        xn_ref[...] = _rms(x_ref[...], g_ref[...]).astype(BF16)

    y = _dot(xn_ref[...], w_ref[...])
    c = cos_ref[...]
    s = sin_ref[...]
    half = RET_HEAD_DIM // 2
    parts = []
    for hd in range(PROJ_TN // RET_HEAD_DIM):
        lo = hd * RET_HEAD_DIM
        x1 = y[:, lo:lo + half]
        x2 = y[:, lo + half:lo + 2 * half]
        parts += [x1 * c - x2 * s, x2 * c + x1 * s]
    o_ref[...] = jnp.concatenate(parts, axis=1).reshape(o_ref.shape)


def _proj_ab(h, norm_g, w, cos_tab, sin_tab, seq):
    n, d = h.shape
    nout = w.shape[1]
    tm = min(1024, seq)
    per_seq = seq // tm
    tab = pl.BlockSpec((None, tm, RET_HEAD_DIM // 2), lambda i, j: (jnp.minimum(j, 2), i % per_seq, 0))
    return pl.pallas_call(
        _proj_ab_body,
        grid=(n // tm, nout // PROJ_TN),
        in_specs=[
            pl.BlockSpec((tm, d), lambda i, j: (i, 0)),
            pl.BlockSpec((1, d), lambda i, j: (0, 0)),
            pl.BlockSpec((d, PROJ_TN), lambda i, j: (0, j)),
            tab, tab,
        ],
        out_specs=pl.BlockSpec((tm // SSM_T, SSM_T, PROJ_TN), lambda i, j: (i, 0, j)),
        out_shape=jax.ShapeDtypeStruct((n // SSM_T, SSM_T, nout), F32),
        scratch_shapes=[pltpu.VMEM((tm, d), BF16)],
        compiler_params=_params("parallel", "arbitrary"),
        name="proj_ab",
    )(h, norm_g.reshape(1, d), w, cos_tab, sin_tab)


def _proj_c_body(x_ref, g_ref, w_ref, cf_ref, sa_ref, sb_ref, o_ref, xn_ref):
    @pl.when(pl.program_id(1) == 0)
    def _():
        xn_ref[...] = _rms(x_ref[...], g_ref[...]).astype(BF16)

    y = _dot(xn_ref[...], w_ref[...])
    cf = cf_ref[...]
    sa = sa_ref[...]
    sb = sb_ref[...]
    parts = []
    for hd in range(PROJ_TN // DIFF_HEAD_DIM):
        lo = hd * DIFF_HEAD_DIM
        yh = y[:, lo:lo + DIFF_HEAD_DIM]
        r = (yh * cf + pltpu.roll(yh, ROT_HALF, axis=1) * sa
             + pltpu.roll(yh, DIFF_HEAD_DIM - ROT_HALF, axis=1) * sb)
        parts.append(r.astype(BF16))
    o_ref[...] = jnp.concatenate(parts, axis=1)


def _proj_c(h, norm_g, w, cf, sa, sb, seq):
    n, d = h.shape
    nout = w.shape[1]
    tm = min(1024, seq)
    per_seq = seq // tm
    qk_tiles = 2 * D_MODEL // PROJ_TN
    tab = pl.BlockSpec((None, tm, DIFF_HEAD_DIM), lambda i, j: (j // qk_tiles, i % per_seq, 0))
    return pl.pallas_call(
        _proj_c_body,
        grid=(n // tm, nout // PROJ_TN),
        in_specs=[
            pl.BlockSpec((tm, d), lambda i, j: (i, 0)),
            pl.BlockSpec((1, d), lambda i, j: (0, 0)),
            pl.BlockSpec((d, PROJ_TN), lambda i, j: (0, j)),
            tab, tab, tab,
        ],
        out_specs=pl.BlockSpec((tm, PROJ_TN), lambda i, j: (i, j)),
        out_shape=jax.ShapeDtypeStruct((n, nout), BF16),
        scratch_shapes=[pltpu.VMEM((tm, d), BF16)],
        compiler_params=_params("parallel", "arbitrary"),
        name="proj_c",
    )(h, norm_g.reshape(1, d), w, cf, sa, sb)


def _ret_body(q_ref, k_ref, v_ref, g_ref, dm_ref, qd_ref, kd_ref, cd_ref, o_ref, *, nblk):
    t = RET_BLOCK
    t3 = t // SSM_T
    dmask = dm_ref[...]
    qdec = qd_ref[...]
    kdec = kd_ref[...]
    cdec = cd_ref[...]

    def rows(ref, n):
        r0 = pl.multiple_of(n * t3, t3)
        return ref[pl.ds(r0, t3), :, :].reshape(t, RET_HEAD_DIM)

    def step(n, state):
        q = rows(q_ref, n)
        k = rows(k_ref, n)
        vb = rows(v_ref, n).astype(BF16)
        scores = _dot_nt(q.astype(BF16), k.astype(BF16)) * dmask
        intra = _dot(scores.astype(BF16), vb)
        inter = _dot((q * qdec).astype(BF16), state.astype(BF16))
        new_state = state * cdec + _dot_tn((k * kdec).astype(BF16), vb)
        o = intra + inter
        o = o * lax.rsqrt(jnp.mean(o * o, axis=-1, keepdims=True) + NORM_EPS)
        gate = rows(g_ref, n)
        r0 = pl.multiple_of(n * t, t)
        o_ref[pl.ds(r0, t), :] = (o * (gate * jax.nn.sigmoid(gate))).astype(BF16)
        return new_state

    lax.fori_loop(0, nblk, step, jnp.zeros((RET_HEAD_DIM, RET_HEAD_DIM), F32))


def _retention(proj3, batch, seq):
    n = proj3.shape[0] * SSM_T
    t = RET_BLOCK
    hd = RET_HEAD_DIM
    log_g = jnp.log(1.0 - 2.0 ** (-5.0 - jnp.arange(RET_HEADS, dtype=F32)))
    idx = jnp.arange(t, dtype=F32)
    chunk = jnp.arange(t) // CHUNK
    visible = chunk[:, None] >= chunk[None, :]
    dist = jnp.abs(idx[:, None] - idx[None, :])
    dmask = jnp.where(visible[None], jnp.exp(log_g[:, None, None] * dist[None]), 0.0)
    qdec = jnp.broadcast_to(jnp.exp(log_g[:, None] * (idx[None] + 1.0))[:, :, None], (RET_HEADS, t, hd))
    kdec = jnp.broadcast_to(jnp.exp(log_g[:, None] * (t - 1.0 - idx[None]))[:, :, None], (RET_HEADS, t, hd))
    cdec = jnp.broadcast_to(jnp.exp(log_g * t)[:, None, None], (RET_HEADS, 1, hd))

    def col(off):
        return pl.BlockSpec((seq // SSM_T, SSM_T, hd), lambda b, h: (b, 0, off + h))

    def tab(rows, cols):
        return pl.BlockSpec((None, rows, cols), lambda b, h: (h, 0, 0))

    return pl.pallas_call(
        functools.partial(_ret_body, nblk=seq // t),
        grid=(batch, RET_HEADS),
        in_specs=[col(0), col(RET_HEADS), col(2 * RET_HEADS), col(3 * RET_HEADS),
                  tab(t, t), tab(t, hd), tab(t, hd), tab(1, hd)],
        out_specs=pl.BlockSpec((seq, hd), lambda b, h: (b, h)),
        out_shape=jax.ShapeDtypeStruct((n, RET_WIDTH), BF16),
        compiler_params=_params("parallel", "parallel"),
        name="retention",
    )(proj3, proj3, proj3, proj3, dmask, qdec, kdec, cdec)


def _s5_weights(lam_re, lam_im, log_step, b_re, b_im, c_re, c_im):
    hp = lax.Precision.HIGHEST
    step = jnp.exp(log_step)[:, None]
    mag = jnp.exp(lam_re * step)
    a_re = mag * jnp.cos(lam_im * step)
    a_im = mag * jnp.sin(lam_im * step)
    den = lam_re * lam_re + lam_im * lam_im
    nr = a_re - 1.0
    f_re = (nr * lam_re + a_im * lam_im) / den
    f_im = (a_im * lam_re - nr * lam_im) / den
    bb_re = f_re[..., None] * b_re - f_im[..., None] * b_im
    bb_im = f_re[..., None] * b_im + f_im[..., None] * b_re
    prs, pis = [jnp.ones_like(a_re)], [jnp.zeros_like(a_im)]
    for _ in range(SSM_T):
        prs.append(prs[-1] * a_re - pis[-1] * a_im)
        pis.append(prs[-2] * a_im + pis[-1] * a_re)
    pr = jnp.stack(prs)
    pi = jnp.stack(pis)
    ca_re = c_re[None] * pr[:, :, None, :] - c_im[None] * pi[:, :, None, :]
    ca_im = c_re[None] * pi[:, :, None, :] + c_im[None] * pr[:, :, None, :]
    kern = (jnp.einsum('kgpn,gnq->gkpq', ca_re[:SSM_T], bb_re, precision=hp)
            - jnp.einsum('kgpn,gnq->gkpq', ca_im[:SSM_T], bb_im, precision=hp))
    nj = SSM_NJ
    ks = kern.transpose(0, 3, 1, 2).reshape(nj, 128, SSM_T * SSM_GROUP)
    ro = jnp.stack([ca_re[1:], -ca_im[1:]])
    ro = ro.reshape(2, SSM_T, nj, SSM_GPB, SSM_GROUP, SSM_STATE).transpose(2, 0, 3, 5, 1, 4)
    wo = ro.reshape(nj, 2 * SSM_GPB * SSM_STATE, SSM_T * SSM_GROUP)
    rev_re = jnp.stack([prs[SSM_T - 1 - s] for s in range(SSM_T)])
    rev_im = jnp.stack([pis[SSM_T - 1 - s] for s in range(SSM_T)])
    in_re = rev_re[..., None] * bb_re[None] - rev_im[..., None] * bb_im[None]
    in_im = rev_re[..., None] * bb_im[None] + rev_im[..., None] * bb_re[None]
    wi = jnp.stack([in_re, in_im]).reshape(2, SSM_T, nj, SSM_GPB, SSM_STATE, SSM_GROUP)
    wi = wi.transpose(2, 1, 3, 5, 0, 4).reshape(nj, SSM_T * 128, 2 * SSM_STATE)
    a_t = jnp.stack([pr[SSM_T], pi[SSM_T]]).reshape(2, nj, SSM_GPB * SSM_STATE)
    a_t = jnp.moveaxis(a_t, 0, 1)
    return ks, wo, wi, a_t


def _s5_expand_body(ks_ref, wo_ref, wi_ref, wy_ref, win_ref):
    tp = SSM_T * SSM_GROUP
    wide = SSM_T * 128
    half = SSM_GPB * SSM_STATE

    def iota(shape, dim):
        return lax.broadcasted_iota(jnp.int32, shape, dim)

    r = iota((tp, wide), 0)
    c = iota((tp, wide), 1)
    rep = jnp.where((r // SSM_GROUP == c // 128) & (r % SSM_GROUP == c % SSM_GROUP), 1.0, 0.0).astype(BF16)
    col_b = (iota((1, wide), 1) % 128) // SSM_GROUP

    row_a = iota((128, 1), 0) // SSM_GROUP
    bdk = jnp.where(row_a == col_b, _dot(ks_ref[...].astype(BF16), rep), 0.0).astype(BF16)
    for s in range(SSM_T):
        if s:
            wy_ref[s * 128:(s + 1) * 128, :s * 128] = jnp.zeros((128, s * 128), BF16)
        wy_ref[s * 128:(s + 1) * 128, s * 128:] = bdk[:, :wide - s * 128]

    row_a = (iota((2 * half, 1), 0) % half) // SSM_STATE
    ro = _dot(wo_ref[...].astype(BF16), rep)
    wy_ref[wide:, :] = jnp.where(row_a == col_b, ro, 0.0).astype(BF16)

    r = iota((2 * SSM_STATE, 2 * half), 0)
    c = iota((2 * SSM_STATE, 2 * half), 1)
    rep_in = jnp.where((r // SSM_STATE == c // half) & (r % SSM_STATE == c % SSM_STATE), 1.0, 0.0).astype(BF16)
    row_a = (iota((wide, 1), 0) % 128) // SSM_GROUP
    col_b = (iota((1, 2 * half), 1) % half) // SSM_STATE
    win = _dot(wi_ref[...].astype(BF16), rep_in)
    win_ref[...] = jnp.where(row_a == col_b, win, 0.0).astype(BF16)


def _s5_expand(ks, wo, wi):
    nj = ks.shape[0]
    wide = SSM_T * 128
    half2 = 2 * SSM_GPB * SSM_STATE

    def blk(a):
        return pl.BlockSpec((None,) + a.shape[1:], lambda j: (j, 0, 0))

    return pl.pallas_call(
        _s5_expand_body,
        grid=(nj,),
        in_specs=[blk(ks), blk(wo), blk(wi)],
        out_specs=[pl.BlockSpec((None, wide + half2, wide), lambda j: (j, 0, 0)),
                   pl.BlockSpec((None, wide, half2), lambda j: (j, 0, 0))],
        out_shape=[jax.ShapeDtypeStruct((nj, wide + half2, wide), BF16),
                   jax.ShapeDtypeStruct((nj, wide, half2), BF16)],
        compiler_params=_params("parallel"),
        name="s5_expand",
    )(ks, wo, wi)


def _s5_body(u_ref, wy_ref, win_ref, at_ref, y_ref, s_scr, hp_scr, *, nb, nchunk):
    half = SSM_GPB * SSM_STATE
    u = jnp.concatenate([u_ref[:, s, :] for s in range(SSM_T)], axis=1).astype(BF16)
    s_scr[...] = _dot(u, win_ref[...]).reshape(nb, nchunk, 2 * half)
    a_re = at_ref[0:1, :]
    a_im = at_ref[1:2, :]
    h_re = jnp.zeros((nb, half), F32)
    h_im = jnp.zeros((nb, half), F32)
    for c in range(nchunk):
        hp_scr[:, c, :half] = h_re
        hp_scr[:, c, half:] = h_im
        s_re = s_scr[:, c, :half]
        s_im = s_scr[:, c, half:]
        h_re, h_im = (a_re * h_re - a_im * h_im + s_re, a_re * h_im + a_im * h_re + s_im)
    hp = hp_scr[...].reshape(nb * nchunk, 2 * half).astype(BF16)
    wide = SSM_T * 128
    cb = 256
    for lo in range(0, wide, cb):
        y = (_dot(u[:, :lo + cb], wy_ref[:lo + cb, lo:lo + cb])
             + _dot(hp, wy_ref[wide:, lo:lo + cb]))
        for t in range(lo // 128, (lo + cb) // 128):
            y_ref[:, t, :] = y[:, t * 128 - lo:(t + 1) * 128 - lo]


def _s5(proj3, w_y, w_in, a_t, batch, seq):
    rows, _, width = proj3.shape
    nchunk = seq // SSM_T
    nb = min(4, batch)
    r = nb * nchunk
    ucol0 = (width - SSM_WIDTH) // 128
    half2 = 2 * SSM_GPB * SSM_STATE
    once = pl.Buffered(1)
    return pl.pallas_call(
        functools.partial(_s5_body, nb=nb, nchunk=nchunk),
        grid=(SSM_NJ, rows // r),
        in_specs=[
            pl.BlockSpec((r, SSM_T, 128), lambda j, i: (i, 0, ucol0 + j)),
            pl.BlockSpec((None, SSM_T * 128 + half2, SSM_T * 128), lambda j, i: (j, 0, 0), pipeline_mode=once),
            pl.BlockSpec((None, SSM_T * 128, half2), lambda j, i: (j, 0, 0), pipeline_mode=once),
            pl.BlockSpec((None, 2, half2 // 2), lambda j, i: (j, 0, 0)),
        ],
        out_specs=pl.BlockSpec((r, SSM_T, 128), lambda j, i: (i, 0, j)),
        out_shape=jax.ShapeDtypeStruct((rows, SSM_T, SSM_WIDTH), F32),
        scratch_shapes=[pltpu.VMEM((nb, nchunk, half2), F32), pltpu.VMEM((nb, nchunk, half2), F32)],
        compiler_params=_params("arbitrary", "arbitrary"),
        name="s5",
    )(proj3, w_y, w_in, a_t)


def _gelu(x):
    return 0.5 * x * (1.0 + jnp.tanh(math.sqrt(2.0 / math.pi) * (x + 0.044715 * (x * x * x))))


def _ab_out_body(ya_ref, ys_ref, u_ref, d_ref, wglu_ref, bglu_ref, woa_ref, wob_ref, h_ref, o_ref):
    tm = h_ref.shape[0]
    y = (ys_ref[...] + d_ref[...] * u_ref[...]).reshape(tm, SSM_WIDTH)
    z = _gelu(y)
    gate = jax.nn.sigmoid(_dot(z.astype(BF16), wglu_ref[...]) + bglu_ref[...])
    yb = (z * gate).astype(BF16)
    o_ref[...] = h_ref[...] + (_dot(ya_ref[...], woa_ref[...]) + _dot(yb, wob_ref[...]))


def _ab_out(y_a, y_ssm3, proj3, d_skip, w_glu, b_glu, w_out, h):
    n, d = h.shape
    tm = min(512, n)
    ucol = (proj3.shape[2] - SSM_WIDTH) // SSM_WIDTH
    row = lambda i: (i, 0)
    fixed = lambda i: (0, 0)
    return pl.pallas_call(
        _ab_out_body,
        grid=(n // tm,),
        in_specs=[
            pl.BlockSpec((tm, RET_WIDTH), row),
            pl.BlockSpec((tm // SSM_T, SSM_T, SSM_WIDTH), lambda i: (i, 0, 0)),
            pl.BlockSpec((tm // SSM_T, SSM_T, SSM_WIDTH), lambda i: (i, 0, ucol)),
            pl.BlockSpec((1, 1, SSM_WIDTH), lambda i: (0, 0, 0)),
            pl.BlockSpec((SSM_WIDTH, SSM_WIDTH), fixed),
            pl.BlockSpec((1, SSM_WIDTH), fixed),
            pl.BlockSpec((RET_WIDTH, d), fixed),
            pl.BlockSpec((SSM_WIDTH, d), lambda i: (1, 0)),
            pl.BlockSpec((tm, d), row),
        ],
        out_specs=pl.BlockSpec((tm, d), row),
        out_shape=jax.ShapeDtypeStruct((n, d), F32),
        compiler_params=_params("parallel"),
        name="ab_out",
    )(y_a, y_ssm3, proj3, d_skip.reshape(1, 1, -1), w_glu, b_glu.reshape(1, -1), w_out, w_out, h)


def _att_body(lam_ref, q_ref, k_ref, v_ref, sub_ref, o_ref, *, seq, lambda_init):
    tq = min(ATT_QBLOCK, seq)
    dh = DIFF_HEAD_DIM
    scale = dh ** -0.5
    lam = lam_ref[0]
    neg = jnp.finfo(F32).min
    rc = lax.broadcasted_iota(jnp.int32, (tq, tq), 0) // CHUNK
    cc = lax.broadcasted_iota(jnp.int32, (tq, tq), 1) // CHUNK
    visible = rc >= cc
    for qb in range(seq // tq):
        q0 = qb * tq
        q = q_ref[q0:q0 + tq, :]
        v_diag = v_ref[q0:q0 + tq, :]
        comps = []
        for c in range(2):
            qc = q[:, c * dh:(c + 1) * dh]
            s_diag = _dot_nt(qc, k_ref[q0:q0 + tq, c * dh:(c + 1) * dh]) * scale
            s_diag = jnp.where(visible, s_diag, neg)
            m = jnp.max(s_diag, axis=-1, keepdims=True)
            if qb > 0:
                s_off = _dot_nt(qc, k_ref[0:q0, c * dh:(c + 1) * dh]) * scale
                m = jnp.maximum(m, jnp.max(s_off, axis=-1, keepdims=True))
            p_diag = jnp.exp(s_diag - m)
            l = jnp.sum(p_diag, axis=-1, keepdims=True)
            acc = _dot(p_diag.astype(BF16), v_diag)
            if qb > 0:
                p_off = jnp.exp(s_off - m)
                l = l + jnp.sum(p_off, axis=-1, keepdims=True)
                acc = acc + _dot(p_off.astype(BF16), v_ref[0:q0, :])
            comps.append(acc / l)
        o = comps[0] - lam * comps[1]
        o = _rms(o, sub_ref[...]) * (1.0 - lambda_init)
        o_ref[q0:q0 + tq, :] = o.astype(BF16)


def _attention(qkv, lam, subln, batch, seq, lambda_init):
    n = qkv.shape[0]
    w = 2 * DIFF_HEAD_DIM

    def col(off):
        return pl.BlockSpec((seq, w), lambda b, h: (b, off + h))

    return pl.pallas_call(
        functools.partial(_att_body, seq=seq, lambda_init=lambda_init),
        grid=(batch, DIFF_HEADS),
        in_specs=[
            pl.BlockSpec(memory_space=pltpu.SMEM),
            col(0), col(DIFF_HEADS), col(2 * DIFF_HEADS),
            pl.BlockSpec((1, w), lambda b, h: (0, 0)),
        ],
        out_specs=pl.BlockSpec((seq, w), lambda b, h: (b, h)),
        out_shape=jax.ShapeDtypeStruct((n, D_MODEL), BF16),
        compiler_params=_params("parallel", "parallel"),
        name="diff_attention",
    )(lam.reshape(1), qkv, qkv, qkv, subln.reshape(1, w))


def _out_proj_body(o_ref, w_ref, h_ref, y_ref):
    y_ref[...] = h_ref[...] + _dot(o_ref[...], w_ref[...])


def _out_proj(o, w, h):
    n, d = h.shape
    tm = min(512, n)
    return pl.pallas_call(
        _out_proj_body,
        grid=(n // tm,),
        in_specs=[
            pl.BlockSpec((tm, d), lambda i: (i, 0)),
            pl.BlockSpec((d, d), lambda i: (0, 0)),
            pl.BlockSpec((tm, d), lambda i: (i, 0)),
        ],
        out_specs=pl.BlockSpec((tm, d), lambda i: (i, 0)),
        out_shape=jax.ShapeDtypeStruct((n, d), F32),
        compiler_params=_params("parallel"),
        name="out_proj",
    )(o, w, h)


def _rope_tables(seq, rot_dim, theta):
    inv = 1.0 / (theta ** (jnp.arange(0, rot_dim, 2, dtype=F32) / rot_dim))
    ang = jnp.arange(seq, dtype=F32)[:, None] * inv[None, :]
    return jnp.cos(ang), jnp.sin(ang)


def kernel(x, ffn_norm, ffn_w_gate, ffn_w_up, ffn_w_down, mix_norm, ab_w_in, ab_w_out, ssm_lambda_re, ssm_lambda_im, ssm_log_step, ssm_b_re, ssm_b_im, ssm_c_re, ssm_c_im, ssm_d, ssm_w_glu, ssm_b_glu, c_w_qkv, c_w_out, c_lambda_q1, c_lambda_k1, c_lambda_q2, c_lambda_k2, c_subln, final_norm):
    batch, seq, d = x.shape
    n = batch * seq
    h = x.reshape(n, d)
    wg = _cast_stack(ffn_w_gate)
    wu = _cast_stack(ffn_w_up)
    wd = _cast_stack(ffn_w_down)

    def ffn(h, layer, half, final=False):
        return _ffn(h, ffn_norm[layer, half], wg, wu, wd, final_norm, layer, half, final=final)

    h = ffn(h, 0, 0)
    ret_cos, ret_sin = _rope_tables(seq, RET_HEAD_DIM, RET_THETA)
    k_scale = RET_HEAD_DIM ** -0.5
    cos_tab = jnp.stack([ret_cos, ret_cos * k_scale, jnp.ones_like(ret_cos)])
    sin_tab = jnp.stack([ret_sin, ret_sin * k_scale, jnp.zeros_like(ret_sin)])
    proj3 = _proj_ab(h, mix_norm[0], ab_w_in[0].astype(BF16), cos_tab, sin_tab, seq)
    y_a = _retention(proj3, batch, seq)
    ks, wo, wi, a_t = _s5_weights(ssm_lambda_re[0], ssm_lambda_im[0], ssm_log_step[0], ssm_b_re[0],
                                  ssm_b_im[0], ssm_c_re[0], ssm_c_im[0])
    w_y, w_in = _s5_expand(ks, wo, wi)
    y_ssm3 = _s5(proj3, w_y, w_in, a_t, batch, seq)
    h = _ab_out(y_a, y_ssm3, proj3, ssm_d[0], ssm_w_glu[0].astype(BF16), ssm_b_glu[0],
                ab_w_out[0].astype(BF16), h)
    h = ffn(h, 0, 1)

    h = ffn(h, 1, 0)
    att_cos, att_sin = _rope_tables(seq, 2 * ROT_HALF, ROPE_THETA)
    rest = jnp.zeros((seq, DIFF_HEAD_DIM - 2 * ROT_HALF), F32)
    zhalf = jnp.zeros((seq, ROT_HALF), F32)
    ident = jnp.zeros((seq, DIFF_HEAD_DIM), F32)
    cf = jnp.stack([jnp.concatenate([att_cos, att_cos, rest + 1.0], axis=1), ident + 1.0])
    sa = jnp.stack([jnp.concatenate([zhalf, att_sin, rest], axis=1), ident])
    sb = jnp.stack([jnp.concatenate([-att_sin, zhalf, rest], axis=1), ident])
    qkv = _proj_c(h, mix_norm[1], c_w_qkv[0].astype(BF16), cf, sa, sb, seq)
    lambda_init = 0.8 - 0.6 * math.exp(-0.3 * 1)
    lam = (jnp.exp(jnp.sum(c_lambda_q1[0] * c_lambda_k1[0]))
           - jnp.exp(jnp.sum(c_lambda_q2[0] * c_lambda_k2[0])) + lambda_init)
    o = _attention(qkv, lam, c_subln[0], batch, seq, lambda_init)
    h = _out_proj(o, c_w_out[0].astype(BF16), h)
    h = ffn(h, 1, 1, final=True)
    return h.reshape(batch, seq, d)
```

```python
import functools
import math

import jax
import jax.numpy as jnp
from jax import lax
from jax.experimental import pallas as pl
from jax.experimental.pallas import tpu as pltpu

F32 = jnp.float32
BF16 = jnp.bfloat16

D_MODEL = 2048
CHUNK = 64
NORM_EPS = 1e-6

RET_WIDTH = 1024
RET_HEADS = 4
RET_HEAD_DIM = 256
RET_THETA = 10000.0
RET_BLOCK = 256

SSM_WIDTH = 1024
SSM_GROUP = 16
SSM_GROUPS = 64
SSM_STATE = 64
SSM_T = 16
SSM_GPB = 128 // SSM_GROUP
SSM_NJ = SSM_WIDTH // 128

DIFF_HEAD_DIM = 128
DIFF_HEADS = 8
ROPE_THETA = 500000.0
ROT_HALF = DIFF_HEAD_DIM // 4 // 2
ATT_QBLOCK = 256

FF_TILE = 1024
CAST_ROWS = 512
PROJ_TN = 1024
VMEM_LIMIT = 56 * 1024 * 1024


def _params(*sem):
    return pltpu.CompilerParams(dimension_semantics=sem, vmem_limit_bytes=VMEM_LIMIT)


def _rms(x, g):
    y = x * lax.rsqrt(jnp.mean(x * x, axis=-1, keepdims=True) + NORM_EPS)
    return y * g


def _dot(a, b):
    return jnp.dot(a, b, preferred_element_type=F32)


def _dot_nt(a, b):
    return lax.dot_general(a, b, (((1,), (1,)), ((), ())), preferred_element_type=F32)


def _dot_tn(a, b):
    return lax.dot_general(a, b, (((0,), (0,)), ((), ())), preferred_element_type=F32)


def _cast_body(w_ref, o_ref):
    o_ref[...] = w_ref[...].astype(BF16)


def _cast_stack(w):
    a, b, r, c = w.shape
    spec = pl.BlockSpec((None, None, CAST_ROWS, c), lambda s, i: (s // b, s % b, i, 0))
    return pl.pallas_call(
        _cast_body,
        grid=(a * b, pl.cdiv(r, CAST_ROWS)),
        in_specs=[spec],
        out_specs=spec,
        out_shape=jax.ShapeDtypeStruct(w.shape, BF16),
        compiler_params=_params("parallel", "parallel"),
        name="cast_bf16",
    )(w)


def _ffn_body(x_ref, g_ref, wg_ref, wu_ref, wd_ref, fn_ref, o_ref, xn_ref, acc_ref, *, final, tail):
    j = pl.program_id(1)
    last = pl.num_programs(1) - 1

    def tile(width):
        xn = xn_ref[...]
        gate = _dot(xn, wg_ref[:, :width])
        up = _dot(xn, wu_ref[:, :width])
        act = (gate * jax.nn.sigmoid(gate) * up).astype(BF16)
        return _dot(act, wd_ref[:width, :])

    @pl.when(j == 0)
    def _():
        xn_ref[...] = _rms(x_ref[...], g_ref[...]).astype(BF16)
        acc_ref[...] = tile(FF_TILE)

    @pl.when((j > 0) & (j < last))
    def _():
        acc_ref[...] += tile(FF_TILE)

    @pl.when(j == last)
    def _():
        h = x_ref[...] + 0.5 * (acc_ref[...] + tile(tail))
        if final:
            h = _rms(h, fn_ref[...])
        o_ref[...] = h


def _ffn(h, norm_g, wg, wu, wd, final_g, layer, half, *, final):
    n, d = h.shape
    tm = min(512, n)
    ff = wg.shape[-1]
    steps = pl.cdiv(ff, FF_TILE)
    tail = ff - (steps - 1) * FF_TILE
    return pl.pallas_call(
        functools.partial(_ffn_body, final=final, tail=tail),
        grid=(n // tm, steps),
        in_specs=[
            pl.BlockSpec((tm, d), lambda i, j: (i, 0)),
            pl.BlockSpec((1, d), lambda i, j: (0, 0)),
            pl.BlockSpec((None, None, d, FF_TILE), lambda i, j: (layer, half, 0, j)),
            pl.BlockSpec((None, None, d, FF_TILE), lambda i, j: (layer, half, 0, j)),
            pl.BlockSpec((None, None, FF_TILE, d), lambda i, j: (layer, half, j, 0)),
            pl.BlockSpec((1, d), lambda i, j: (0, 0)),
        ],
        out_specs=pl.BlockSpec((tm, d), lambda i, j: (i, 0)),
        out_shape=jax.ShapeDtypeStruct((n, d), F32),
        scratch_shapes=[pltpu.VMEM((tm, d), BF16), pltpu.VMEM((tm, d), F32)],
        compiler_params=_params("parallel", "arbitrary"),
        name="ffn",
    )(h, norm_g.reshape(1, d), wg, wu, wd, final_g.reshape(1, d))


def _proj_ab_body(x_ref, g_ref, w_ref, cos_ref, sin_ref, o_ref, xn_ref):
    @pl.when(pl.program_id(1) == 0)
    def _():
        xn_ref[...] = _rms(x_ref[...], g_ref[...]).astype(BF16)

    y = _dot(xn_ref[...], w_ref[...])
    c = cos_ref[...]
    s = sin_ref[...]
    half = RET_HEAD_DIM // 2
    parts = []
    for hd in range(PROJ_TN // RET_HEAD_DIM):
        lo = hd * RET_HEAD_DIM
        x1 = y[:, lo:lo + half]
        x2 = y[:, lo + half:lo + 2 * half]
        parts += [x1 * c - x2 * s, x2 * c + x1 * s]
    o_ref[...] = jnp.concatenate(parts, axis=1).reshape(o_ref.shape)


def _proj_ab(h, norm_g, w, cos_tab, sin_tab, seq):
    n, d = h.shape
    nout = w.shape[1]
    tm = min(1024, seq)
    per_seq = seq // tm
    tab = pl.BlockSpec((None, tm, RET_HEAD_DIM // 2), lambda i, j: (jnp.minimum(j, 2), i % per_seq, 0))
    return pl.pallas_call(
        _proj_ab_body,
        grid=(n // tm, nout // PROJ_TN),
        in_specs=[
            pl.BlockSpec((tm, d), lambda i, j: (i, 0)),
            pl.BlockSpec((1, d), lambda i, j: (0, 0)),
            pl.BlockSpec((d, PROJ_TN), lambda i, j: (0, j)),
            tab, tab,
        ],
        out_specs=pl.BlockSpec((tm // SSM_T, SSM_T, PROJ_TN), lambda i, j: (i, 0, j)),
        out_shape=jax.ShapeDtypeStruct((n // SSM_T, SSM_T, nout), F32),
        scratch_shapes=[pltpu.VMEM((tm, d), BF16)],
        compiler_params=_params("parallel", "arbitrary"),
        name="proj_ab",
    )(h, norm_g.reshape(1, d), w, cos_tab, sin_tab)


def _cast_qkv_body(w_ref, o_ref):
    dh = DIFF_HEAD_DIM
    old = lax.broadcasted_iota(jnp.int32, (dh, dh), 0)
    new = lax.broadcasted_iota(jnp.int32, (dh, dh), 1)
    src = jnp.where(new < ROT_HALF, new,
                    jnp.where(new < dh // 2, new + ROT_HALF,
                              jnp.where(new < dh // 2 + ROT_HALF, new - (dh // 2 - ROT_HALF), new)))
    src = jnp.where(pl.program_id(1) < 2 * D_MODEL // PROJ_TN, src, new)
    perm = jnp.where(old == src, 1.0, 0.0).astype(BF16)
    w = w_ref[...].astype(BF16)
    parts = [_dot(w[:, hd * dh:(hd + 1) * dh], perm) for hd in range(PROJ_TN // dh)]
    o_ref[...] = jnp.concatenate(parts, axis=1).astype(BF16)


def _cast_qkv(w):
    r, c = w.shape
    spec = pl.BlockSpec((CAST_ROWS, PROJ_TN), lambda i, j: (i, j))
    return pl.pallas_call(
        _cast_qkv_body,
        grid=(r // CAST_ROWS, c // PROJ_TN),
        in_specs=[spec],
        out_specs=spec,
        out_shape=jax.ShapeDtypeStruct(w.shape, BF16),
        compiler_params=_params("parallel", "parallel"),
        name="cast_qkv",
    )(w)


def _proj_c_body(x_ref, g_ref, w_ref, cf_ref, sg_ref, o_ref, xn_ref):
    @pl.when(pl.program_id(1) == 0)
    def _():
        xn_ref[...] = _rms(x_ref[...], g_ref[...]).astype(BF16)

    y = _dot(xn_ref[...], w_ref[...])
    cf = cf_ref[...]
    sg = sg_ref[...]
    parts = []
    for hd in range(PROJ_TN // DIFF_HEAD_DIM):
        lo = hd * DIFF_HEAD_DIM
        yh = y[:, lo:lo + DIFF_HEAD_DIM]
        r = yh * cf + pltpu.roll(yh, DIFF_HEAD_DIM // 2, axis=1) * sg
        parts.append(r.astype(BF16))
    o_ref[...] = jnp.concatenate(parts, axis=1)


def _proj_c(h, norm_g, w, cf, sg, seq):
    n, d = h.shape
    nout = w.shape[1]
    tm = min(1024, seq)
    per_seq = seq // tm
    qk_tiles = 2 * D_MODEL // PROJ_TN
    tab = pl.BlockSpec((None, tm, DIFF_HEAD_DIM), lambda i, j: (j // qk_tiles, i % per_seq, 0))
    return pl.pallas_call(
        _proj_c_body,
        grid=(n // tm, nout // PROJ_TN),
        in_specs=[
            pl.BlockSpec((tm, d), lambda i, j: (i, 0)),
            pl.BlockSpec((1, d), lambda i, j: (0, 0)),
            pl.BlockSpec((d, PROJ_TN), lambda i, j: (0, j)),
            tab, tab,
        ],
        out_specs=pl.BlockSpec((tm, PROJ_TN), lambda i, j: (i, j)),
        out_shape=jax.ShapeDtypeStruct((n, nout), BF16),
        scratch_shapes=[pltpu.VMEM((tm, d), BF16)],
        compiler_params=_params("parallel", "arbitrary"),
        name="proj_c",
    )(h, norm_g.reshape(1, d), w, cf, sg)


def _ret_body(q_ref, k_ref, v_ref, g_ref, dm_ref, qd_ref, kd_ref, cd_ref, o_ref, *, nblk):
    t = RET_BLOCK
    t3 = t // SSM_T
    dmask = dm_ref[...]
    qdec = qd_ref[...]
    kdec = kd_ref[...]
    cdec = cd_ref[...]

    def rows(ref, n):
        r0 = pl.multiple_of(n * t3, t3)
        return ref[pl.ds(r0, t3), :, :].reshape(t, RET_HEAD_DIM)

    def step(n, state):
        q = rows(q_ref, n)
        k = rows(k_ref, n)
        vb = rows(v_ref, n).astype(BF16)
        scores = _dot_nt(q.astype(BF16), k.astype(BF16)) * dmask
        intra = _dot(scores.astype(BF16), vb)
        inter = _dot((q * qdec).astype(BF16), state.astype(BF16))
        new_state = state * cdec + _dot_tn((k * kdec).astype(BF16), vb)
        o = intra + inter
        o = o * lax.rsqrt(jnp.mean(o * o, axis=-1, keepdims=True) + NORM_EPS)
        gate = rows(g_ref, n)
        r0 = pl.multiple_of(n * t, t)
        o_ref[pl.ds(r0, t), :] = (o * (gate * jax.nn.sigmoid(gate))).astype(BF16)
        return new_state

    lax.fori_loop(0, nblk, step, jnp.zeros((RET_HEAD_DIM, RET_HEAD_DIM), F32))


def _retention(proj3, batch, seq):
    n = proj3.shape[0] * SSM_T
    t = RET_BLOCK
    hd = RET_HEAD_DIM
    log_g = jnp.log(1.0 - 2.0 ** (-5.0 - jnp.arange(RET_HEADS, dtype=F32)))
    idx = jnp.arange(t, dtype=F32)
    chunk = jnp.arange(t) // CHUNK
    visible = chunk[:, None] >= chunk[None, :]
    dist = jnp.abs(idx[:, None] - idx[None, :])
    dmask = jnp.where(visible[None], jnp.exp(log_g[:, None, None] * dist[None]), 0.0)
    qdec = jnp.broadcast_to(jnp.exp(log_g[:, None] * (idx[None] + 1.0))[:, :, None], (RET_HEADS, t, hd))
    kdec = jnp.broadcast_to(jnp.exp(log_g[:, None] * (t - 1.0 - idx[None]))[:, :, None], (RET_HEADS, t, hd))
    cdec = jnp.broadcast_to(jnp.exp(log_g * t)[:, None, None], (RET_HEADS, 1, hd))

    def col(off):
        return pl.BlockSpec((seq // SSM_T, SSM_T, hd), lambda b, h: (b, 0, off + h))

    def tab(rows, cols):
        return pl.BlockSpec((None, rows, cols), lambda b, h: (h, 0, 0))

    return pl.pallas_call(
        functools.partial(_ret_body, nblk=seq // t),
        grid=(batch, RET_HEADS),
        in_specs=[col(0), col(RET_HEADS), col(2 * RET_HEADS), col(3 * RET_HEADS),
                  tab(t, t), tab(t, hd), tab(t, hd), tab(1, hd)],
        out_specs=pl.BlockSpec((seq, hd), lambda b, h: (b, h)),
        out_shape=jax.ShapeDtypeStruct((n, RET_WIDTH), BF16),
        compiler_params=_params("parallel", "parallel"),
        name="retention",
    )(proj3, proj3, proj3, proj3, dmask, qdec, kdec, cdec)


def _s5_weights(lam_re, lam_im, log_step, b_re, b_im, c_re, c_im):
    hp = lax.Precision.HIGHEST
    step = jnp.exp(log_step)[:, None]
    mag = jnp.exp(lam_re * step)
    a_re = mag * jnp.cos(lam_im * step)
    a_im = mag * jnp.sin(lam_im * step)
    den = lam_re * lam_re + lam_im * lam_im
    nr = a_re - 1.0
    f_re = (nr * lam_re + a_im * lam_im) / den
    f_im = (a_im * lam_re - nr * lam_im) / den
    bb_re = f_re[..., None] * b_re - f_im[..., None] * b_im
    bb_im = f_re[..., None] * b_im + f_im[..., None] * b_re
    prs, pis = [jnp.ones_like(a_re)], [jnp.zeros_like(a_im)]
    for _ in range(SSM_T):
        prs.append(prs[-1] * a_re - pis[-1] * a_im)
        pis.append(prs[-2] * a_im + pis[-1] * a_re)
    pr = jnp.stack(prs)
    pi = jnp.stack(pis)
    ca_re = c_re[None] * pr[:, :, None, :] - c_im[None] * pi[:, :, None, :]
    ca_im = c_re[None] * pi[:, :, None, :] + c_im[None] * pr[:, :, None, :]
    kern = (jnp.einsum('kgpn,gnq->gkpq', ca_re[:SSM_T], bb_re, precision=hp)
            - jnp.einsum('kgpn,gnq->gkpq', ca_im[:SSM_T], bb_im, precision=hp))
    nj = SSM_NJ
    ks = kern.transpose(0, 3, 1, 2).reshape(nj, 128, SSM_T * SSM_GROUP)
    ro = jnp.stack([ca_re[1:], -ca_im[1:]])
    ro = ro.reshape(2, SSM_T, nj, SSM_GPB, SSM_GROUP, SSM_STATE).transpose(2, 0, 3, 5, 1, 4)
    wo = ro.reshape(nj, 2 * SSM_GPB * SSM_STATE, SSM_T * SSM_GROUP)
    rev_re = jnp.stack([prs[SSM_T - 1 - s] for s in range(SSM_T)])
    rev_im = jnp.stack([pis[SSM_T - 1 - s] for s in range(SSM_T)])
    in_re = rev_re[..., None] * bb_re[None] - rev_im[..., None] * bb_im[None]
    in_im = rev_re[..., None] * bb_im[None] + rev_im[..., None] * bb_re[None]
    wi = jnp.stack([in_re, in_im]).reshape(2, SSM_T, nj, SSM_GPB, SSM_STATE, SSM_GROUP)
    wi = wi.transpose(2, 1, 3, 5, 0, 4).reshape(nj, SSM_T * 128, 2 * SSM_STATE)
    a_t = jnp.stack([pr[SSM_T], pi[SSM_T]]).reshape(2, nj, SSM_GPB * SSM_STATE)
    a_t = jnp.moveaxis(a_t, 0, 1)
    return ks, wo, wi, a_t


def _s5_expand_body(ks_ref, wo_ref, wi_ref, wy_ref, win_ref):
    tp = SSM_T * SSM_GROUP
    wide = SSM_T * 128
    half = SSM_GPB * SSM_STATE

    def iota(shape, dim):
        return lax.broadcasted_iota(jnp.int32, shape, dim)

    r = iota((tp, wide), 0)
    c = iota((tp, wide), 1)
    rep = jnp.where((r // SSM_GROUP == c // 128) & (r % SSM_GROUP == c % SSM_GROUP), 1.0, 0.0).astype(BF16)
    col_b = (iota((1, wide), 1) % 128) // SSM_GROUP

    row_a = iota((128, 1), 0) // SSM_GROUP
    bdk = jnp.where(row_a == col_b, _dot(ks_ref[...].astype(BF16), rep), 0.0).astype(BF16)
    for s in range(SSM_T):
        if s:
            wy_ref[s * 128:(s + 1) * 128, :s * 128] = jnp.zeros((128, s * 128), BF16)
        wy_ref[s * 128:(s + 1) * 128, s * 128:] = bdk[:, :wide - s * 128]

    row_a = (iota((2 * half, 1), 0) % half) // SSM_STATE
    ro = _dot(wo_ref[...].astype(BF16), rep)
    wy_ref[wide:, :] = jnp.where(row_a == col_b, ro, 0.0).astype(BF16)

    r = iota((2 * SSM_STATE, 2 * half), 0)
    c = iota((2 * SSM_STATE, 2 * half), 1)
    rep_in = jnp.where((r // SSM_STATE == c // half) & (r % SSM_STATE == c % SSM_STATE), 1.0, 0.0).astype(BF16)
    row_a = (iota((wide, 1), 0) % 128) // SSM_GROUP
    col_b = (iota((1, 2 * half), 1) % half) // SSM_STATE
    win = _dot(wi_ref[...].astype(BF16), rep_in)
    win_ref[...] = jnp.where(row_a == col_b, win, 0.0).astype(BF16)


def _s5_expand(ks, wo, wi):
    nj = ks.shape[0]
    wide = SSM_T * 128
    half2 = 2 * SSM_GPB * SSM_STATE

    def blk(a):
        return pl.BlockSpec((None,) + a.shape[1:], lambda j: (j, 0, 0))

    return pl.pallas_call(
        _s5_expand_body,
        grid=(nj,),
        in_specs=[blk(ks), blk(wo), blk(wi)],
        out_specs=[pl.BlockSpec((None, wide + half2, wide), lambda j: (j, 0, 0)),
                   pl.BlockSpec((None, wide, half2), lambda j: (j, 0, 0))],
        out_shape=[jax.ShapeDtypeStruct((nj, wide + half2, wide), BF16),
                   jax.ShapeDtypeStruct((nj, wide, half2), BF16)],
        compiler_params=_params("parallel"),
        name="s5_expand",
    )(ks, wo, wi)


def _s5_body(u_ref, wy_ref, win_ref, at_ref, y_ref, s_scr, hp_scr, *, nb, nchunk):
    half = SSM_GPB * SSM_STATE
    u = jnp.concatenate([u_ref[:, s, :] for s in range(SSM_T)], axis=1).astype(BF16)
    s_scr[...] = _dot(u, win_ref[...]).reshape(nb, nchunk, 2 * half)
    a_re = at_ref[0:1, :]
    a_im = at_ref[1:2, :]
    h_re = jnp.zeros((nb, half), F32)
    h_im = jnp.zeros((nb, half), F32)
    for c in range(nchunk):
        hp_scr[:, c, :half] = h_re
        hp_scr[:, c, half:] = h_im
        s_re = s_scr[:, c, :half]
        s_im = s_scr[:, c, half:]
        h_re, h_im = (a_re * h_re - a_im * h_im + s_re, a_re * h_im + a_im * h_re + s_im)
    hp = hp_scr[...].reshape(nb * nchunk, 2 * half).astype(BF16)
    wide = SSM_T * 128
    cb = 256
    for lo in range(0, wide, cb):
        y = (_dot(u[:, :lo + cb], wy_ref[:lo + cb, lo:lo + cb])
             + _dot(hp, wy_ref[wide:, lo:lo + cb]))
        for t in range(lo // 128, (lo + cb) // 128):
            y_ref[:, t, :] = y[:, t * 128 - lo:(t + 1) * 128 - lo]


def _s5(proj3, w_y, w_in, a_t, batch, seq):
    rows, _, width = proj3.shape
    nchunk = seq // SSM_T
    nb = min(4, batch)
    r = nb * nchunk
    ucol0 = (width - SSM_WIDTH) // 128
    half2 = 2 * SSM_GPB * SSM_STATE
    once = pl.Buffered(1)
    return pl.pallas_call(
        functools.partial(_s5_body, nb=nb, nchunk=nchunk),
        grid=(SSM_NJ, rows // r),
        in_specs=[
            pl.BlockSpec((r, SSM_T, 128), lambda j, i: (i, 0, ucol0 + j)),
            pl.BlockSpec((None, SSM_T * 128 + half2, SSM_T * 128), lambda j, i: (j, 0, 0), pipeline_mode=once),
            pl.BlockSpec((None, SSM_T * 128, half2), lambda j, i: (j, 0, 0), pipeline_mode=once),
            pl.BlockSpec((None, 2, half2 // 2), lambda j, i: (j, 0, 0)),
        ],
        out_specs=pl.BlockSpec((r, SSM_T, 128), lambda j, i: (i, 0, j)),
        out_shape=jax.ShapeDtypeStruct((rows, SSM_T, SSM_WIDTH), F32),
        scratch_shapes=[pltpu.VMEM((nb, nchunk, half2), F32), pltpu.VMEM((nb, nchunk, half2), F32)],
        compiler_params=_params("arbitrary", "arbitrary"),
        name="s5",
    )(proj3, w_y, w_in, a_t)


def _gelu(x):
    return 0.5 * x * (1.0 + jnp.tanh(math.sqrt(2.0 / math.pi) * (x + 0.044715 * (x * x * x))))


def _ab_out_body(ya_ref, ys_ref, u_ref, d_ref, wglu_ref, bglu_ref, woa_ref, wob_ref, h_ref, o_ref):
    tm = h_ref.shape[0]
    y = (ys_ref[...] + d_ref[...] * u_ref[...]).reshape(tm, SSM_WIDTH)
    z = _gelu(y)
    gate = jax.nn.sigmoid(_dot(z.astype(BF16), wglu_ref[...]) + bglu_ref[...])
    yb = (z * gate).astype(BF16)
    o_ref[...] = h_ref[...] + (_dot(ya_ref[...], woa_ref[...]) + _dot(yb, wob_ref[...]))


def _ab_out(y_a, y_ssm3, proj3, d_skip, w_glu, b_glu, w_out, h):
    n, d = h.shape
    tm = min(512, n)
    ucol = (proj3.shape[2] - SSM_WIDTH) // SSM_WIDTH
    row = lambda i: (i, 0)
    fixed = lambda i: (0, 0)
    return pl.pallas_call(
        _ab_out_body,
        grid=(n // tm,),
        in_specs=[
            pl.BlockSpec((tm, RET_WIDTH), row),
            pl.BlockSpec((tm // SSM_T, SSM_T, SSM_WIDTH), lambda i: (i, 0, 0)),
            pl.BlockSpec((tm // SSM_T, SSM_T, SSM_WIDTH), lambda i: (i, 0, ucol)),
            pl.BlockSpec((1, 1, SSM_WIDTH), lambda i: (0, 0, 0)),
            pl.BlockSpec((SSM_WIDTH, SSM_WIDTH), fixed),
            pl.BlockSpec((1, SSM_WIDTH), fixed),
            pl.BlockSpec((RET_WIDTH, d), fixed),
            pl.BlockSpec((SSM_WIDTH, d), lambda i: (1, 0)),
            pl.BlockSpec((tm, d), row),
        ],
        out_specs=pl.BlockSpec((tm, d), row),
        out_shape=jax.ShapeDtypeStruct((n, d), F32),
        compiler_params=_params("parallel"),
        name="ab_out",
    )(y_a, y_ssm3, proj3, d_skip.reshape(1, 1, -1), w_glu, b_glu.reshape(1, -1), w_out, w_out, h)


def _att_body(lam_ref, q_ref, k_ref, v_ref, sub_ref, o_ref, *, seq, lambda_init):
    tq = min(ATT_QBLOCK, seq)
    dh = DIFF_HEAD_DIM
    k2 = dh ** -0.5 * math.log2(math.e)
    lam = lam_ref[0]
    neg = jnp.finfo(F32).min
    rc = lax.broadcasted_iota(jnp.int32, (tq, tq), 0) // CHUNK
    cc = lax.broadcasted_iota(jnp.int32, (tq, tq), 1) // CHUNK
    visible = rc >= cc
    for qb in range(seq // tq):
        q0 = qb * tq
        q = q_ref[q0:q0 + tq, :]
        v_diag = v_ref[q0:q0 + tq, :]
        comps = []
        for c in range(2):
            qc = q[:, c * dh:(c + 1) * dh]
            s_diag = _dot_nt(qc, k_ref[q0:q0 + tq, c * dh:(c + 1) * dh])
            s_diag = jnp.where(visible, s_diag, neg)
            m = jnp.max(s_diag, axis=-1, keepdims=True)
            if qb > 0:
                s_off = _dot_nt(qc, k_ref[0:q0, c * dh:(c + 1) * dh])
                m = jnp.maximum(m, jnp.max(s_off, axis=-1, keepdims=True))
            mk = m * k2
            p_diag = jnp.exp2(s_diag * k2 - mk)
            l = jnp.sum(p_diag, axis=-1, keepdims=True)
            acc = _dot(p_diag.astype(BF16), v_diag)
            if qb > 0:
                p_off = jnp.exp2(s_off * k2 - mk)
                l = l + jnp.sum(p_off, axis=-1, keepdims=True)
                acc = acc + _dot(p_off.astype(BF16), v_ref[0:q0, :])
            comps.append(acc / l)
        o = comps[0] - lam * comps[1]
        o = _rms(o, sub_ref[...]) * (1.0 - lambda_init)
        o_ref[q0:q0 + tq, :] = o.astype(BF16)


def _attention(qkv, lam, subln, batch, seq, lambda_init):
    n = qkv.shape[0]
    w = 2 * DIFF_HEAD_DIM

    def col(off):
        return pl.BlockSpec((seq, w), lambda b, h: (b, off + h))

    return pl.pallas_call(
        functools.partial(_att_body, seq=seq, lambda_init=lambda_init),
        grid=(batch, DIFF_HEADS),
        in_specs=[
            pl.BlockSpec(memory_space=pltpu.SMEM),
            col(0), col(DIFF_HEADS), col(2 * DIFF_HEADS),
            pl.BlockSpec((1, w), lambda b, h: (0, 0)),
        ],
        out_specs=pl.BlockSpec((seq, w), lambda b, h: (b, h)),
        out_shape=jax.ShapeDtypeStruct((n, D_MODEL), BF16),
        compiler_params=_params("parallel", "parallel"),
        name="diff_attention",
    )(lam.reshape(1), qkv, qkv, qkv, subln.reshape(1, w))


def _out_proj_body(o_ref, w_ref, h_ref, y_ref):
    y_ref[...] = h_ref[...] + _dot(o_ref[...], w_ref[...])


def _out_proj(o, w, h):
    n, d = h.shape
    tm = min(512, n)
    return pl.pallas_call(
        _out_proj_body,
        grid=(n // tm,),
        in_specs=[
            pl.BlockSpec((tm, d), lambda i: (i, 0)),
            pl.BlockSpec((d, d), lambda i: (0, 0)),
            pl.BlockSpec((tm, d), lambda i: (i, 0)),
        ],
        out_specs=pl.BlockSpec((tm, d), lambda i: (i, 0)),
        out_shape=jax.ShapeDtypeStruct((n, d), F32),
        compiler_params=_params("parallel"),
        name="out_proj",
    )(o, w, h)


def _rope_tables(seq, rot_dim, theta):
    inv = 1.0 / (theta ** (jnp.arange(0, rot_dim, 2, dtype=F32) / rot_dim))
    ang = jnp.arange(seq, dtype=F32)[:, None] * inv[None, :]
    return jnp.cos(ang), jnp.sin(ang)


def kernel(x, ffn_norm, ffn_w_gate, ffn_w_up, ffn_w_down, mix_norm, ab_w_in, ab_w_out, ssm_lambda_re, ssm_lambda_im, ssm_log_step, ssm_b_re, ssm_b_im, ssm_c_re, ssm_c_im, ssm_d, ssm_w_glu, ssm_b_glu, c_w_qkv, c_w_out, c_lambda_q1, c_lambda_k1, c_lambda_q2, c_lambda_k2, c_subln, final_norm):
    batch, seq, d = x.shape
    n = batch * seq
    h = x.reshape(n, d)
    wg = _cast_stack(ffn_w_gate)
    wu = _cast_stack(ffn_w_up)
    wd = _cast_stack(ffn_w_down)

    def ffn(h, layer, half, final=False):
        return _ffn(h, ffn_norm[layer, half], wg, wu, wd, final_norm, layer, half, final=final)

    h = ffn(h, 0, 0)
    ret_cos, ret_sin = _rope_tables(seq, RET_HEAD_DIM, RET_THETA)
    k_scale = RET_HEAD_DIM ** -0.5
    cos_tab = jnp.stack([ret_cos, ret_cos * k_scale, jnp.ones_like(ret_cos)])
    sin_tab = jnp.stack([ret_sin, ret_sin * k_scale, jnp.zeros_like(ret_sin)])
    proj3 = _proj_ab(h, mix_norm[0], ab_w_in[0].astype(BF16), cos_tab, sin_tab, seq)
    y_a = _retention(proj3, batch, seq)
    ks, wo, wi, a_t = _s5_weights(ssm_lambda_re[0], ssm_lambda_im[0], ssm_log_step[0], ssm_b_re[0],
                                  ssm_b_im[0], ssm_c_re[0], ssm_c_im[0])
    w_y, w_in = _s5_expand(ks, wo, wi)
    y_ssm3 = _s5(proj3, w_y, w_in, a_t, batch, seq)
    h = _ab_out(y_a, y_ssm3, proj3, ssm_d[0], ssm_w_glu[0].astype(BF16), ssm_b_glu[0],
                ab_w_out[0].astype(BF16), h)
    h = ffn(h, 0, 1)

    h = ffn(h, 1, 0)
    att_cos, att_sin = _rope_tables(seq, 2 * ROT_HALF, ROPE_THETA)
    rest = jnp.zeros((seq, DIFF_HEAD_DIM // 2 - ROT_HALF), F32)
    ident = jnp.zeros((seq, DIFF_HEAD_DIM), F32)
    cf = jnp.stack([jnp.concatenate([att_cos, rest + 1.0, att_cos, rest + 1.0], axis=1), ident + 1.0])
    sg = jnp.stack([jnp.concatenate([-att_sin, rest, att_sin, rest], axis=1), ident])
    qkv = _proj_c(h, mix_norm[1], _cast_qkv(c_w_qkv[0]), cf, sg, seq)
    lambda_init = 0.8 - 0.6 * math.exp(-0.3 * 1)
    lam = (jnp.exp(jnp.sum(c_lambda_q1[0] * c_lambda_k1[0]))
           - jnp.exp(jnp.sum(c_lambda_q2[0] * c_lambda_k2[0])) + lambda_init)
    o = _attention(qkv, lam, c_subln[0], batch, seq, lambda_init)
    h = _out_proj(o, c_w_out[0].astype(BF16), h)
    h = ffn(h, 1, 1, final=True)
    return h.reshape(batch, seq, d)
```

```python
import functools
import math

import jax
import jax.numpy as jnp
from jax import lax
from jax.experimental import pallas as pl
from jax.experimental.pallas import tpu as pltpu

F32 = jnp.float32
BF16 = jnp.bfloat16

D_MODEL = 2048
CHUNK = 64
NORM_EPS = 1e-6

RET_WIDTH = 1024
RET_HEADS = 4
RET_HEAD_DIM = 256
RET_THETA = 10000.0
RET_BLOCK = 256

SSM_WIDTH = 1024
SSM_GROUP = 16
SSM_GROUPS = 64
SSM_STATE = 64
SSM_T = 16
SSM_GPB = 128 // SSM_GROUP
SSM_NJ = SSM_WIDTH // 128

DIFF_HEAD_DIM = 128
DIFF_HEADS = 8
ROPE_THETA = 500000.0
ROT_HALF = DIFF_HEAD_DIM // 4 // 2
ATT_QBLOCK = 256

FF_TILE = 1024
CAST_ROWS = 512
PROJ_TN = 1024
VMEM_LIMIT = 56 * 1024 * 1024


def _params(*sem):
    return pltpu.CompilerParams(dimension_semantics=sem, vmem_limit_bytes=VMEM_LIMIT)


def _rms(x, g):
    y = x * lax.rsqrt(jnp.mean(x * x, axis=-1, keepdims=True) + NORM_EPS)
    return y * g


def _dot(a, b):
    return jnp.dot(a, b, preferred_element_type=F32)


def _dot_nt(a, b):
    return lax.dot_general(a, b, (((1,), (1,)), ((), ())), preferred_element_type=F32)


def _dot_tn(a, b):
    return lax.dot_general(a, b, (((0,), (0,)), ((), ())), preferred_element_type=F32)


def _cast_body(w_ref, o_ref):
    o_ref[...] = w_ref[...].astype(BF16)


def _cast_stack(w):
    a, b, r, c = w.shape
    spec = pl.BlockSpec((None, None, CAST_ROWS, c), lambda s, i: (s // b, s % b, i, 0))
    return pl.pallas_call(
        _cast_body,
        grid=(a * b, pl.cdiv(r, CAST_ROWS)),
        in_specs=[spec],
        out_specs=spec,
        out_shape=jax.ShapeDtypeStruct(w.shape, BF16),
        compiler_params=_params("parallel", "parallel"),
        name="cast_bf16",
    )(w)


def _ffn_body(x_ref, g_ref, wg_ref, wu_ref, wd_ref, fn_ref, o_ref, xn_ref, acc_ref, *, final, tail):
    j = pl.program_id(1)
    last = pl.num_programs(1) - 1

    def tile(width):
        xn = xn_ref[...]
        gate = _dot(xn, wg_ref[:, :width])
        up = _dot(xn, wu_ref[:, :width])
        act = (gate * jax.nn.sigmoid(gate) * up).astype(BF16)
        return _dot(act, wd_ref[:width, :])

    @pl.when(j == 0)
    def _():
        xn_ref[...] = _rms(x_ref[...], g_ref[...]).astype(BF16)
        acc_ref[...] = tile(tail)

    @pl.when((j > 0) & (j < last))
    def _():
        acc_ref[...] += tile(FF_TILE)

    @pl.when(j == last)
    def _():
        h = x_ref[...] + 0.5 * (acc_ref[...] + tile(FF_TILE))
        if final:
            h = _rms(h, fn_ref[...])
        o_ref[...] = h


def _ffn(h, norm_g, wg, wu, wd, final_g, layer, half, *, final):
    n, d = h.shape
    tm = min(512, n)
    ff = wg.shape[-1]
    steps = pl.cdiv(ff, FF_TILE)
    tail = ff - (steps - 1) * FF_TILE
    return pl.pallas_call(
        functools.partial(_ffn_body, final=final, tail=tail),
        grid=(n // tm, steps),
        in_specs=[
            pl.BlockSpec((tm, d), lambda i, j: (i, 0)),
            pl.BlockSpec((1, d), lambda i, j: (0, 0)),
            pl.BlockSpec((None, None, d, FF_TILE), lambda i, j: (layer, half, 0, (j + steps - 1) % steps)),
            pl.BlockSpec((None, None, d, FF_TILE), lambda i, j: (layer, half, 0, (j + steps - 1) % steps)),
            pl.BlockSpec((None, None, FF_TILE, d), lambda i, j: (layer, half, (j + steps - 1) % steps, 0)),
            pl.BlockSpec((1, d), lambda i, j: (0, 0)),
        ],
        out_specs=pl.BlockSpec((tm, d), lambda i, j: (i, 0)),
        out_shape=jax.ShapeDtypeStruct((n, d), F32),
        scratch_shapes=[pltpu.VMEM((tm, d), BF16), pltpu.VMEM((tm, d), F32)],
        compiler_params=_params("parallel", "arbitrary"),
        name="ffn",
    )(h, norm_g.reshape(1, d), wg, wu, wd, final_g.reshape(1, d))


def _proj_ab_body(x_ref, g_ref, w_ref, cos_ref, sin_ref, o_ref, xn_ref):
    @pl.when(pl.program_id(1) == 0)
    def _():
        xn_ref[...] = _rms(x_ref[...], g_ref[...]).astype(BF16)

    y = _dot(xn_ref[...], w_ref[...])
    c = cos_ref[...]
    s = sin_ref[...]
    half = RET_HEAD_DIM // 2
    parts = []
    for hd in range(PROJ_TN // RET_HEAD_DIM):
        lo = hd * RET_HEAD_DIM
        x1 = y[:, lo:lo + half]
        x2 = y[:, lo + half:lo + 2 * half]
        parts += [x1 * c - x2 * s, x2 * c + x1 * s]
    o_ref[...] = jnp.concatenate(parts, axis=1)


def _proj_ab(h, norm_g, w, cos_tab, sin_tab, seq):
    n, d = h.shape
    nout = w.shape[1]
    tm = min(1024, seq)
    per_seq = seq // tm
    tab = pl.BlockSpec((None, tm, RET_HEAD_DIM // 2), lambda i, j: (jnp.minimum(j, 2), i % per_seq, 0))
    return pl.pallas_call(
        _proj_ab_body,
        grid=(n // tm, nout // PROJ_TN),
        in_specs=[
            pl.BlockSpec((tm, d), lambda i, j: (i, 0)),
            pl.BlockSpec((1, d), lambda i, j: (0, 0)),
            pl.BlockSpec((d, PROJ_TN), lambda i, j: (0, j)),
            tab, tab,
        ],
        out_specs=pl.BlockSpec((tm, PROJ_TN), lambda i, j: (i, j)),
        out_shape=jax.ShapeDtypeStruct((n, nout), F32),
        scratch_shapes=[pltpu.VMEM((tm, d), BF16)],
        compiler_params=_params("parallel", "arbitrary"),
        name="proj_ab",
    )(h, norm_g.reshape(1, d), w, cos_tab, sin_tab)


def _cast_qkv_body(w_ref, o_ref):
    dh = DIFF_HEAD_DIM
    old = lax.broadcasted_iota(jnp.int32, (dh, dh), 0)
    new = lax.broadcasted_iota(jnp.int32, (dh, dh), 1)
    src = jnp.where(new < ROT_HALF, new,
                    jnp.where(new < dh // 2, new + ROT_HALF,
                              jnp.where(new < dh // 2 + ROT_HALF, new - (dh // 2 - ROT_HALF), new)))
    src = jnp.where(pl.program_id(1) < 2 * D_MODEL // PROJ_TN, src, new)
    perm = jnp.where(old == src, 1.0, 0.0).astype(BF16)
    w = w_ref[...].astype(BF16)
    parts = [_dot(w[:, hd * dh:(hd + 1) * dh], perm) for hd in range(PROJ_TN // dh)]
    o_ref[...] = jnp.concatenate(parts, axis=1).astype(BF16)


def _cast_qkv(w):
    r, c = w.shape
    spec = pl.BlockSpec((CAST_ROWS, PROJ_TN), lambda i, j: (i, j))
    return pl.pallas_call(
        _cast_qkv_body,
        grid=(r // CAST_ROWS, c // PROJ_TN),
        in_specs=[spec],
        out_specs=spec,
        out_shape=jax.ShapeDtypeStruct(w.shape, BF16),
        compiler_params=_params("parallel", "parallel"),
        name="cast_qkv",
    )(w)


def _proj_c_body(x_ref, g_ref, w_ref, cf_ref, sg_ref, o_ref, xn_ref):
    @pl.when(pl.program_id(1) == 0)
    def _():
        xn_ref[...] = _rms(x_ref[...], g_ref[...]).astype(BF16)

    y = _dot(xn_ref[...], w_ref[...])
    cf = cf_ref[...]
    sg = sg_ref[...]
    parts = []
    for hd in range(PROJ_TN // DIFF_HEAD_DIM):
        lo = hd * DIFF_HEAD_DIM
        yh = y[:, lo:lo + DIFF_HEAD_DIM]
        r = yh * cf + pltpu.roll(yh, DIFF_HEAD_DIM // 2, axis=1) * sg
        parts.append(r.astype(BF16))
    o_ref[...] = jnp.concatenate(parts, axis=1)


def _proj_c(h, norm_g, w, cf, sg, seq):
    n, d = h.shape
    nout = w.shape[1]
    tm = min(1024, seq)
    per_seq = seq // tm
    qk_tiles = 2 * D_MODEL // PROJ_TN
    tab = pl.BlockSpec((None, tm, DIFF_HEAD_DIM), lambda i, j: (j // qk_tiles, i % per_seq, 0))
    return pl.pallas_call(
        _proj_c_body,
        grid=(n // tm, nout // PROJ_TN),
        in_specs=[
            pl.BlockSpec((tm, d), lambda i, j: (i, 0)),
            pl.BlockSpec((1, d), lambda i, j: (0, 0)),
            pl.BlockSpec((d, PROJ_TN), lambda i, j: (0, j)),
            tab, tab,
        ],
        out_specs=pl.BlockSpec((tm, PROJ_TN), lambda i, j: (i, j)),
        out_shape=jax.ShapeDtypeStruct((n, nout), BF16),
        scratch_shapes=[pltpu.VMEM((tm, d), BF16)],
        compiler_params=_params("parallel", "arbitrary"),
        name="proj_c",
    )(h, norm_g.reshape(1, d), w, cf, sg)


def _ret_body(q_ref, k_ref, v_ref, g_ref, dm_ref, qd_ref, kd_ref, cd_ref, o_ref, *, nblk):
    t = RET_BLOCK
    dmask = dm_ref[...]
    qdec = qd_ref[...]
    kdec = kd_ref[...]
    cdec = cd_ref[...]

    def rows(ref, n):
        return ref[pl.ds(pl.multiple_of(n * t, t), t), :]

    def step(n, state):
        q = rows(q_ref, n)
        k = rows(k_ref, n)
        vb = rows(v_ref, n).astype(BF16)
        scores = _dot_nt(q.astype(BF16), k.astype(BF16)) * dmask
        intra = _dot(scores.astype(BF16), vb)
        inter = _dot((q * qdec).astype(BF16), state.astype(BF16))
        new_state = state * cdec + _dot_tn((k * kdec).astype(BF16), vb)
        o = intra + inter
        o = o * lax.rsqrt(jnp.mean(o * o, axis=-1, keepdims=True) + NORM_EPS)
        gate = rows(g_ref, n)
        r0 = pl.multiple_of(n * t, t)
        o_ref[pl.ds(r0, t), :] = (o * (gate * jax.nn.sigmoid(gate))).astype(BF16)
        return new_state

    lax.fori_loop(0, nblk, step, jnp.zeros((RET_HEAD_DIM, RET_HEAD_DIM), F32))


def _retention(proj, batch, seq):
    n = proj.shape[0]
    t = RET_BLOCK
    hd = RET_HEAD_DIM
    log_g = jnp.log(1.0 - 2.0 ** (-5.0 - jnp.arange(RET_HEADS, dtype=F32)))
    idx = jnp.arange(t, dtype=F32)
    chunk = jnp.arange(t) // CHUNK
    visible = chunk[:, None] >= chunk[None, :]
    dist = jnp.abs(idx[:, None] - idx[None, :])
    dmask = jnp.where(visible[None], jnp.exp(log_g[:, None, None] * dist[None]), 0.0)
    qdec = jnp.broadcast_to(jnp.exp(log_g[:, None] * (idx[None] + 1.0))[:, :, None], (RET_HEADS, t, hd))
    kdec = jnp.broadcast_to(jnp.exp(log_g[:, None] * (t - 1.0 - idx[None]))[:, :, None], (RET_HEADS, t, hd))
    cdec = jnp.broadcast_to(jnp.exp(log_g * t)[:, None, None], (RET_HEADS, 1, hd))

    def col(off):
        return pl.BlockSpec((seq, hd), lambda b, h: (b, off + h))

    def tab(rows, cols):
        return pl.BlockSpec((None, rows, cols), lambda b, h: (h, 0, 0))

    return pl.pallas_call(
        functools.partial(_ret_body, nblk=seq // t),
        grid=(batch, RET_HEADS),
        in_specs=[col(0), col(RET_HEADS), col(2 * RET_HEADS), col(3 * RET_HEADS),
                  tab(t, t), tab(t, hd), tab(t, hd), tab(1, hd)],
        out_specs=pl.BlockSpec((seq, hd), lambda b, h: (b, h)),
        out_shape=jax.ShapeDtypeStruct((n, RET_WIDTH), BF16),
        compiler_params=_params("parallel", "parallel"),
        name="retention",
    )(proj, proj, proj, proj, dmask, qdec, kdec, cdec)


def _s5_weights(lam_re, lam_im, log_step, b_re, b_im, c_re, c_im):
    hp = lax.Precision.HIGHEST
    step = jnp.exp(log_step)[:, None]
    mag = jnp.exp(lam_re * step)
    a_re = mag * jnp.cos(lam_im * step)
    a_im = mag * jnp.sin(lam_im * step)
    den = lam_re * lam_re + lam_im * lam_im
    nr = a_re - 1.0
    f_re = (nr * lam_re + a_im * lam_im) / den
    f_im = (a_im * lam_re - nr * lam_im) / den
    bb_re = f_re[..., None] * b_re - f_im[..., None] * b_im
    bb_im = f_re[..., None] * b_im + f_im[..., None] * b_re
    prs, pis = [jnp.ones_like(a_re)], [jnp.zeros_like(a_im)]
    for _ in range(SSM_T):
        prs.append(prs[-1] * a_re - pis[-1] * a_im)
        pis.append(prs[-2] * a_im + pis[-1] * a_re)
    pr = jnp.stack(prs)
    pi = jnp.stack(pis)
    ca_re = c_re[None] * pr[:, :, None, :] - c_im[None] * pi[:, :, None, :]
    ca_im = c_re[None] * pi[:, :, None, :] + c_im[None] * pr[:, :, None, :]
    kern = (jnp.einsum('kgpn,gnq->gkpq', ca_re[:SSM_T], bb_re, precision=hp)
            - jnp.einsum('kgpn,gnq->gkpq', ca_im[:SSM_T], bb_im, precision=hp))
    nj = SSM_NJ
    ks = kern.transpose(0, 3, 1, 2).reshape(nj, 128, SSM_T * SSM_GROUP)
    ro = jnp.stack([ca_re[1:], -ca_im[1:]])
    ro = ro.reshape(2, SSM_T, nj, SSM_GPB, SSM_GROUP, SSM_STATE).transpose(2, 0, 3, 5, 1, 4)
    wo = ro.reshape(nj, 2 * SSM_GPB * SSM_STATE, SSM_T * SSM_GROUP)
    rev_re = jnp.stack([prs[SSM_T - 1 - s] for s in range(SSM_T)])
    rev_im = jnp.stack([pis[SSM_T - 1 - s] for s in range(SSM_T)])
    in_re = rev_re[..., None] * bb_re[None] - rev_im[..., None] * bb_im[None]
    in_im = rev_re[..., None] * bb_im[None] + rev_im[..., None] * bb_re[None]
    wi = jnp.stack([in_re, in_im]).reshape(2, SSM_T, nj, SSM_GPB, SSM_STATE, SSM_GROUP)
    wi = wi.transpose(2, 1, 3, 5, 0, 4).reshape(nj, SSM_T * 128, 2 * SSM_STATE)
    a_t = jnp.stack([pr[SSM_T], pi[SSM_T]]).reshape(2, nj, SSM_GPB * SSM_STATE)
    a_t = jnp.moveaxis(a_t, 0, 1)
    return ks, wo, wi, a_t


def _s5_expand_body(ks_ref, wo_ref, wi_ref, wy_ref, win_ref):
    tp = SSM_T * SSM_GROUP
    wide = SSM_T * 128
    half = SSM_GPB * SSM_STATE

    def iota(shape, dim):
        return lax.broadcasted_iota(jnp.int32, shape, dim)

    r = iota((tp, wide), 0)
    c = iota((tp, wide), 1)
    rep = jnp.where((r // SSM_GROUP == c // 128) & (r % SSM_GROUP == c % SSM_GROUP), 1.0, 0.0).astype(BF16)
    col_b = (iota((1, wide), 1) % 128) // SSM_GROUP

    row_a = iota((128, 1), 0) // SSM_GROUP
    bdk = jnp.where(row_a == col_b, _dot(ks_ref[...].astype(BF16), rep), 0.0).astype(BF16)
    for s in range(SSM_T):
        if s:
            wy_ref[s * 128:(s + 1) * 128, :s * 128] = jnp.zeros((128, s * 128), BF16)
        wy_ref[s * 128:(s + 1) * 128, s * 128:] = bdk[:, :wide - s * 128]

    row_a = (iota((2 * half, 1), 0) % half) // SSM_STATE
    ro = _dot(wo_ref[...].astype(BF16), rep)
    wy_ref[wide:, :] = jnp.where(row_a == col_b, ro, 0.0).astype(BF16)

    r = iota((2 * SSM_STATE, 2 * half), 0)
    c = iota((2 * SSM_STATE, 2 * half), 1)
    rep_in = jnp.where((r // SSM_STATE == c // half) & (r % SSM_STATE == c % SSM_STATE), 1.0, 0.0).astype(BF16)
    row_a = (iota((wide, 1), 0) % 128) // SSM_GROUP
    col_b = (iota((1, 2 * half), 1) % half) // SSM_STATE
    win = _dot(wi_ref[...].astype(BF16), rep_in)
    win_ref[...] = jnp.where(row_a == col_b, win, 0.0).astype(BF16)


def _s5_expand(ks, wo, wi):
    nj = ks.shape[0]
    wide = SSM_T * 128
    half2 = 2 * SSM_GPB * SSM_STATE

    def blk(a):
        return pl.BlockSpec((None,) + a.shape[1:], lambda j: (j, 0, 0))

    return pl.pallas_call(
        _s5_expand_body,
        grid=(nj,),
        in_specs=[blk(ks), blk(wo), blk(wi)],
        out_specs=[pl.BlockSpec((None, wide + half2, wide), lambda j: (j, 0, 0)),
                   pl.BlockSpec((None, wide, half2), lambda j: (j, 0, 0))],
        out_shape=[jax.ShapeDtypeStruct((nj, wide + half2, wide), BF16),
                   jax.ShapeDtypeStruct((nj, wide, half2), BF16)],
        compiler_params=_params("parallel"),
        name="s5_expand",
    )(ks, wo, wi)


def _s5_body(u_ref, wy_ref, win_ref, at_ref, y_ref, s_scr, hp_scr, *, nb, nchunk):
    half = SSM_GPB * SSM_STATE
    r = nb * nchunk
    u = jnp.concatenate([u_ref[pl.ds(s, r, stride=SSM_T), :] for s in range(SSM_T)],
                        axis=1).astype(BF16)
    nslab = half // 128
    s_all = _dot(u, win_ref[...])
    for k in range(2 * nslab):
        s_scr[k] = s_all[:, k * 128:(k + 1) * 128]
    a_re = [at_ref[0:1, k * 128:(k + 1) * 128] for k in range(nslab)]
    a_im = [at_ref[1:2, k * 128:(k + 1) * 128] for k in range(nslab)]
    h_re = [jnp.zeros((nb, 128), F32)] * nslab
    h_im = [jnp.zeros((nb, 128), F32)] * nslab
    for c in range(nchunk):
        chunk_rows = pl.ds(c, nb, stride=nchunk)
        for k in range(nslab):
            hp_scr[k, chunk_rows, :] = h_re[k]
            hp_scr[nslab + k, chunk_rows, :] = h_im[k]
            s_re = s_scr[k, chunk_rows, :]
            s_im = s_scr[nslab + k, chunk_rows, :]
            h_re[k], h_im[k] = (a_re[k] * h_re[k] - a_im[k] * h_im[k] + s_re,
                                a_re[k] * h_im[k] + a_im[k] * h_re[k] + s_im)
    hp = jnp.concatenate([hp_scr[k] for k in range(2 * nslab)], axis=1).astype(BF16)
    wide = SSM_T * 128
    cb = 256
    for lo in range(0, wide, cb):
        y = (_dot(u[:, :lo + cb], wy_ref[:lo + cb, lo:lo + cb])
             + _dot(hp, wy_ref[wide:, lo:lo + cb]))
        for t in range(lo // 128, (lo + cb) // 128):
            y_ref[pl.ds(t, r, stride=SSM_T), :] = y[:, t * 128 - lo:(t + 1) * 128 - lo]


def _s5(proj, w_y, w_in, a_t, batch, seq):
    n, width = proj.shape
    nchunk = seq // SSM_T
    nb = min(4, batch)
    r = nb * nchunk
    ucol0 = (width - SSM_WIDTH) // 128
    half2 = 2 * SSM_GPB * SSM_STATE
    once = pl.Buffered(1)
    return pl.pallas_call(
        functools.partial(_s5_body, nb=nb, nchunk=nchunk),
        grid=(SSM_NJ, batch // nb),
        in_specs=[
            pl.BlockSpec((r * SSM_T, 128), lambda j, i: (i, ucol0 + j)),
            pl.BlockSpec((None, SSM_T * 128 + half2, SSM_T * 128), lambda j, i: (j, 0, 0), pipeline_mode=once),
            pl.BlockSpec((None, SSM_T * 128, half2), lambda j, i: (j, 0, 0), pipeline_mode=once),
            pl.BlockSpec((None, 2, half2 // 2), lambda j, i: (j, 0, 0)),
        ],
        out_specs=pl.BlockSpec((r * SSM_T, 128), lambda j, i: (i, j)),
        out_shape=jax.ShapeDtypeStruct((n, SSM_WIDTH), F32),
        scratch_shapes=[pltpu.VMEM((half2 // 128, r, 128), F32), pltpu.VMEM((half2 // 128, r, 128), F32)],
        compiler_params=_params("arbitrary", "arbitrary"),
        name="s5",
    )(proj, w_y, w_in, a_t)


def _gelu(x):
    return 0.5 * x * (1.0 + jnp.tanh(math.sqrt(2.0 / math.pi) * (x + 0.044715 * (x * x * x))))


def _ab_out_body(ya_ref, ys_ref, u_ref, d_ref, wglu_ref, bglu_ref, woa_ref, wob_ref, h_ref, o_ref):
    y = ys_ref[...] + d_ref[...] * u_ref[...]
    z = _gelu(y)
    gate = jax.nn.sigmoid(_dot(z.astype(BF16), wglu_ref[...]) + bglu_ref[...])
    yb = (z * gate).astype(BF16)
    o_ref[...] = h_ref[...] + (_dot(ya_ref[...], woa_ref[...]) + _dot(yb, wob_ref[...]))


def _ab_out(y_a, y_ssm, proj, d_skip, w_glu, b_glu, w_out, h):
    n, d = h.shape
    tm = min(512, n)
    ucol = (proj.shape[1] - SSM_WIDTH) // SSM_WIDTH
    row = lambda i: (i, 0)
    fixed = lambda i: (0, 0)
    return pl.pallas_call(
        _ab_out_body,
        grid=(n // tm,),
        in_specs=[
            pl.BlockSpec((tm, RET_WIDTH), row),
            pl.BlockSpec((tm, SSM_WIDTH), row),
            pl.BlockSpec((tm, SSM_WIDTH), lambda i: (i, ucol)),
            pl.BlockSpec((1, SSM_WIDTH), fixed),
            pl.BlockSpec((SSM_WIDTH, SSM_WIDTH), fixed),
            pl.BlockSpec((1, SSM_WIDTH), fixed),
            pl.BlockSpec((RET_WIDTH, d), fixed),
            pl.BlockSpec((SSM_WIDTH, d), lambda i: (1, 0)),
            pl.BlockSpec((tm, d), row),
        ],
        out_specs=pl.BlockSpec((tm, d), row),
        out_shape=jax.ShapeDtypeStruct((n, d), F32),
        compiler_params=_params("parallel"),
        name="ab_out",
    )(y_a, y_ssm, proj, d_skip.reshape(1, -1), w_glu, b_glu.reshape(1, -1), w_out, w_out, h)


def _att_body(lam_ref, q_ref, k_ref, v_ref, sub_ref, o_ref, *, seq, lambda_init):
    tq = min(ATT_QBLOCK, seq)
    dh = DIFF_HEAD_DIM
    k2 = dh ** -0.5 * math.log2(math.e)
    lam = lam_ref[0]
    neg = jnp.finfo(F32).min
    rc = lax.broadcasted_iota(jnp.int32, (tq, tq), 0) // CHUNK
    cc = lax.broadcasted_iota(jnp.int32, (tq, tq), 1) // CHUNK
    visible = rc >= cc
    for qb in range(seq // tq):
        q0 = qb * tq
        q = q_ref[q0:q0 + tq, :]
        v_diag = v_ref[q0:q0 + tq, :]
        comps = []
        for c in range(2):
            qc = q[:, c * dh:(c + 1) * dh]
            s_diag = _dot_nt(qc, k_ref[q0:q0 + tq, c * dh:(c + 1) * dh])
            s_diag = jnp.where(visible, s_diag, neg)
            m = jnp.max(s_diag, axis=-1, keepdims=True)
            if qb > 0:
                s_off = _dot_nt(qc, k_ref[0:q0, c * dh:(c + 1) * dh])
                m = jnp.maximum(m, jnp.max(s_off, axis=-1, keepdims=True))
            mk = m * k2
            p_diag = jnp.exp2(s_diag * k2 - mk)
            l = jnp.sum(p_diag, axis=-1, keepdims=True)
            acc = _dot(p_diag.astype(BF16), v_diag)
            if qb > 0:
                p_off = jnp.exp2(s_off * k2 - mk)
                l = l + jnp.sum(p_off, axis=-1, keepdims=True)
                acc = acc + _dot(p_off.astype(BF16), v_ref[0:q0, :])
            comps.append(acc / l)
        o = comps[0] - lam * comps[1]
        o = _rms(o, sub_ref[...]) * (1.0 - lambda_init)
        o_ref[q0:q0 + tq, :] = o.astype(BF16)


def _attention(qkv, lam, subln, batch, seq, lambda_init):
    n = qkv.shape[0]
    w = 2 * DIFF_HEAD_DIM

    def col(off):
        return pl.BlockSpec((seq, w), lambda b, h: (b, off + h))

    return pl.pallas_call(
        functools.partial(_att_body, seq=seq, lambda_init=lambda_init),
        grid=(batch, DIFF_HEADS),
        in_specs=[
            pl.BlockSpec(memory_space=pltpu.SMEM),
            col(0), col(DIFF_HEADS), col(2 * DIFF_HEADS),
            pl.BlockSpec((1, w), lambda b, h: (0, 0)),
        ],
        out_specs=pl.BlockSpec((seq, w), lambda b, h: (b, h)),
        out_shape=jax.ShapeDtypeStruct((n, D_MODEL), BF16),
        compiler_params=_params("parallel", "parallel"),
        name="diff_attention",
    )(lam.reshape(1), qkv, qkv, qkv, subln.reshape(1, w))


def _out_proj_body(o_ref, w_ref, h_ref, y_ref):
    y_ref[...] = h_ref[...] + _dot(o_ref[...], w_ref[...])


def _out_proj(o, w, h):
    n, d = h.shape
    tm = min(512, n)
    return pl.pallas_call(
        _out_proj_body,
        grid=(n // tm,),
        in_specs=[
            pl.BlockSpec((tm, d), lambda i: (i, 0)),
            pl.BlockSpec((d, d), lambda i: (0, 0)),
            pl.BlockSpec((tm, d), lambda i: (i, 0)),
        ],
        out_specs=pl.BlockSpec((tm, d), lambda i: (i, 0)),
        out_shape=jax.ShapeDtypeStruct((n, d), F32),
        compiler_params=_params("parallel"),
        name="out_proj",
    )(o, w, h)


def _rope_tables(seq, rot_dim, theta):
    inv = 1.0 / (theta ** (jnp.arange(0, rot_dim, 2, dtype=F32) / rot_dim))
    ang = jnp.arange(seq, dtype=F32)[:, None] * inv[None, :]
    return jnp.cos(ang), jnp.sin(ang)


def kernel(x, ffn_norm, ffn_w_gate, ffn_w_up, ffn_w_down, mix_norm, ab_w_in, ab_w_out, ssm_lambda_re, ssm_lambda_im, ssm_log_step, ssm_b_re, ssm_b_im, ssm_c_re, ssm_c_im, ssm_d, ssm_w_glu, ssm_b_glu, c_w_qkv, c_w_out, c_lambda_q1, c_lambda_k1, c_lambda_q2, c_lambda_k2, c_subln, final_norm):
    batch, seq, d = x.shape
    n = batch * seq
    h = x.reshape(n, d)
    wg = _cast_stack(ffn_w_gate)
    wu = _cast_stack(ffn_w_up)
    wd = _cast_stack(ffn_w_down)

    def ffn(h, layer, half, final=False):
        return _ffn(h, ffn_norm[layer, half], wg, wu, wd, final_norm, layer, half, final=final)

    h = ffn(h, 0, 0)
    ret_cos, ret_sin = _rope_tables(seq, RET_HEAD_DIM, RET_THETA)
    k_scale = RET_HEAD_DIM ** -0.5
    cos_tab = jnp.stack([ret_cos, ret_cos * k_scale, jnp.ones_like(ret_cos)])
    sin_tab = jnp.stack([ret_sin, ret_sin * k_scale, jnp.zeros_like(ret_sin)])
    proj = _proj_ab(h, mix_norm[0], ab_w_in[0].astype(BF16), cos_tab, sin_tab, seq)
    y_a = _retention(proj, batch, seq)
    ks, wo, wi, a_t = _s5_weights(ssm_lambda_re[0], ssm_lambda_im[0], ssm_log_step[0], ssm_b_re[0],
                                  ssm_b_im[0], ssm_c_re[0], ssm_c_im[0])
    w_y, w_in = _s5_expand(ks, wo, wi)
    y_ssm = _s5(proj, w_y, w_in, a_t, batch, seq)
    h = _ab_out(y_a, y_ssm, proj, ssm_d[0], ssm_w_glu[0].astype(BF16), ssm_b_glu[0],
                ab_w_out[0].astype(BF16), h)
    h = ffn(h, 0, 1)

    h = ffn(h, 1, 0)
    att_cos, att_sin = _rope_tables(seq, 2 * ROT_HALF, ROPE_THETA)
    rest = jnp.zeros((seq, DIFF_HEAD_DIM // 2 - ROT_HALF), F32)
    ident = jnp.zeros((seq, DIFF_HEAD_DIM), F32)
    cf = jnp.stack([jnp.concatenate([att_cos, rest + 1.0, att_cos, rest + 1.0], axis=1), ident + 1.0])
    sg = jnp.stack([jnp.concatenate([-att_sin, rest, att_sin, rest], axis=1), ident])
    qkv = _proj_c(h, mix_norm[1], _cast_qkv(c_w_qkv[0]), cf, sg, seq)
    lambda_init = 0.8 - 0.6 * math.exp(-0.3 * 1)
    lam = (jnp.exp(jnp.sum(c_lambda_q1[0] * c_lambda_k1[0]))
           - jnp.exp(jnp.sum(c_lambda_q2[0] * c_lambda_k2[0])) + lambda_init)
    o = _attention(qkv, lam, c_subln[0], batch, seq, lambda_init)
    h = _out_proj(o, c_w_out[0].astype(BF16), h)
    h = ffn(h, 1, 1, final=True)
    return h.reshape(batch, seq, d)
```

```python
import functools
import math

import jax
import jax.numpy as jnp
import numpy as np
from jax import lax
from jax.experimental import pallas as pl
from jax.experimental.pallas import tpu as pltpu

F32 = jnp.float32
BF16 = jnp.bfloat16

D_MODEL = 2048
CHUNK = 64
NORM_EPS = 1e-6

RET_WIDTH = 1024
RET_HEADS = 4
RET_HEAD_DIM = 256
RET_THETA = 10000.0
RET_BLOCK = 256

SSM_WIDTH = 1024
SSM_GROUP = 16
SSM_GROUPS = 64
SSM_STATE = 64
SSM_T = 16
SSM_GPB = 128 // SSM_GROUP
SSM_NJ = SSM_WIDTH // 128

DIFF_HEAD_DIM = 128
DIFF_HEADS = 8
ROPE_THETA = 500000.0
ROT_HALF = DIFF_HEAD_DIM // 4 // 2
ATT_QBLOCK = 256

FF_TILE = 1024
CAST_ROWS = 512
PROJ_TN = 1024
VMEM_LIMIT = 56 * 1024 * 1024


def _params(*sem):
    return pltpu.CompilerParams(dimension_semantics=sem, vmem_limit_bytes=VMEM_LIMIT)


def _rms(x, g):
    y = x * lax.rsqrt(jnp.mean(x * x, axis=-1, keepdims=True) + NORM_EPS)
    return y * g


def _dot(a, b):
    return jnp.dot(a, b, preferred_element_type=F32)


def _dot_nt(a, b):
    return lax.dot_general(a, b, (((1,), (1,)), ((), ())), preferred_element_type=F32)


def _dot_tn(a, b):
    return lax.dot_general(a, b, (((0,), (0,)), ((), ())), preferred_element_type=F32)


def _cast_body(w_ref, o_ref):
    o_ref[...] = w_ref[...].astype(BF16)


def _cast_stack(w):
    a, b, r, c = w.shape
    spec = pl.BlockSpec((None, None, CAST_ROWS, c), lambda s, i: (s // b, s % b, i, 0))
    return pl.pallas_call(
        _cast_body,
        grid=(a * b, pl.cdiv(r, CAST_ROWS)),
        in_specs=[spec],
        out_specs=spec,
        out_shape=jax.ShapeDtypeStruct(w.shape, BF16),
        compiler_params=_params("parallel", "parallel"),
        name="cast_bf16",
    )(w)


def _ffn_body(x_ref, g_ref, wg_ref, wu_ref, wd_ref, fn_ref, o_ref, xn_ref, acc_ref, *, final, tail):
    j = pl.program_id(1)
    last = pl.num_programs(1) - 1

    def tile(width):
        xn = xn_ref[...]
        gate = _dot(xn, wg_ref[:, :width])
        up = _dot(xn, wu_ref[:, :width])
        act = (gate * jax.nn.sigmoid(gate) * up).astype(BF16)
        return _dot(act, wd_ref[:width, :])

    @pl.when(j == 0)
    def _():
        xn_ref[...] = _rms(x_ref[...], g_ref[...]).astype(BF16)
        acc_ref[...] = tile(tail)

    @pl.when((j > 0) & (j < last))
    def _():
        acc_ref[...] += tile(FF_TILE)

    @pl.when(j == last)
    def _():
        h = x_ref[...] + 0.5 * (acc_ref[...] + tile(FF_TILE))
        if final:
            h = _rms(h, fn_ref[...])
        o_ref[...] = h


def _ffn(h, norm_g, wg, wu, wd, final_g, layer, half, *, final):
    n, d = h.shape
    tm = min(512, n)
    ff = wg.shape[-1]
    steps = pl.cdiv(ff, FF_TILE)
    tail = ff - (steps - 1) * FF_TILE
    return pl.pallas_call(
        functools.partial(_ffn_body, final=final, tail=tail),
        grid=(n // tm, steps),
        in_specs=[
            pl.BlockSpec((tm, d), lambda i, j: (i, 0)),
            pl.BlockSpec((1, d), lambda i, j: (0, 0)),
            pl.BlockSpec((None, None, d, FF_TILE), lambda i, j: (layer, half, 0, (j + steps - 1) % steps)),
            pl.BlockSpec((None, None, d, FF_TILE), lambda i, j: (layer, half, 0, (j + steps - 1) % steps)),
            pl.BlockSpec((None, None, FF_TILE, d), lambda i, j: (layer, half, (j + steps - 1) % steps, 0)),
            pl.BlockSpec((1, d), lambda i, j: (0, 0)),
        ],
        out_specs=pl.BlockSpec((tm, d), lambda i, j: (i, 0)),
        out_shape=jax.ShapeDtypeStruct((n, d), F32),
        scratch_shapes=[pltpu.VMEM((tm, d), BF16), pltpu.VMEM((tm, d), F32)],
        compiler_params=_params("parallel", "arbitrary"),
        name="ffn",
    )(h, norm_g.reshape(1, d), wg, wu, wd, final_g.reshape(1, d))


def _proj_ab_body(x_ref, g_ref, w_ref, cos_ref, sin_ref, o_ref, xn_ref):
    @pl.when(pl.program_id(1) == 0)
    def _():
        xn_ref[...] = _rms(x_ref[...], g_ref[...]).astype(BF16)

    y = _dot(xn_ref[...], w_ref[...])
    c = cos_ref[...]
    s = sin_ref[...]
    half = RET_HEAD_DIM // 2
    parts = []
    for hd in range(PROJ_TN // RET_HEAD_DIM):
        lo = hd * RET_HEAD_DIM
        x1 = y[:, lo:lo + half]
        x2 = y[:, lo + half:lo + 2 * half]
        parts += [x1 * c - x2 * s, x2 * c + x1 * s]
    o_ref[...] = jnp.concatenate(parts, axis=1)


def _proj_ab(h, norm_g, w, cos_tab, sin_tab, seq):
    n, d = h.shape
    nout = w.shape[1]
    tm = min(1024, seq)
    per_seq = seq // tm
    tab = pl.BlockSpec((None, tm, RET_HEAD_DIM // 2), lambda i, j: (jnp.minimum(j, 2), i % per_seq, 0))
    return pl.pallas_call(
        _proj_ab_body,
        grid=(n // tm, nout // PROJ_TN),
        in_specs=[
            pl.BlockSpec((tm, d), lambda i, j: (i, 0)),
            pl.BlockSpec((1, d), lambda i, j: (0, 0)),
            pl.BlockSpec((d, PROJ_TN), lambda i, j: (0, j)),
            tab, tab,
        ],
        out_specs=pl.BlockSpec((tm, PROJ_TN), lambda i, j: (i, j)),
        out_shape=jax.ShapeDtypeStruct((n, nout), F32),
        scratch_shapes=[pltpu.VMEM((tm, d), BF16)],
        compiler_params=_params("parallel", "arbitrary"),
        name="proj_ab",
    )(h, norm_g.reshape(1, d), w, cos_tab, sin_tab)


def _cast_qkv_body(w_ref, o_ref):
    dh = DIFF_HEAD_DIM
    old = lax.broadcasted_iota(jnp.int32, (dh, dh), 0)
    new = lax.broadcasted_iota(jnp.int32, (dh, dh), 1)
    src = jnp.where(new < ROT_HALF, new,
                    jnp.where(new < dh // 2, new + ROT_HALF,
                              jnp.where(new < dh // 2 + ROT_HALF, new - (dh // 2 - ROT_HALF), new)))
    src = jnp.where(pl.program_id(1) < 2 * D_MODEL // PROJ_TN, src, new)
    perm = jnp.where(old == src, 1.0, 0.0).astype(BF16)
    w = w_ref[...].astype(BF16)
    parts = [_dot(w[:, hd * dh:(hd + 1) * dh], perm) for hd in range(PROJ_TN // dh)]
    o_ref[...] = jnp.concatenate(parts, axis=1).astype(BF16)


def _cast_qkv(w):
    r, c = w.shape
    spec = pl.BlockSpec((CAST_ROWS, PROJ_TN), lambda i, j: (i, j))
    return pl.pallas_call(
        _cast_qkv_body,
        grid=(r // CAST_ROWS, c // PROJ_TN),
        in_specs=[spec],
        out_specs=spec,
        out_shape=jax.ShapeDtypeStruct(w.shape, BF16),
        compiler_params=_params("parallel", "parallel"),
        name="cast_qkv",
    )(w)


def _proj_c_body(x_ref, g_ref, w_ref, cf_ref, sg_ref, o_ref, xn_ref):
    @pl.when(pl.program_id(1) == 0)
    def _():
        xn_ref[...] = _rms(x_ref[...], g_ref[...]).astype(BF16)

    y = _dot(xn_ref[...], w_ref[...])
    cf = cf_ref[...]
    sg = sg_ref[...]
    parts = []
    for hd in range(PROJ_TN // DIFF_HEAD_DIM):
        lo = hd * DIFF_HEAD_DIM
        yh = y[:, lo:lo + DIFF_HEAD_DIM]
        r = yh * cf + pltpu.roll(yh, DIFF_HEAD_DIM // 2, axis=1) * sg
        parts.append(r.astype(BF16))
    o_ref[...] = jnp.concatenate(parts, axis=1)


def _proj_c(h, norm_g, w, cf, sg, seq):
    n, d = h.shape
    nout = w.shape[1]
    tm = min(1024, seq)
    per_seq = seq // tm
    qk_tiles = 2 * D_MODEL // PROJ_TN
    tab = pl.BlockSpec((None, tm, DIFF_HEAD_DIM), lambda i, j: (j // qk_tiles, i % per_seq, 0))
    return pl.pallas_call(
        _proj_c_body,
        grid=(n // tm, nout // PROJ_TN),
        in_specs=[
            pl.BlockSpec((tm, d), lambda i, j: (i, 0)),
            pl.BlockSpec((1, d), lambda i, j: (0, 0)),
            pl.BlockSpec((d, PROJ_TN), lambda i, j: (0, j)),
            tab, tab,
        ],
        out_specs=pl.BlockSpec((tm, PROJ_TN), lambda i, j: (i, j)),
        out_shape=jax.ShapeDtypeStruct((n, nout), BF16),
        scratch_shapes=[pltpu.VMEM((tm, d), BF16)],
        compiler_params=_params("parallel", "arbitrary"),
        name="proj_c",
    )(h, norm_g.reshape(1, d), w, cf, sg)


def _ret_body(q_ref, k_ref, v_ref, g_ref, dm_ref, qd_ref, kd_ref, cd_ref, o_ref, *, nblk):
    t = RET_BLOCK
    dmask = dm_ref[...]
    qdec = qd_ref[...]
    kdec = kd_ref[...]
    cdec = cd_ref[...]

    def rows(ref, n):
        return ref[pl.ds(pl.multiple_of(n * t, t), t), :]

    def step(n, state):
        q = rows(q_ref, n)
        k = rows(k_ref, n)
        vb = rows(v_ref, n).astype(BF16)
        scores = _dot_nt(q.astype(BF16), k.astype(BF16)) * dmask
        intra = _dot(scores.astype(BF16), vb)
        inter = _dot((q * qdec).astype(BF16), state.astype(BF16))
        new_state = state * cdec + _dot_tn((k * kdec).astype(BF16), vb)
        o = intra + inter
        o = o * lax.rsqrt(jnp.mean(o * o, axis=-1, keepdims=True) + NORM_EPS)
        gate = rows(g_ref, n)
        r0 = pl.multiple_of(n * t, t)
        o_ref[pl.ds(r0, t), :] = (o * (gate * jax.nn.sigmoid(gate))).astype(BF16)
        return new_state

    lax.fori_loop(0, nblk, step, jnp.zeros((RET_HEAD_DIM, RET_HEAD_DIM), F32), unroll=True)


def _retention(proj, batch, seq):
    n = proj.shape[0]
    t = RET_BLOCK
    hd = RET_HEAD_DIM
    log_g = np.log(1.0 - 2.0 ** (-5.0 - np.arange(RET_HEADS, dtype=np.float64)))
    idx = np.arange(t, dtype=np.float64)
    chunk = np.arange(t) // CHUNK
    visible = chunk[:, None] >= chunk[None, :]
    dist = np.abs(idx[:, None] - idx[None, :])
    dmask = np.where(visible[None], np.exp(log_g[:, None, None] * dist[None]), 0.0)
    f32 = np.float32
    dmask = dmask.astype(f32)
    qdec = np.broadcast_to(np.exp(log_g[:, None] * (idx[None] + 1.0))[:, :, None], (RET_HEADS, t, hd)).astype(f32)
    kdec = np.broadcast_to(np.exp(log_g[:, None] * (t - 1.0 - idx[None]))[:, :, None], (RET_HEADS, t, hd)).astype(f32)
    cdec = np.broadcast_to(np.exp(log_g * t)[:, None, None], (RET_HEADS, 1, hd)).astype(f32)

    def col(off):
        return pl.BlockSpec((seq, hd), lambda b, h: (b, off + h))

    def tab(rows, cols):
        return pl.BlockSpec((None, rows, cols), lambda b, h: (h, 0, 0))

    return pl.pallas_call(
        functools.partial(_ret_body, nblk=seq // t),
        grid=(batch, RET_HEADS),
        in_specs=[col(0), col(RET_HEADS), col(2 * RET_HEADS), col(3 * RET_HEADS),
                  tab(t, t), tab(t, hd), tab(t, hd), tab(1, hd)],
        out_specs=pl.BlockSpec((seq, hd), lambda b, h: (b, h)),
        out_shape=jax.ShapeDtypeStruct((n, RET_WIDTH), BF16),
        compiler_params=_params("parallel", "parallel"),
        name="retention",
    )(proj, proj, proj, proj, dmask, qdec, kdec, cdec)


def _s5_weights(lam_re, lam_im, log_step, b_re, b_im, c_re, c_im):
    hp = lax.Precision.HIGHEST
    step = jnp.exp(log_step)[:, None]
    mag = jnp.exp(lam_re * step)
    a_re = mag * jnp.cos(lam_im * step)
    a_im = mag * jnp.sin(lam_im * step)
    den = lam_re * lam_re + lam_im * lam_im
    nr = a_re - 1.0
    f_re = (nr * lam_re + a_im * lam_im) / den
    f_im = (a_im * lam_re - nr * lam_im) / den
    bb_re = f_re[..., None] * b_re - f_im[..., None] * b_im
    bb_im = f_re[..., None] * b_im + f_im[..., None] * b_re
    prs, pis = [jnp.ones_like(a_re)], [jnp.zeros_like(a_im)]
    for _ in range(SSM_T):
        prs.append(prs[-1] * a_re - pis[-1] * a_im)
        pis.append(prs[-2] * a_im + pis[-1] * a_re)
    pr = jnp.stack(prs)
    pi = jnp.stack(pis)
    ca_re = c_re[None] * pr[:, :, None, :] - c_im[None] * pi[:, :, None, :]
    ca_im = c_re[None] * pi[:, :, None, :] + c_im[None] * pr[:, :, None, :]
    kern = jnp.einsum('kgpm,gmq->gkpq', jnp.concatenate([ca_re[:SSM_T], -ca_im[:SSM_T]], axis=-1),
                      jnp.concatenate([bb_re, bb_im], axis=1), precision=hp)
    nj = SSM_NJ
    ks = kern.transpose(0, 3, 1, 2).reshape(nj, 128, SSM_T * SSM_GROUP)
    ro = jnp.stack([ca_re[1:], -ca_im[1:]])
    ro = ro.reshape(2, SSM_T, nj, SSM_GPB, SSM_GROUP, SSM_STATE).transpose(2, 0, 3, 5, 1, 4)
    wo = ro.reshape(nj, 2 * SSM_GPB * SSM_STATE, SSM_T * SSM_GROUP)
    rev_re = jnp.stack([prs[SSM_T - 1 - s] for s in range(SSM_T)])
    rev_im = jnp.stack([pis[SSM_T - 1 - s] for s in range(SSM_T)])
    rev_re, rev_im = rev_re[:, :, None, :], rev_im[:, :, None, :]
    bt_re, bt_im = bb_re.transpose(0, 2, 1)[None], bb_im.transpose(0, 2, 1)[None]
    wi = jnp.concatenate([rev_re * bt_re - rev_im * bt_im, rev_re * bt_im + rev_im * bt_re], axis=-1)
    wi = wi.reshape(SSM_T, nj, 128, 2 * SSM_STATE).transpose(1, 0, 2, 3).reshape(nj, SSM_T * 128, 2 * SSM_STATE)
    a_t = jnp.stack([pr[SSM_T], pi[SSM_T]]).reshape(2, nj, SSM_GPB * SSM_STATE)
    a_t = jnp.moveaxis(a_t, 0, 1)
    return ks, wo, wi, a_t


def _s5_expand_body(ks_ref, wo_ref, wi_ref, wy_ref, win_ref):
    tp = SSM_T * SSM_GROUP
    wide = SSM_T * 128
    half = SSM_GPB * SSM_STATE

    def iota(shape, dim):
        return lax.broadcasted_iota(jnp.int32, shape, dim)

    r = iota((tp, wide), 0)
    c = iota((tp, wide), 1)
    rep = jnp.where((r // SSM_GROUP == c // 128) & (r % SSM_GROUP == c % SSM_GROUP), 1.0, 0.0).astype(BF16)
    col_b = (iota((1, wide), 1) % 128) // SSM_GROUP

    row_a = iota((128, 1), 0) // SSM_GROUP
    bdk = jnp.where(row_a == col_b, _dot(ks_ref[...].astype(BF16), rep), 0.0).astype(BF16)
    for s in range(SSM_T):
        if s:
            wy_ref[s * 128:(s + 1) * 128, :s * 128] = jnp.zeros((128, s * 128), BF16)
        wy_ref[s * 128:(s + 1) * 128, s * 128:] = bdk[:, :wide - s * 128]

    row_a = (iota((2 * half, 1), 0) % half) // SSM_STATE
    ro = _dot(wo_ref[...].astype(BF16), rep)
    wy_ref[wide:, :] = jnp.where(row_a == col_b, ro, 0.0).astype(BF16)

    r = iota((2 * SSM_STATE, 2 * half), 0)
    c = iota((2 * SSM_STATE, 2 * half), 1)
    rep_in = jnp.where((r // SSM_STATE == c // half) & (r % SSM_STATE == c % SSM_STATE), 1.0, 0.0).astype(BF16)
    row_a = (iota((wide, 1), 0) % 128) // SSM_GROUP
    col_b = (iota((1, 2 * half), 1) % half) // SSM_STATE
    win = _dot(wi_ref[...].astype(BF16), rep_in)
    win_ref[...] = jnp.where(row_a == col_b, win, 0.0).astype(BF16)


def _s5_expand(ks, wo, wi):
    nj = ks.shape[0]
    wide = SSM_T * 128
    half2 = 2 * SSM_GPB * SSM_STATE

    def blk(a):
        return pl.BlockSpec((None,) + a.shape[1:], lambda j: (j, 0, 0))

    return pl.pallas_call(
        _s5_expand_body,
        grid=(nj,),
        in_specs=[blk(ks), blk(wo), blk(wi)],
        out_specs=[pl.BlockSpec((None, wide + half2, wide), lambda j: (j, 0, 0)),
                   pl.BlockSpec((None, wide, half2), lambda j: (j, 0, 0))],
        out_shape=[jax.ShapeDtypeStruct((nj, wide + half2, wide), BF16),
                   jax.ShapeDtypeStruct((nj, wide, half2), BF16)],
        compiler_params=_params("parallel"),
        name="s5_expand",
    )(ks, wo, wi)


def _s5_body(u_ref, wy_ref, win_ref, at_ref, y_ref, s_scr, hp_scr, *, nb, nchunk):
    half = SSM_GPB * SSM_STATE
    r = nb * nchunk
    u = jnp.concatenate([u_ref[pl.ds(s, r, stride=SSM_T), :] for s in range(SSM_T)],
                        axis=1).astype(BF16)
    nslab = half // 128
    s_all = _dot(u, win_ref[...])
    for k in range(2 * nslab):
        s_scr[k] = s_all[:, k * 128:(k + 1) * 128]
    a_re = [at_ref[0:1, k * 128:(k + 1) * 128] for k in range(nslab)]
    a_im = [at_ref[1:2, k * 128:(k + 1) * 128] for k in range(nslab)]
    h_re = [jnp.zeros((nb, 128), F32)] * nslab
    h_im = [jnp.zeros((nb, 128), F32)] * nslab
    for c in range(nchunk):
        chunk_rows = pl.ds(c, nb, stride=nchunk)
        for k in range(nslab):
            hp_scr[k, chunk_rows, :] = h_re[k]
            hp_scr[nslab + k, chunk_rows, :] = h_im[k]
            s_re = s_scr[k, chunk_rows, :]
            s_im = s_scr[nslab + k, chunk_rows, :]
            h_re[k], h_im[k] = (a_re[k] * h_re[k] - a_im[k] * h_im[k] + s_re,
                                a_re[k] * h_im[k] + a_im[k] * h_re[k] + s_im)
    hp = jnp.concatenate([hp_scr[k] for k in range(2 * nslab)], axis=1).astype(BF16)
    wide = SSM_T * 128
    cb = 256
    for lo in range(0, wide, cb):
        y = (_dot(u[:, :lo + cb], wy_ref[:lo + cb, lo:lo + cb])
             + _dot(hp, wy_ref[wide:, lo:lo + cb]))
        for t in range(lo // 128, (lo + cb) // 128):
            y_ref[pl.ds(t, r, stride=SSM_T), :] = y[:, t * 128 - lo:(t + 1) * 128 - lo]


def _s5(proj, w_y, w_in, a_t, batch, seq):
    n, width = proj.shape
    nchunk = seq // SSM_T
    nb = min(4, batch)
    r = nb * nchunk
    ucol0 = (width - SSM_WIDTH) // 128
    half2 = 2 * SSM_GPB * SSM_STATE
    once = pl.Buffered(1)
    return pl.pallas_call(
        functools.partial(_s5_body, nb=nb, nchunk=nchunk),
        grid=(SSM_NJ, batch // nb),
        in_specs=[
            pl.BlockSpec((r * SSM_T, 128), lambda j, i: (i, ucol0 + j)),
            pl.BlockSpec((None, SSM_T * 128 + half2, SSM_T * 128), lambda j, i: (j, 0, 0), pipeline_mode=once),
            pl.BlockSpec((None, SSM_T * 128, half2), lambda j, i: (j, 0, 0), pipeline_mode=once),
            pl.BlockSpec((None, 2, half2 // 2), lambda j, i: (j, 0, 0)),
        ],
        out_specs=pl.BlockSpec((r * SSM_T, 128), lambda j, i: (i, j)),
        out_shape=jax.ShapeDtypeStruct((n, SSM_WIDTH), F32),
        scratch_shapes=[pltpu.VMEM((half2 // 128, r, 128), F32), pltpu.VMEM((half2 // 128, r, 128), F32)],
        compiler_params=_params("arbitrary", "arbitrary"),
        name="s5",
    )(proj, w_y, w_in, a_t)


def _gelu(x):
    return 0.5 * x * (1.0 + jnp.tanh(math.sqrt(2.0 / math.pi) * (x + 0.044715 * (x * x * x))))


def _ab_out_body(ya_ref, ys_ref, u_ref, d_ref, wglu_ref, bglu_ref, woa_ref, wob_ref, h_ref, o_ref):
    y = ys_ref[...] + d_ref[...] * u_ref[...]
    z = _gelu(y)
    gate = jax.nn.sigmoid(_dot(z.astype(BF16), wglu_ref[...]) + bglu_ref[...])
    yb = (z * gate).astype(BF16)
    o_ref[...] = h_ref[...] + (_dot(ya_ref[...], woa_ref[...]) + _dot(yb, wob_ref[...]))


def _ab_out(y_a, y_ssm, proj, d_skip, w_glu, b_glu, w_out, h):
    n, d = h.shape
    tm = min(512, n)
    ucol = (proj.shape[1] - SSM_WIDTH) // SSM_WIDTH
    row = lambda i: (i, 0)
    fixed = lambda i: (0, 0)
    return pl.pallas_call(
        _ab_out_body,
        grid=(n // tm,),
        in_specs=[
            pl.BlockSpec((tm, RET_WIDTH), row),
            pl.BlockSpec((tm, SSM_WIDTH), row),
            pl.BlockSpec((tm, SSM_WIDTH), lambda i: (i, ucol)),
            pl.BlockSpec((1, SSM_WIDTH), fixed),
            pl.BlockSpec((SSM_WIDTH, SSM_WIDTH), fixed),
            pl.BlockSpec((1, SSM_WIDTH), fixed),
            pl.BlockSpec((RET_WIDTH, d), fixed),
            pl.BlockSpec((SSM_WIDTH, d), lambda i: (1, 0)),
            pl.BlockSpec((tm, d), row),
        ],
        out_specs=pl.BlockSpec((tm, d), row),
        out_shape=jax.ShapeDtypeStruct((n, d), F32),
        compiler_params=_params("parallel"),
        name="ab_out",
    )(y_a, y_ssm, proj, d_skip.reshape(1, -1), w_glu, b_glu.reshape(1, -1), w_out, w_out, h)


def _att_body(lam_ref, q_ref, k_ref, v_ref, sub_ref, o_ref, *, seq, lambda_init):
    tq = min(ATT_QBLOCK, seq)
    dh = DIFF_HEAD_DIM
    k2 = dh ** -0.5 * math.log2(math.e)
    lam = lam_ref[0]
    neg = jnp.finfo(F32).min
    rc = lax.broadcasted_iota(jnp.int32, (tq, tq), 0) // CHUNK
    cc = lax.broadcasted_iota(jnp.int32, (tq, tq), 1) // CHUNK
    visible = rc >= cc
    for qb in range(seq // tq):
        q0 = qb * tq
        q = q_ref[q0:q0 + tq, :]
        v_diag = v_ref[q0:q0 + tq, :]
        comps = []
        for c in range(2):
            qc = q[:, c * dh:(c + 1) * dh]
            s_diag = _dot_nt(qc, k_ref[q0:q0 + tq, c * dh:(c + 1) * dh])
            s_diag = jnp.where(visible, s_diag, neg)
            m = jnp.max(s_diag, axis=-1, keepdims=True)
            if qb > 0:
                s_off = _dot_nt(qc, k_ref[0:q0, c * dh:(c + 1) * dh])
                m = jnp.maximum(m, jnp.max(s_off, axis=-1, keepdims=True))
            mk = m * k2
            p_diag = jnp.exp2(s_diag * k2 - mk)
            l = jnp.sum(p_diag, axis=-1, keepdims=True)
            acc = _dot(p_diag.astype(BF16), v_diag)
            if qb > 0:
                p_off = jnp.exp2(s_off * k2 - mk)
                l = l + jnp.sum(p_off, axis=-1, keepdims=True)
                acc = acc + _dot(p_off.astype(BF16), v_ref[0:q0, :])
            comps.append(acc / l)
        o = comps[0] - lam * comps[1]
        o = _rms(o, sub_ref[...]) * (1.0 - lambda_init)
        o_ref[q0:q0 + tq, :] = o.astype(BF16)


def _attention(qkv, lam, subln, batch, seq, lambda_init):
    n = qkv.shape[0]
    w = 2 * DIFF_HEAD_DIM

    def col(off):
        return pl.BlockSpec((seq, w), lambda b, h: (b, off + h))

    return pl.pallas_call(
        functools.partial(_att_body, seq=seq, lambda_init=lambda_init),
        grid=(batch, DIFF_HEADS),
        in_specs=[
            pl.BlockSpec(memory_space=pltpu.SMEM),
            col(0), col(DIFF_HEADS), col(2 * DIFF_HEADS),
            pl.BlockSpec((1, w), lambda b, h: (0, 0)),
        ],
        out_specs=pl.BlockSpec((seq, w), lambda b, h: (b, h)),
        out_shape=jax.ShapeDtypeStruct((n, D_MODEL), BF16),
        compiler_params=_params("parallel", "parallel"),
        name="diff_attention",
    )(lam.reshape(1), qkv, qkv, qkv, subln.reshape(1, w))


def _out_proj_body(o_ref, w_ref, h_ref, y_ref):
    y_ref[...] = h_ref[...] + _dot(o_ref[...], w_ref[...])


def _out_proj(o, w, h):
    n, d = h.shape
    tm = min(512, n)
    return pl.pallas_call(
        _out_proj_body,
        grid=(n // tm,),
        in_specs=[
            pl.BlockSpec((tm, d), lambda i: (i, 0)),
            pl.BlockSpec((d, d), lambda i: (0, 0)),
            pl.BlockSpec((tm, d), lambda i: (i, 0)),
        ],
        out_specs=pl.BlockSpec((tm, d), lambda i: (i, 0)),
        out_shape=jax.ShapeDtypeStruct((n, d), F32),
        compiler_params=_params("parallel"),
        name="out_proj",
    )(o, w, h)


def _rope_tables(seq, rot_dim, theta):
    inv = 1.0 / (theta ** (np.arange(0, rot_dim, 2, dtype=np.float64) / rot_dim))
    ang = np.arange(seq, dtype=np.float64)[:, None] * inv[None, :]
    return np.cos(ang), np.sin(ang)


def kernel(x, ffn_norm, ffn_w_gate, ffn_w_up, ffn_w_down, mix_norm, ab_w_in, ab_w_out, ssm_lambda_re, ssm_lambda_im, ssm_log_step, ssm_b_re, ssm_b_im, ssm_c_re, ssm_c_im, ssm_d, ssm_w_glu, ssm_b_glu, c_w_qkv, c_w_out, c_lambda_q1, c_lambda_k1, c_lambda_q2, c_lambda_k2, c_subln, final_norm):
    batch, seq, d = x.shape
    n = batch * seq
    h = x.reshape(n, d)
    wg = _cast_stack(ffn_w_gate)
    wu = _cast_stack(ffn_w_up)
    wd = _cast_stack(ffn_w_down)

    def ffn(h, layer, half, final=False):
        return _ffn(h, ffn_norm[layer, half], wg, wu, wd, final_norm, layer, half, final=final)

    h = ffn(h, 0, 0)
    ret_cos, ret_sin = _rope_tables(seq, RET_HEAD_DIM, RET_THETA)
    k_scale = RET_HEAD_DIM ** -0.5
    cos_tab = np.stack([ret_cos, ret_cos * k_scale, np.ones_like(ret_cos)]).astype(np.float32)
    sin_tab = np.stack([ret_sin, ret_sin * k_scale, np.zeros_like(ret_sin)]).astype(np.float32)
    proj = _proj_ab(h, mix_norm[0], ab_w_in[0].astype(BF16), cos_tab, sin_tab, seq)
    y_a = _retention(proj, batch, seq)
    ks, wo, wi, a_t = _s5_weights(ssm_lambda_re[0], ssm_lambda_im[0], ssm_log_step[0], ssm_b_re[0],
                                  ssm_b_im[0], ssm_c_re[0], ssm_c_im[0])
    w_y, w_in = _s5_expand(ks, wo, wi)
    y_ssm = _s5(proj, w_y, w_in, a_t, batch, seq)
    h = _ab_out(y_a, y_ssm, proj, ssm_d[0], ssm_w_glu[0].astype(BF16), ssm_b_glu[0],
                ab_w_out[0].astype(BF16), h)
    h = ffn(h, 0, 1)

    h = ffn(h, 1, 0)
    att_cos, att_sin = _rope_tables(seq, 2 * ROT_HALF, ROPE_THETA)
    rest = np.zeros((seq, DIFF_HEAD_DIM // 2 - ROT_HALF))
    ident = np.zeros((seq, DIFF_HEAD_DIM))
    cf = np.stack([np.concatenate([att_cos, rest + 1.0, att_cos, rest + 1.0], axis=1), ident + 1.0])
    sg = np.stack([np.concatenate([-att_sin, rest, att_sin, rest], axis=1), ident])
    cf, sg = cf.astype(np.float32), sg.astype(np.float32)
    qkv = _proj_c(h, mix_norm[1], _cast_qkv(c_w_qkv[0]), cf, sg, seq)
    lambda_init = 0.8 - 0.6 * math.exp(-0.3 * 1)
    lam = (jnp.exp(jnp.sum(c_lambda_q1[0] * c_lambda_k1[0]))
           - jnp.exp(jnp.sum(c_lambda_q2[0] * c_lambda_k2[0])) + lambda_init)
    o = _attention(qkv, lam, c_subln[0], batch, seq, lambda_init)
    h = _out_proj(o, c_w_out[0].astype(BF16), h)
    h = ffn(h, 1, 1, final=True)
    return h.reshape(batch, seq, d)
```

```python
import functools
import math

import jax
import jax.numpy as jnp
import numpy as np
from jax import lax
from jax.experimental import pallas as pl
from jax.experimental.pallas import tpu as pltpu

F32 = jnp.float32
BF16 = jnp.bfloat16

D_MODEL = 2048
CHUNK = 64
NORM_EPS = 1e-6

RET_WIDTH = 1024
RET_HEADS = 4
RET_HEAD_DIM = 256
RET_THETA = 10000.0
RET_BLOCK = 256

SSM_WIDTH = 1024
SSM_GROUP = 16
SSM_GROUPS = 64
SSM_STATE = 64
SSM_T = 16
SSM_GPB = 128 // SSM_GROUP
SSM_NJ = SSM_WIDTH // 128

DIFF_HEAD_DIM = 128
DIFF_HEADS = 8
ROPE_THETA = 500000.0
ROT_HALF = DIFF_HEAD_DIM // 4 // 2
ATT_QBLOCK = 512

FF_TILE = 1024
CAST_ROWS = 512
PROJ_TN = 1024
VMEM_LIMIT = 56 * 1024 * 1024


def _params(*sem):
    return pltpu.CompilerParams(dimension_semantics=sem, vmem_limit_bytes=VMEM_LIMIT)


def _rms(x, g):
    y = x * lax.rsqrt(jnp.mean(x * x, axis=-1, keepdims=True) + NORM_EPS)
    return y * g


def _dot(a, b):
    return jnp.dot(a, b, preferred_element_type=F32)


def _dot_nt(a, b):
    return lax.dot_general(a, b, (((1,), (1,)), ((), ())), preferred_element_type=F32)


def _dot_tn(a, b):
    return lax.dot_general(a, b, (((0,), (0,)), ((), ())), preferred_element_type=F32)


def _cast_body(w_ref, o_ref):
    o_ref[...] = w_ref[...].astype(BF16)


def _cast_stack(w):
    a, b, r, c = w.shape
    spec = pl.BlockSpec((None, None, CAST_ROWS, c), lambda s, i: (s // b, s % b, i, 0))
    return pl.pallas_call(
        _cast_body,
        grid=(a * b, pl.cdiv(r, CAST_ROWS)),
        in_specs=[spec],
        out_specs=spec,
        out_shape=jax.ShapeDtypeStruct(w.shape, BF16),
        compiler_params=_params("parallel", "parallel"),
        name="cast_bf16",
    )(w)


def _ffn_body(x_ref, g_ref, wg_ref, wu_ref, wd_ref, fn_ref, o_ref, xn_ref, acc_ref, *, final, tail):
    j = pl.program_id(1)
    last = pl.num_programs(1) - 1

    def tile(width):
        xn = xn_ref[...]
        gate = _dot(xn, wg_ref[:, :width])
        up = _dot(xn, wu_ref[:, :width])
        act = (gate * jax.nn.sigmoid(gate) * up).astype(BF16)
        return _dot(act, wd_ref[:width, :])

    @pl.when(j == 0)
    def _():
        xn_ref[...] = _rms(x_ref[...], g_ref[...]).astype(BF16)
        acc_ref[...] = tile(tail)

    @pl.when((j > 0) & (j < last))
    def _():
        acc_ref[...] += tile(FF_TILE)

    @pl.when(j == last)
    def _():
        h = x_ref[...] + 0.5 * (acc_ref[...] + tile(FF_TILE))
        if final:
            h = _rms(h, fn_ref[...])
        o_ref[...] = h


def _ffn(h, norm_g, wg, wu, wd, final_g, layer, half, *, final):
    n, d = h.shape
    tm = min(512, n)
    ff = wg.shape[-1]
    steps = pl.cdiv(ff, FF_TILE)
    tail = ff - (steps - 1) * FF_TILE
    return pl.pallas_call(
        functools.partial(_ffn_body, final=final, tail=tail),
        grid=(n // tm, steps),
        in_specs=[
            pl.BlockSpec((tm, d), lambda i, j: (i, 0)),
            pl.BlockSpec((1, d), lambda i, j: (0, 0)),
            pl.BlockSpec((None, None, d, FF_TILE), lambda i, j: (layer, half, 0, (j + steps - 1) % steps)),
            pl.BlockSpec((None, None, d, FF_TILE), lambda i, j: (layer, half, 0, (j + steps - 1) % steps)),
            pl.BlockSpec((None, None, FF_TILE, d), lambda i, j: (layer, half, (j + steps - 1) % steps, 0)),
            pl.BlockSpec((1, d), lambda i, j: (0, 0)),
        ],
        out_specs=pl.BlockSpec((tm, d), lambda i, j: (i, 0)),
        out_shape=jax.ShapeDtypeStruct((n, d), F32),
        scratch_shapes=[pltpu.VMEM((tm, d), BF16), pltpu.VMEM((tm, d), F32)],
        compiler_params=_params("parallel", "arbitrary"),
        name="ffn",
    )(h, norm_g.reshape(1, d), wg, wu, wd, final_g.reshape(1, d))


def _proj_ab_body(x_ref, g_ref, w_ref, cos_ref, sin_ref, o_ref, xn_ref):
    @pl.when(pl.program_id(1) == 0)
    def _():
        xn_ref[...] = _rms(x_ref[...], g_ref[...]).astype(BF16)

    y = _dot(xn_ref[...], w_ref[...])
    c = cos_ref[...]
    s = sin_ref[...]
    half = RET_HEAD_DIM // 2
    parts = []
    for hd in range(PROJ_TN // RET_HEAD_DIM):
        lo = hd * RET_HEAD_DIM
        x1 = y[:, lo:lo + half]
        x2 = y[:, lo + half:lo + 2 * half]
        parts += [x1 * c - x2 * s, x2 * c + x1 * s]
    o_ref[...] = jnp.concatenate(parts, axis=1)


def _proj_ab(h, norm_g, w, cos_tab, sin_tab, seq):
    n, d = h.shape
    nout = w.shape[1]
    tm = min(1024, seq)
    per_seq = seq // tm
    tab = pl.BlockSpec((None, tm, RET_HEAD_DIM // 2), lambda i, j: (jnp.minimum(j, 2), i % per_seq, 0))
    return pl.pallas_call(
        _proj_ab_body,
        grid=(n // tm, nout // PROJ_TN),
        in_specs=[
            pl.BlockSpec((tm, d), lambda i, j: (i, 0)),
            pl.BlockSpec((1, d), lambda i, j: (0, 0)),
            pl.BlockSpec((d, PROJ_TN), lambda i, j: (0, j)),
            tab, tab,
        ],
        out_specs=pl.BlockSpec((tm, PROJ_TN), lambda i, j: (i, j)),
        out_shape=jax.ShapeDtypeStruct((n, nout), F32),
        scratch_shapes=[pltpu.VMEM((tm, d), BF16)],
        compiler_params=_params("parallel", "arbitrary"),
        name="proj_ab",
    )(h, norm_g.reshape(1, d), w, cos_tab, sin_tab)


def _cast_qkv_body(w_ref, o_ref):
    dh = DIFF_HEAD_DIM
    old = lax.broadcasted_iota(jnp.int32, (dh, dh), 0)
    new = lax.broadcasted_iota(jnp.int32, (dh, dh), 1)
    src = jnp.where(new < ROT_HALF, new,
                    jnp.where(new < dh // 2, new + ROT_HALF,
                              jnp.where(new < dh // 2 + ROT_HALF, new - (dh // 2 - ROT_HALF), new)))
    src = jnp.where(pl.program_id(1) < 2 * D_MODEL // PROJ_TN, src, new)
    perm = jnp.where(old == src, 1.0, 0.0).astype(BF16)
    w = w_ref[...].astype(BF16)
    parts = [_dot(w[:, hd * dh:(hd + 1) * dh], perm) for hd in range(PROJ_TN // dh)]
    o_ref[...] = jnp.concatenate(parts, axis=1).astype(BF16)


def _cast_qkv(w):
    r, c = w.shape
    spec = pl.BlockSpec((CAST_ROWS, PROJ_TN), lambda i, j: (i, j))
    return pl.pallas_call(
        _cast_qkv_body,
        grid=(r // CAST_ROWS, c // PROJ_TN),
        in_specs=[spec],
        out_specs=spec,
        out_shape=jax.ShapeDtypeStruct(w.shape, BF16),
        compiler_params=_params("parallel", "parallel"),
        name="cast_qkv",
    )(w)


def _proj_c_body(x_ref, g_ref, w_ref, cf_ref, sg_ref, o_ref, xn_ref):
    @pl.when(pl.program_id(1) == 0)
    def _():
        xn_ref[...] = _rms(x_ref[...], g_ref[...]).astype(BF16)

    y = _dot(xn_ref[...], w_ref[...])
    cf = cf_ref[...]
    sg = sg_ref[...]
    parts = []
    for hd in range(PROJ_TN // DIFF_HEAD_DIM):
        lo = hd * DIFF_HEAD_DIM
        yh = y[:, lo:lo + DIFF_HEAD_DIM]
        r = yh * cf + pltpu.roll(yh, DIFF_HEAD_DIM // 2, axis=1) * sg
        parts.append(r.astype(BF16))
    o_ref[...] = jnp.concatenate(parts, axis=1)


def _proj_c(h, norm_g, w, cf, sg, seq):
    n, d = h.shape
    nout = w.shape[1]
    tm = min(1024, seq)
    per_seq = seq // tm
    qk_tiles = 2 * D_MODEL // PROJ_TN
    tab = pl.BlockSpec((None, tm, DIFF_HEAD_DIM), lambda i, j: (j // qk_tiles, i % per_seq, 0))
    return pl.pallas_call(
        _proj_c_body,
        grid=(n // tm, nout // PROJ_TN),
        in_specs=[
            pl.BlockSpec((tm, d), lambda i, j: (i, 0)),
            pl.BlockSpec((1, d), lambda i, j: (0, 0)),
            pl.BlockSpec((d, PROJ_TN), lambda i, j: (0, j)),
            tab, tab,
        ],
        out_specs=pl.BlockSpec((tm, PROJ_TN), lambda i, j: (i, j)),
        out_shape=jax.ShapeDtypeStruct((n, nout), BF16),
        scratch_shapes=[pltpu.VMEM((tm, d), BF16)],
        compiler_params=_params("parallel", "arbitrary"),
        name="proj_c",
    )(h, norm_g.reshape(1, d), w, cf, sg)


def _ret_body(q_ref, k_ref, v_ref, g_ref, dm_ref, qd_ref, kd_ref, cd_ref, o_ref, *, nblk):
    t = RET_BLOCK
    dmask = dm_ref[...]
    qdec = qd_ref[...]
    kdec = kd_ref[...]
    cdec = cd_ref[...]

    def rows(ref, n):
        return ref[pl.ds(pl.multiple_of(n * t, t), t), :]

    def step(n, state):
        q = rows(q_ref, n)
        k = rows(k_ref, n)
        vb = rows(v_ref, n).astype(BF16)
        scores = _dot_nt(q.astype(BF16), k.astype(BF16)) * dmask
        intra = _dot(scores.astype(BF16), vb)
        inter = _dot((q * qdec).astype(BF16), state.astype(BF16))
        new_state = state * cdec + _dot_tn((k * kdec).astype(BF16), vb)
        o = intra + inter
        o = o * lax.rsqrt(jnp.mean(o * o, axis=-1, keepdims=True) + NORM_EPS)
        gate = rows(g_ref, n)
        r0 = pl.multiple_of(n * t, t)
        o_ref[pl.ds(r0, t), :] = (o * (gate * jax.nn.sigmoid(gate))).astype(BF16)
        return new_state

    lax.fori_loop(0, nblk, step, jnp.zeros((RET_HEAD_DIM, RET_HEAD_DIM), F32), unroll=True)


def _retention(proj, batch, seq):
    n = proj.shape[0]
    t = RET_BLOCK
    hd = RET_HEAD_DIM
    log_g = np.log(1.0 - 2.0 ** (-5.0 - np.arange(RET_HEADS, dtype=np.float64)))
    idx = np.arange(t, dtype=np.float64)
    chunk = np.arange(t) // CHUNK
    visible = chunk[:, None] >= chunk[None, :]
    dist = np.abs(idx[:, None] - idx[None, :])
    dmask = np.where(visible[None], np.exp(log_g[:, None, None] * dist[None]), 0.0)
    f32 = np.float32
    dmask = dmask.astype(f32)
    qdec = np.broadcast_to(np.exp(log_g[:, None] * (idx[None] + 1.0))[:, :, None], (RET_HEADS, t, hd)).astype(f32)
    kdec = np.broadcast_to(np.exp(log_g[:, None] * (t - 1.0 - idx[None]))[:, :, None], (RET_HEADS, t, hd)).astype(f32)
    cdec = np.broadcast_to(np.exp(log_g * t)[:, None, None], (RET_HEADS, 1, hd)).astype(f32)

    def col(off):
        return pl.BlockSpec((seq, hd), lambda b, h: (b, off + h))

    def tab(rows, cols):
        return pl.BlockSpec((None, rows, cols), lambda b, h: (h, 0, 0))

    return pl.pallas_call(
        functools.partial(_ret_body, nblk=seq // t),
        grid=(batch, RET_HEADS),
        in_specs=[col(0), col(RET_HEADS), col(2 * RET_HEADS), col(3 * RET_HEADS),
                  tab(t, t), tab(t, hd), tab(t, hd), tab(1, hd)],
        out_specs=pl.BlockSpec((seq, hd), lambda b, h: (b, h)),
        out_shape=jax.ShapeDtypeStruct((n, RET_WIDTH), BF16),
        compiler_params=_params("parallel", "parallel"),
        name="retention",
    )(proj, proj, proj, proj, dmask, qdec, kdec, cdec)


def _s5_weights(lam_re, lam_im, log_step, b_re, b_im, c_re, c_im):
    hp = lax.Precision.HIGHEST
    step = jnp.exp(log_step)[:, None]
    mag = jnp.exp(lam_re * step)
    a_re = mag * jnp.cos(lam_im * step)
    a_im = mag * jnp.sin(lam_im * step)
    den = lam_re * lam_re + lam_im * lam_im
    nr = a_re - 1.0
    f_re = (nr * lam_re + a_im * lam_im) / den
    f_im = (a_im * lam_re - nr * lam_im) / den
    bb_re = f_re[..., None] * b_re - f_im[..., None] * b_im
    bb_im = f_re[..., None] * b_im + f_im[..., None] * b_re
    prs, pis = [jnp.ones_like(a_re)], [jnp.zeros_like(a_im)]
    for _ in range(SSM_T):
        prs.append(prs[-1] * a_re - pis[-1] * a_im)
        pis.append(prs[-2] * a_im + pis[-1] * a_re)
    pr = jnp.stack(prs)
    pi = jnp.stack(pis)
    ca_re = c_re[None] * pr[:, :, None, :] - c_im[None] * pi[:, :, None, :]
    ca_im = c_re[None] * pi[:, :, None, :] + c_im[None] * pr[:, :, None, :]
    kern = jnp.einsum('kgpm,gmq->gkpq', jnp.concatenate([ca_re[:SSM_T], -ca_im[:SSM_T]], axis=-1),
                      jnp.concatenate([bb_re, bb_im], axis=1), precision=hp)
    nj = SSM_NJ
    ks = kern.transpose(0, 3, 1, 2).reshape(nj, 128, SSM_T * SSM_GROUP)
    ro = jnp.stack([ca_re[1:], -ca_im[1:]])
    ro = ro.reshape(2, SSM_T, nj, SSM_GPB, SSM_GROUP, SSM_STATE).transpose(2, 0, 3, 5, 1, 4)
    wo = ro.reshape(nj, 2 * SSM_GPB * SSM_STATE, SSM_T * SSM_GROUP)
    rev_re = jnp.stack([prs[SSM_T - 1 - s] for s in range(SSM_T)])
    rev_im = jnp.stack([pis[SSM_T - 1 - s] for s in range(SSM_T)])
    rev_re, rev_im = rev_re[:, :, None, :], rev_im[:, :, None, :]
    bt_re, bt_im = bb_re.transpose(0, 2, 1)[None], bb_im.transpose(0, 2, 1)[None]
    wi = jnp.concatenate([rev_re * bt_re - rev_im * bt_im, rev_re * bt_im + rev_im * bt_re], axis=-1)
    wi = wi.reshape(SSM_T, nj, 128, 2 * SSM_STATE).transpose(1, 0, 2, 3).reshape(nj, SSM_T * 128, 2 * SSM_STATE)
    a_t = jnp.stack([pr[SSM_T], pi[SSM_T]]).reshape(2, nj, SSM_GPB * SSM_STATE)
    a_t = jnp.moveaxis(a_t, 0, 1)
    return ks, wo, wi, a_t


def _s5_expand_body(ks_ref, wo_ref, wi_ref, wy_ref, win_ref):
    tp = SSM_T * SSM_GROUP
    wide = SSM_T * 128
    half = SSM_GPB * SSM_STATE

    def iota(shape, dim):
        return lax.broadcasted_iota(jnp.int32, shape, dim)

    r = iota((tp, wide), 0)
    c = iota((tp, wide), 1)
    rep = jnp.where((r // SSM_GROUP == c // 128) & (r % SSM_GROUP == c % SSM_GROUP), 1.0, 0.0).astype(BF16)
    col_b = (iota((1, wide), 1) % 128) // SSM_GROUP

    row_a = iota((128, 1), 0) // SSM_GROUP
    bdk = jnp.where(row_a == col_b, _dot(ks_ref[...].astype(BF16), rep), 0.0).astype(BF16)
    for s in range(SSM_T):
        if s:
            wy_ref[s * 128:(s + 1) * 128, :s * 128] = jnp.zeros((128, s * 128), BF16)
        wy_ref[s * 128:(s + 1) * 128, s * 128:] = bdk[:, :wide - s * 128]

    row_a = (iota((2 * half, 1), 0) % half) // SSM_STATE
    ro = _dot(wo_ref[...].astype(BF16), rep)
    wy_ref[wide:, :] = jnp.where(row_a == col_b, ro, 0.0).astype(BF16)

    r = iota((2 * SSM_STATE, 2 * half), 0)
    c = iota((2 * SSM_STATE, 2 * half), 1)
    rep_in = jnp.where((r // SSM_STATE == c // half) & (r % SSM_STATE == c % SSM_STATE), 1.0, 0.0).astype(BF16)
    row_a = (iota((wide, 1), 0) % 128) // SSM_GROUP
    col_b = (iota((1, 2 * half), 1) % half) // SSM_STATE
    win = _dot(wi_ref[...].astype(BF16), rep_in)
    win_ref[...] = jnp.where(row_a == col_b, win, 0.0).astype(BF16)


def _s5_expand(ks, wo, wi):
    nj = ks.shape[0]
    wide = SSM_T * 128
    half2 = 2 * SSM_GPB * SSM_STATE

    def blk(a):
        return pl.BlockSpec((None,) + a.shape[1:], lambda j: (j, 0, 0))

    return pl.pallas_call(
        _s5_expand_body,
        grid=(nj,),
        in_specs=[blk(ks), blk(wo), blk(wi)],
        out_specs=[pl.BlockSpec((None, wide + half2, wide), lambda j: (j, 0, 0)),
                   pl.BlockSpec((None, wide, half2), lambda j: (j, 0, 0))],
        out_shape=[jax.ShapeDtypeStruct((nj, wide + half2, wide), BF16),
                   jax.ShapeDtypeStruct((nj, wide, half2), BF16)],
        compiler_params=_params("parallel"),
        name="s5_expand",
    )(ks, wo, wi)


def _s5_body(u_ref, wy_ref, win_ref, at_ref, y_ref, s_scr, hp_scr, *, nb, nchunk):
    half = SSM_GPB * SSM_STATE
    r = nb * nchunk
    u = jnp.concatenate([u_ref[pl.ds(s, r, stride=SSM_T), :] for s in range(SSM_T)],
                        axis=1).astype(BF16)
    nslab = half // 128
    s_all = _dot(u, win_ref[...])
    for k in range(nslab):
        s_scr[k, 0:r, :] = s_all[:, k * 128:(k + 1) * 128]
        s_scr[k, r:2 * r, :] = s_all[:, half + k * 128:half + (k + 1) * 128]
    is_re = lax.broadcasted_iota(jnp.int32, (2 * nb, 128), 0) < nb
    coef_same = [jnp.broadcast_to(at_ref[0:1, k * 128:(k + 1) * 128], (2 * nb, 128)) for k in range(nslab)]
    coef_swap = [jnp.where(is_re, -at_ref[1:2, k * 128:(k + 1) * 128], at_ref[1:2, k * 128:(k + 1) * 128])
                 for k in range(nslab)]
    h = [jnp.zeros((2 * nb, 128), F32)] * nslab
    for c in range(nchunk):
        chunk_rows = pl.ds(c, 2 * nb, stride=nchunk)
        for k in range(nslab):
            hp_scr[k, chunk_rows, :] = h[k]
            h[k] = (coef_same[k] * h[k] + coef_swap[k] * pltpu.roll(h[k], nb, axis=0)
                    + s_scr[k, chunk_rows, :])
    hp = jnp.concatenate([hp_scr[k, 0:r, :] for k in range(nslab)]
                         + [hp_scr[k, r:2 * r, :] for k in range(nslab)], axis=1).astype(BF16)
    wide = SSM_T * 128
    cb = 256
    for lo in range(0, wide, cb):
        y = (_dot(u[:, :lo + cb], wy_ref[:lo + cb, lo:lo + cb])
             + _dot(hp, wy_ref[wide:, lo:lo + cb]))
        for t in range(lo // 128, (lo + cb) // 128):
            y_ref[pl.ds(t, r, stride=SSM_T), :] = y[:, t * 128 - lo:(t + 1) * 128 - lo]


def _s5(proj, w_y, w_in, a_t, batch, seq):
    n, width = proj.shape
    nchunk = seq // SSM_T
    nb = min(4, batch)
    r = nb * nchunk
    ucol0 = (width - SSM_WIDTH) // 128
    half2 = 2 * SSM_GPB * SSM_STATE
    once = pl.Buffered(1)
    return pl.pallas_call(
        functools.partial(_s5_body, nb=nb, nchunk=nchunk),
        grid=(SSM_NJ, batch // nb),
        in_specs=[
            pl.BlockSpec((r * SSM_T, 128), lambda j, i: (i, ucol0 + j)),
            pl.BlockSpec((None, SSM_T * 128 + half2, SSM_T * 128), lambda j, i: (j, 0, 0), pipeline_mode=once),
            pl.BlockSpec((None, SSM_T * 128, half2), lambda j, i: (j, 0, 0), pipeline_mode=once),
            pl.BlockSpec((None, 2, half2 // 2), lambda j, i: (j, 0, 0)),
        ],
        out_specs=pl.BlockSpec((r * SSM_T, 128), lambda j, i: (i, j)),
        out_shape=jax.ShapeDtypeStruct((n, SSM_WIDTH), F32),
        scratch_shapes=[pltpu.VMEM((half2 // 256, 2 * r, 128), F32), pltpu.VMEM((half2 // 256, 2 * r, 128), F32)],
        compiler_params=_params("arbitrary", "arbitrary"),
        name="s5",
    )(proj, w_y, w_in, a_t)


def _gelu(x):
    return 0.5 * x * (1.0 + jnp.tanh(math.sqrt(2.0 / math.pi) * (x + 0.044715 * (x * x * x))))


def _ab_out_body(ya_ref, ys_ref, u_ref, d_ref, wglu_ref, bglu_ref, woa_ref, wob_ref, h_ref, o_ref):
    y = ys_ref[...] + d_ref[...] * u_ref[...]
    z = _gelu(y)
    gate = jax.nn.sigmoid(_dot(z.astype(BF16), wglu_ref[...]) + bglu_ref[...])
    yb = (z * gate).astype(BF16)
    o_ref[...] = h_ref[...] + (_dot(ya_ref[...], woa_ref[...]) + _dot(yb, wob_ref[...]))


def _ab_out(y_a, y_ssm, proj, d_skip, w_glu, b_glu, w_out, h):
    n, d = h.shape
    tm = min(512, n)
    ucol = (proj.shape[1] - SSM_WIDTH) // SSM_WIDTH
    row = lambda i: (i, 0)
    fixed = lambda i: (0, 0)
    return pl.pallas_call(
        _ab_out_body,
        grid=(n // tm,),
        in_specs=[
            pl.BlockSpec((tm, RET_WIDTH), row),
            pl.BlockSpec((tm, SSM_WIDTH), row),
            pl.BlockSpec((tm, SSM_WIDTH), lambda i: (i, ucol)),
            pl.BlockSpec((1, SSM_WIDTH), fixed),
            pl.BlockSpec((SSM_WIDTH, SSM_WIDTH), fixed),
            pl.BlockSpec((1, SSM_WIDTH), fixed),
            pl.BlockSpec((RET_WIDTH, d), fixed),
            pl.BlockSpec((SSM_WIDTH, d), lambda i: (1, 0)),
            pl.BlockSpec((tm, d), row),
        ],
        out_specs=pl.BlockSpec((tm, d), row),
        out_shape=jax.ShapeDtypeStruct((n, d), F32),
        compiler_params=_params("parallel"),
        name="ab_out",
    )(y_a, y_ssm, proj, d_skip.reshape(1, -1), w_glu, b_glu.reshape(1, -1), w_out, w_out, h)


def _att_body(lam_ref, q_ref, k_ref, v_ref, sub_ref, o_ref, *, seq, lambda_init):
    tq = min(ATT_QBLOCK, seq)
    dh = DIFF_HEAD_DIM
    k2 = dh ** -0.5 * math.log2(math.e)
    lam = lam_ref[0]
    neg = jnp.finfo(F32).min
    rc = lax.broadcasted_iota(jnp.int32, (tq, tq), 0) // CHUNK
    cc = lax.broadcasted_iota(jnp.int32, (tq, tq), 1) // CHUNK
    visible = rc >= cc
    for qb in range(seq // tq):
        q0 = qb * tq
        q = q_ref[q0:q0 + tq, :]
        v_diag = v_ref[q0:q0 + tq, :]
        comps = []
        for c in range(2):
            qc = q[:, c * dh:(c + 1) * dh]
            s_diag = _dot_nt(qc, k_ref[q0:q0 + tq, c * dh:(c + 1) * dh])
            s_diag = jnp.where(visible, s_diag, neg)
            m = jnp.max(s_diag, axis=-1, keepdims=True)
            if qb > 0:
                s_off = _dot_nt(qc, k_ref[0:q0, c * dh:(c + 1) * dh])
                m = jnp.maximum(m, jnp.max(s_off, axis=-1, keepdims=True))
            mk = m * k2
            p_diag = jnp.exp2(s_diag * k2 - mk)
            l = jnp.sum(p_diag, axis=-1, keepdims=True)
            acc = _dot(p_diag.astype(BF16), v_diag)
            if qb > 0:
                p_off = jnp.exp2(s_off * k2 - mk)
                l = l + jnp.sum(p_off, axis=-1, keepdims=True)
                acc = acc + _dot(p_off.astype(BF16), v_ref[0:q0, :])
            comps.append(acc / l)
        o = comps[0] - lam * comps[1]
        o = _rms(o, sub_ref[...]) * (1.0 - lambda_init)
        o_ref[q0:q0 + tq, :] = o.astype(BF16)


def _attention(qkv, lam, subln, batch, seq, lambda_init):
    n = qkv.shape[0]
    w = 2 * DIFF_HEAD_DIM

    def col(off):
        return pl.BlockSpec((seq, w), lambda b, h: (b, off + h))

    return pl.pallas_call(
        functools.partial(_att_body, seq=seq, lambda_init=lambda_init),
        grid=(batch, DIFF_HEADS),
        in_specs=[
            pl.BlockSpec(memory_space=pltpu.SMEM),
            col(0), col(DIFF_HEADS), col(2 * DIFF_HEADS),
            pl.BlockSpec((1, w), lambda b, h: (0, 0)),
        ],
        out_specs=pl.BlockSpec((seq, w), lambda b, h: (b, h)),
        out_shape=jax.ShapeDtypeStruct((n, D_MODEL), BF16),
        compiler_params=_params("parallel", "parallel"),
        name="diff_attention",
    )(lam.reshape(1), qkv, qkv, qkv, subln.reshape(1, w))


def _out_proj_body(o_ref, w_ref, h_ref, y_ref):
    y_ref[...] = h_ref[...] + _dot(o_ref[...], w_ref[...])


def _out_proj(o, w, h):
    n, d = h.shape
    tm = min(512, n)
    return pl.pallas_call(
        _out_proj_body,
        grid=(n // tm,),
        in_specs=[
            pl.BlockSpec((tm, d), lambda i: (i, 0)),
            pl.BlockSpec((d, d), lambda i: (0, 0)),
            pl.BlockSpec((tm, d), lambda i: (i, 0)),
        ],
        out_specs=pl.BlockSpec((tm, d), lambda i: (i, 0)),
        out_shape=jax.ShapeDtypeStruct((n, d), F32),
        compiler_params=_params("parallel"),
        name="out_proj",
    )(o, w, h)


def _rope_tables(seq, rot_dim, theta):
    inv = 1.0 / (theta ** (np.arange(0, rot_dim, 2, dtype=np.float64) / rot_dim))
    ang = np.arange(seq, dtype=np.float64)[:, None] * inv[None, :]
    return np.cos(ang), np.sin(ang)


def kernel(x, ffn_norm, ffn_w_gate, ffn_w_up, ffn_w_down, mix_norm, ab_w_in, ab_w_out, ssm_lambda_re, ssm_lambda_im, ssm_log_step, ssm_b_re, ssm_b_im, ssm_c_re, ssm_c_im, ssm_d, ssm_w_glu, ssm_b_glu, c_w_qkv, c_w_out, c_lambda_q1, c_lambda_k1, c_lambda_q2, c_lambda_k2, c_subln, final_norm):
    batch, seq, d = x.shape
    n = batch * seq
    h = x.reshape(n, d)
    wg = _cast_stack(ffn_w_gate)
    wu = _cast_stack(ffn_w_up)
    wd = _cast_stack(ffn_w_down)

    def ffn(h, layer, half, final=False):
        return _ffn(h, ffn_norm[layer, half], wg, wu, wd, final_norm, layer, half, final=final)

    h = ffn(h, 0, 0)
    ret_cos, ret_sin = _rope_tables(seq, RET_HEAD_DIM, RET_THETA)
    k_scale = RET_HEAD_DIM ** -0.5
    cos_tab = np.stack([ret_cos, ret_cos * k_scale, np.ones_like(ret_cos)]).astype(np.float32)
    sin_tab = np.stack([ret_sin, ret_sin * k_scale, np.zeros_like(ret_sin)]).astype(np.float32)
    proj = _proj_ab(h, mix_norm[0], ab_w_in[0].astype(BF16), cos_tab, sin_tab, seq)
    y_a = _retention(proj, batch, seq)
    ks, wo, wi, a_t = _s5_weights(ssm_lambda_re[0], ssm_lambda_im[0], ssm_log_step[0], ssm_b_re[0],
                                  ssm_b_im[0], ssm_c_re[0], ssm_c_im[0])
    w_y, w_in = _s5_expand(ks, wo, wi)
    y_ssm = _s5(proj, w_y, w_in, a_t, batch, seq)
    h = _ab_out(y_a, y_ssm, proj, ssm_d[0], ssm_w_glu[0].astype(BF16), ssm_b_glu[0],
                ab_w_out[0].astype(BF16), h)
    h = ffn(h, 0, 1)

    h = ffn(h, 1, 0)
    att_cos, att_sin = _rope_tables(seq, 2 * ROT_HALF, ROPE_THETA)
    rest = np.zeros((seq, DIFF_HEAD_DIM // 2 - ROT_HALF))
    ident = np.zeros((seq, DIFF_HEAD_DIM))
    cf = np.stack([np.concatenate([att_cos, rest + 1.0, att_cos, rest + 1.0], axis=1), ident + 1.0])
    sg = np.stack([np.concatenate([-att_sin, rest, att_sin, rest], axis=1), ident])
    cf, sg = cf.astype(np.float32), sg.astype(np.float32)
    qkv = _proj_c(h, mix_norm[1], _cast_qkv(c_w_qkv[0]), cf, sg, seq)
    lambda_init = 0.8 - 0.6 * math.exp(-0.3 * 1)
    lam = (jnp.exp(jnp.sum(c_lambda_q1[0] * c_lambda_k1[0]))
           - jnp.exp(jnp.sum(c_lambda_q2[0] * c_lambda_k2[0])) + lambda_init)
    o = _attention(qkv, lam, c_subln[0], batch, seq, lambda_init)
    h = _out_proj(o, c_w_out[0].astype(BF16), h)
    h = ffn(h, 1, 1, final=True)
    return h.reshape(batch, seq, d)
```

```python
import functools
import math

import jax
import jax.numpy as jnp
import numpy as np
from jax import lax
from jax.experimental import pallas as pl
from jax.experimental.pallas import tpu as pltpu

F32 = jnp.float32
BF16 = jnp.bfloat16

D_MODEL = 2048
CHUNK = 64
NORM_EPS = 1e-6

RET_WIDTH = 1024
RET_HEADS = 4
RET_HEAD_DIM = 256
RET_THETA = 10000.0
RET_BLOCK = 256

SSM_WIDTH = 1024
SSM_GROUP = 16
SSM_GROUPS = 64
SSM_STATE = 64
SSM_T = 16
SSM_GPB = 128 // SSM_GROUP
SSM_NJ = SSM_WIDTH // 128

DIFF_HEAD_DIM = 128
DIFF_HEADS = 8
ROPE_THETA = 500000.0
ROT_HALF = DIFF_HEAD_DIM // 4 // 2
ATT_QBLOCK = 512

FF_TILE = 1024
NARROW_STEP = 1
CAST_ROWS = 512
PROJ_TN = 1024
VMEM_LIMIT = 56 * 1024 * 1024


def _params(*sem):
    return pltpu.CompilerParams(dimension_semantics=sem, vmem_limit_bytes=VMEM_LIMIT)


def _rms(x, g):
    y = x * lax.rsqrt(jnp.mean(x * x, axis=-1, keepdims=True) + NORM_EPS)
    return y * g


def _dot(a, b):
    return jnp.dot(a, b, preferred_element_type=F32)


def _dot_nt(a, b):
    return lax.dot_general(a, b, (((1,), (1,)), ((), ())), preferred_element_type=F32)


def _dot_tn(a, b):
    return lax.dot_general(a, b, (((0,), (0,)), ((), ())), preferred_element_type=F32)


def _cast_body(w_ref, o_ref):
    o_ref[...] = w_ref[...].astype(BF16)


def _cast_stack(w):
    a, b, r, c = w.shape
    spec = pl.BlockSpec((None, None, CAST_ROWS, c), lambda s, i: (s // b, s % b, i, 0))
    return pl.pallas_call(
        _cast_body,
        grid=(a * b, pl.cdiv(r, CAST_ROWS)),
        in_specs=[spec],
        out_specs=spec,
        out_shape=jax.ShapeDtypeStruct(w.shape, BF16),
        compiler_params=_params("parallel", "parallel"),
        name="cast_bf16",
    )(w)


def _cast_tiles_body(w_ref, o_ref):
    cols = w_ref.shape[1]
    for k in range(o_ref.shape[0]):
        lo = k * FF_TILE
        width = min(FF_TILE, cols - lo)
        o_ref[k, :, :width] = w_ref[:, lo:lo + width].astype(BF16)
        if width < FF_TILE:
            o_ref[k, :, width:] = jnp.zeros((o_ref.shape[1], FF_TILE - width), BF16)


def _cast_col_tiles(w):
    a, b, r, c = w.shape
    steps = pl.cdiv(c, FF_TILE)
    return pl.pallas_call(
        _cast_tiles_body,
        grid=(a * b, r // CAST_ROWS),
        in_specs=[pl.BlockSpec((None, None, CAST_ROWS, c), lambda s, i: (s // b, s % b, i, 0))],
        out_specs=pl.BlockSpec((None, None, steps, CAST_ROWS, FF_TILE), lambda s, i: (s // b, s % b, 0, i, 0)),
        out_shape=jax.ShapeDtypeStruct((a, b, steps, r, FF_TILE), BF16),
        compiler_params=_params("parallel", "parallel"),
        name="cast_col_tiles",
    )(w)


def _ffn_body(x_ref, g_ref, wg_ref, wu_ref, wd_ref, fn_ref, o_ref, xn_ref, acc_ref, *, final, tail):
    j = pl.program_id(1)
    last = pl.num_programs(1) - 1

    def tile(width):
        xn = xn_ref[...]
        gate = _dot(xn, wg_ref[:, :width])
        up = _dot(xn, wu_ref[:, :width])
        act = (gate * jax.nn.sigmoid(gate) * up).astype(BF16)
        return _dot(act, wd_ref[:width, :])

    @pl.when(j == 0)
    def _():
        xn_ref[...] = _rms(x_ref[...], g_ref[...]).astype(BF16)
        acc_ref[...] = tile(FF_TILE)

    @pl.when(j == NARROW_STEP)
    def _():
        acc_ref[...] += tile(tail)

    @pl.when((j > 0) & (j < last) & (j != NARROW_STEP))
    def _():
        acc_ref[...] += tile(FF_TILE)

    @pl.when(j == last)
    def _():
        h = x_ref[...] + 0.5 * (acc_ref[...] + tile(FF_TILE))
        if final:
            h = _rms(h, fn_ref[...])
        o_ref[...] = h


def _ffn(h, norm_g, wg, wu, wd, final_g, layer, half, *, final):
    n, d = h.shape
    tm = min(512, n)
    ff = wd.shape[-2]
    steps = wg.shape[2]
    tail = ff - (steps - 1) * FF_TILE
    assert steps > NARROW_STEP + 1

    def ff_tile(j):
        return jnp.where(j < NARROW_STEP, j, jnp.where(j == NARROW_STEP, steps - 1, j - 1))

    return pl.pallas_call(
        functools.partial(_ffn_body, final=final, tail=tail),
        grid=(n // tm, steps),
        in_specs=[
            pl.BlockSpec((tm, d), lambda i, j: (i, 0)),
            pl.BlockSpec((1, d), lambda i, j: (0, 0)),
            pl.BlockSpec((None, None, None, d, FF_TILE), lambda i, j: (layer, half, ff_tile(j), 0, 0)),
            pl.BlockSpec((None, None, None, d, FF_TILE), lambda i, j: (layer, half, ff_tile(j), 0, 0)),
            pl.BlockSpec((None, None, FF_TILE, d), lambda i, j: (layer, half, ff_tile(j), 0)),
            pl.BlockSpec((1, d), lambda i, j: (0, 0)),
        ],
        out_specs=pl.BlockSpec((tm, d), lambda i, j: (i, 0)),
        out_shape=jax.ShapeDtypeStruct((n, d), F32),
        scratch_shapes=[pltpu.VMEM((tm, d), BF16), pltpu.VMEM((tm, d), F32)],
        compiler_params=_params("parallel", "arbitrary"),
        name="ffn",
    )(h, norm_g.reshape(1, d), wg, wu, wd, final_g.reshape(1, d))


def _proj_ab_body(x_ref, g_ref, w_ref, cos_ref, sin_ref, o_ref, xn_ref):
    @pl.when(pl.program_id(1) == 0)
    def _():
        xn_ref[...] = _rms(x_ref[...], g_ref[...]).astype(BF16)

    y = _dot(xn_ref[...], w_ref[...])
    c = cos_ref[...]
    s = sin_ref[...]
    half = RET_HEAD_DIM // 2
    parts = []
    for hd in range(PROJ_TN // RET_HEAD_DIM):
        lo = hd * RET_HEAD_DIM
        x1 = y[:, lo:lo + half]
        x2 = y[:, lo + half:lo + 2 * half]
        parts += [x1 * c - x2 * s, x2 * c + x1 * s]
    o_ref[...] = jnp.concatenate(parts, axis=1)


def _proj_ab(h, norm_g, w, cos_tab, sin_tab, seq):
    n, d = h.shape
    nout = w.shape[1]
    tm = min(1024, seq)
    per_seq = seq // tm
    tab = pl.BlockSpec((None, tm, RET_HEAD_DIM // 2), lambda i, j: (jnp.minimum(j, 2), i % per_seq, 0))
    return pl.pallas_call(
        _proj_ab_body,
        grid=(n // tm, nout // PROJ_TN),
        in_specs=[
            pl.BlockSpec((tm, d), lambda i, j: (i, 0)),
            pl.BlockSpec((1, d), lambda i, j: (0, 0)),
            pl.BlockSpec((d, PROJ_TN), lambda i, j: (0, j)),
            tab, tab,
        ],
        out_specs=pl.BlockSpec((tm, PROJ_TN), lambda i, j: (i, j)),
        out_shape=jax.ShapeDtypeStruct((n, nout), F32),
        scratch_shapes=[pltpu.VMEM((tm, d), BF16)],
        compiler_params=_params("parallel", "arbitrary"),
        name="proj_ab",
    )(h, norm_g.reshape(1, d), w, cos_tab, sin_tab)


def _cast_qkv_body(w_ref, o_ref):
    dh = DIFF_HEAD_DIM
    old = lax.broadcasted_iota(jnp.int32, (dh, dh), 0)
    new = lax.broadcasted_iota(jnp.int32, (dh, dh), 1)
    src = jnp.where(new < ROT_HALF, new,
                    jnp.where(new < dh // 2, new + ROT_HALF,
                              jnp.where(new < dh // 2 + ROT_HALF, new - (dh // 2 - ROT_HALF), new)))
    src = jnp.where(pl.program_id(1) < 2 * D_MODEL // PROJ_TN, src, new)
    perm = jnp.where(old == src, 1.0, 0.0).astype(BF16)
    w = w_ref[...].astype(BF16)
    parts = [_dot(w[:, hd * dh:(hd + 1) * dh], perm) for hd in range(PROJ_TN // dh)]
    o_ref[...] = jnp.concatenate(parts, axis=1).astype(BF16)


def _cast_qkv(w):
    r, c = w.shape
    spec = pl.BlockSpec((CAST_ROWS, PROJ_TN), lambda i, j: (i, j))
    return pl.pallas_call(
        _cast_qkv_body,
        grid=(r // CAST_ROWS, c // PROJ_TN),
        in_specs=[spec],
        out_specs=spec,
        out_shape=jax.ShapeDtypeStruct(w.shape, BF16),
        compiler_params=_params("parallel", "parallel"),
        name="cast_qkv",
    )(w)


def _proj_c_body(x_ref, g_ref, w_ref, cf_ref, sg_ref, o_ref, xn_ref):
    @pl.when(pl.program_id(1) == 0)
    def _():
        xn_ref[...] = _rms(x_ref[...], g_ref[...]).astype(BF16)

    y = _dot(xn_ref[...], w_ref[...])
    cf = cf_ref[...]
    sg = sg_ref[...]
    parts = []
    for hd in range(PROJ_TN // DIFF_HEAD_DIM):
        lo = hd * DIFF_HEAD_DIM
        yh = y[:, lo:lo + DIFF_HEAD_DIM]
        r = yh * cf + pltpu.roll(yh, DIFF_HEAD_DIM // 2, axis=1) * sg
        parts.append(r.astype(BF16))
    o_ref[...] = jnp.concatenate(parts, axis=1)


def _proj_c(h, norm_g, w, cf, sg, seq):
    n, d = h.shape
    nout = w.shape[1]
    tm = min(1024, seq)
    per_seq = seq // tm
    qk_tiles = 2 * D_MODEL // PROJ_TN
    tab = pl.BlockSpec((None, tm, DIFF_HEAD_DIM), lambda i, j: (j // qk_tiles, i % per_seq, 0))
    return pl.pallas_call(
        _proj_c_body,
        grid=(n // tm, nout // PROJ_TN),
        in_specs=[
            pl.BlockSpec((tm, d), lambda i, j: (i, 0)),
            pl.BlockSpec((1, d), lambda i, j: (0, 0)),
            pl.BlockSpec((d, PROJ_TN), lambda i, j: (0, j)),
            tab, tab,
        ],
        out_specs=pl.BlockSpec((tm, PROJ_TN), lambda i, j: (i, j)),
        out_shape=jax.ShapeDtypeStruct((n, nout), BF16),
        scratch_shapes=[pltpu.VMEM((tm, d), BF16)],
        compiler_params=_params("parallel", "arbitrary"),
        name="proj_c",
    )(h, norm_g.reshape(1, d), w, cf, sg)


def _ret_body(q_ref, k_ref, v_ref, g_ref, dm_ref, qd_ref, kd_ref, cd_ref, o_ref, *, nblk):
    t = RET_BLOCK
    dmask = dm_ref[...]
    qdec = qd_ref[...]
    kdec = kd_ref[...]
    cdec = cd_ref[...]

    def rows(ref, n):
        return ref[pl.ds(pl.multiple_of(n * t, t), t), :]

    def step(n, state):
        q = rows(q_ref, n)
        k = rows(k_ref, n)
        vb = rows(v_ref, n).astype(BF16)
        scores = _dot_nt(q.astype(BF16), k.astype(BF16)) * dmask
        intra = _dot(scores.astype(BF16), vb)
        inter = _dot((q * qdec).astype(BF16), state.astype(BF16))
        new_state = state * cdec + _dot_tn((k * kdec).astype(BF16), vb)
        o = intra + inter
        o = o * lax.rsqrt(jnp.mean(o * o, axis=-1, keepdims=True) + NORM_EPS)
        gate = rows(g_ref, n)
        r0 = pl.multiple_of(n * t, t)
        o_ref[pl.ds(r0, t), :] = (o * (gate * jax.nn.sigmoid(gate))).astype(BF16)
        return new_state

    lax.fori_loop(0, nblk, step, jnp.zeros((RET_HEAD_DIM, RET_HEAD_DIM), F32), unroll=True)


def _retention(proj, batch, seq):
    n = proj.shape[0]
    t = RET_BLOCK
    hd = RET_HEAD_DIM
    log_g = np.log(1.0 - 2.0 ** (-5.0 - np.arange(RET_HEADS, dtype=np.float64)))
    idx = np.arange(t, dtype=np.float64)
    chunk = np.arange(t) // CHUNK
    visible = chunk[:, None] >= chunk[None, :]
    dist = np.abs(idx[:, None] - idx[None, :])
    dmask = np.where(visible[None], np.exp(log_g[:, None, None] * dist[None]), 0.0)
    f32 = np.float32
    dmask = dmask.astype(f32)
    qdec = np.broadcast_to(np.exp(log_g[:, None] * (idx[None] + 1.0))[:, :, None], (RET_HEADS, t, hd)).astype(f32)
    kdec = np.broadcast_to(np.exp(log_g[:, None] * (t - 1.0 - idx[None]))[:, :, None], (RET_HEADS, t, hd)).astype(f32)
    cdec = np.broadcast_to(np.exp(log_g * t)[:, None, None], (RET_HEADS, 1, hd)).astype(f32)

    def col(off):
        return pl.BlockSpec((seq, hd), lambda b, h: (b, off + h))

    def tab(rows, cols):
        return pl.BlockSpec((None, rows, cols), lambda b, h: (h, 0, 0))

    return pl.pallas_call(
        functools.partial(_ret_body, nblk=seq // t),
        grid=(batch, RET_HEADS),
        in_specs=[col(0), col(RET_HEADS), col(2 * RET_HEADS), col(3 * RET_HEADS),
                  tab(t, t), tab(t, hd), tab(t, hd), tab(1, hd)],
        out_specs=pl.BlockSpec((seq, hd), lambda b, h: (b, h)),
        out_shape=jax.ShapeDtypeStruct((n, RET_WIDTH), BF16),
        compiler_params=_params("parallel", "parallel"),
        name="retention",
    )(proj, proj, proj, proj, dmask, qdec, kdec, cdec)


def _s5_weights(lam_re, lam_im, log_step, b_re, b_im, c_re, c_im):
    hp = lax.Precision.HIGHEST
    step = jnp.exp(log_step)[:, None]
    mag = jnp.exp(lam_re * step)
    a_re = mag * jnp.cos(lam_im * step)
    a_im = mag * jnp.sin(lam_im * step)
    den = lam_re * lam_re + lam_im * lam_im
    nr = a_re - 1.0
    f_re = (nr * lam_re + a_im * lam_im) / den
    f_im = (a_im * lam_re - nr * lam_im) / den
    bb_re = f_re[..., None] * b_re - f_im[..., None] * b_im
    bb_im = f_re[..., None] * b_im + f_im[..., None] * b_re
    prs, pis = [jnp.ones_like(a_re)], [jnp.zeros_like(a_im)]
    for _ in range(SSM_T):
        prs.append(prs[-1] * a_re - pis[-1] * a_im)
        pis.append(prs[-2] * a_im + pis[-1] * a_re)
    pr = jnp.stack(prs)
    pi = jnp.stack(pis)
    ca_re = c_re[None] * pr[:, :, None, :] - c_im[None] * pi[:, :, None, :]
    ca_im = c_re[None] * pi[:, :, None, :] + c_im[None] * pr[:, :, None, :]
    kern = jnp.einsum('kgpm,gmq->gkpq', jnp.concatenate([ca_re[:SSM_T], -ca_im[:SSM_T]], axis=-1),
                      jnp.concatenate([bb_re, bb_im], axis=1), precision=hp)
    nj = SSM_NJ
    ks = kern.transpose(0, 3, 1, 2).reshape(nj, 128, SSM_T * SSM_GROUP)
    ro = jnp.stack([ca_re[1:], -ca_im[1:]])
    ro = ro.reshape(2, SSM_T, nj, SSM_GPB, SSM_GROUP, SSM_STATE).transpose(2, 0, 3, 5, 1, 4)
    wo = ro.reshape(nj, 2 * SSM_GPB * SSM_STATE, SSM_T * SSM_GROUP)
    rev_re = jnp.stack([prs[SSM_T - 1 - s] for s in range(SSM_T)])
    rev_im = jnp.stack([pis[SSM_T - 1 - s] for s in range(SSM_T)])
    rev_re, rev_im = rev_re[:, :, None, :], rev_im[:, :, None, :]
    bt_re, bt_im = bb_re.transpose(0, 2, 1)[None], bb_im.transpose(0, 2, 1)[None]
    wi = jnp.concatenate([rev_re * bt_re - rev_im * bt_im, rev_re * bt_im + rev_im * bt_re], axis=-1)
    wi = wi.reshape(SSM_T, nj, 128, 2 * SSM_STATE).transpose(1, 0, 2, 3).reshape(nj, SSM_T * 128, 2 * SSM_STATE)
    a_t = jnp.stack([pr[SSM_T], pi[SSM_T]]).reshape(2, nj, SSM_GPB * SSM_STATE)
    a_t = jnp.moveaxis(a_t, 0, 1)
    return ks, wo, wi, a_t


def _s5_expand_body(ks_ref, wo_ref, wi_ref, wy_ref, win_ref):
    tp = SSM_T * SSM_GROUP
    wide = SSM_T * 128
    half = SSM_GPB * SSM_STATE

    def iota(shape, dim):
        return lax.broadcasted_iota(jnp.int32, shape, dim)

    r = iota((tp, wide), 0)
    c = iota((tp, wide), 1)
    rep = jnp.where((r // SSM_GROUP == c // 128) & (r % SSM_GROUP == c % SSM_GROUP), 1.0, 0.0).astype(BF16)
    col_b = (iota((1, wide), 1) % 128) // SSM_GROUP

    row_a = iota((128, 1), 0) // SSM_GROUP
    bdk = jnp.where(row_a == col_b, _dot(ks_ref[...].astype(BF16), rep), 0.0).astype(BF16)
    for s in range(SSM_T):
        if s:
            wy_ref[s * 128:(s + 1) * 128, :s * 128] = jnp.zeros((128, s * 128), BF16)
        wy_ref[s * 128:(s + 1) * 128, s * 128:] = bdk[:, :wide - s * 128]

    row_a = (iota((2 * half, 1), 0) % half) // SSM_STATE
    ro = _dot(wo_ref[...].astype(BF16), rep)
    wy_ref[wide:, :] = jnp.where(row_a == col_b, ro, 0.0).astype(BF16)

    r = iota((2 * SSM_STATE, 2 * half), 0)
    c = iota((2 * SSM_STATE, 2 * half), 1)
    rep_in = jnp.where((r // SSM_STATE == c // half) & (r % SSM_STATE == c % SSM_STATE), 1.0, 0.0).astype(BF16)
    row_a = (iota((wide, 1), 0) % 128) // SSM_GROUP
    col_b = (iota((1, 2 * half), 1) % half) // SSM_STATE
    win = _dot(wi_ref[...].astype(BF16), rep_in)
    win_ref[...] = jnp.where(row_a == col_b, win, 0.0).astype(BF16)


def _s5_expand(ks, wo, wi):
    nj = ks.shape[0]
    wide = SSM_T * 128
    half2 = 2 * SSM_GPB * SSM_STATE

    def blk(a):
        return pl.BlockSpec((None,) + a.shape[1:], lambda j: (j, 0, 0))

    return pl.pallas_call(
        _s5_expand_body,
        grid=(nj,),
        in_specs=[blk(ks), blk(wo), blk(wi)],
        out_specs=[pl.BlockSpec((None, wide + half2, wide), lambda j: (j, 0, 0)),
                   pl.BlockSpec((None, wide, half2), lambda j: (j, 0, 0))],
        out_shape=[jax.ShapeDtypeStruct((nj, wide + half2, wide), BF16),
                   jax.ShapeDtypeStruct((nj, wide, half2), BF16)],
        compiler_params=_params("parallel"),
        name="s5_expand",
    )(ks, wo, wi)


def _s5_body(u_ref, wy_ref, win_ref, at_ref, y_ref, s_scr, hp_scr, *, nb, nchunk):
    half = SSM_GPB * SSM_STATE
    r = nb * nchunk
    u = jnp.concatenate([u_ref[pl.ds(s, r, stride=SSM_T), :] for s in range(SSM_T)],
                        axis=1).astype(BF16)
    nslab = half // 128
    s_all = _dot(u, win_ref[...])
    for k in range(nslab):
        s_scr[k, 0:r, :] = s_all[:, k * 128:(k + 1) * 128]
        s_scr[k, r:2 * r, :] = s_all[:, half + k * 128:half + (k + 1) * 128]
    is_re = lax.broadcasted_iota(jnp.int32, (2 * nb, 128), 0) < nb
    coef_same = [jnp.broadcast_to(at_ref[0:1, k * 128:(k + 1) * 128], (2 * nb, 128)) for k in range(nslab)]
    coef_swap = [jnp.where(is_re, -at_ref[1:2, k * 128:(k + 1) * 128], at_ref[1:2, k * 128:(k + 1) * 128])
                 for k in range(nslab)]
    h = [jnp.zeros((2 * nb, 128), F32)] * nslab
    for c in range(nchunk):
        chunk_rows = pl.ds(c, 2 * nb, stride=nchunk)
        for k in range(nslab):
            hp_scr[k, chunk_rows, :] = h[k]
            h[k] = (coef_same[k] * h[k] + coef_swap[k] * pltpu.roll(h[k], nb, axis=0)
                    + s_scr[k, chunk_rows, :])
    hp = jnp.concatenate([hp_scr[k, 0:r, :] for k in range(nslab)]
                         + [hp_scr[k, r:2 * r, :] for k in range(nslab)], axis=1).astype(BF16)
    wide = SSM_T * 128
    cb = 256
    for lo in range(0, wide, cb):
        y = (_dot(u[:, :lo + cb], wy_ref[:lo + cb, lo:lo + cb])
             + _dot(hp, wy_ref[wide:, lo:lo + cb]))
        for t in range(lo // 128, (lo + cb) // 128):
            y_ref[pl.ds(t, r, stride=SSM_T), :] = y[:, t * 128 - lo:(t + 1) * 128 - lo]


def _s5(proj, w_y, w_in, a_t, batch, seq):
    n, width = proj.shape
    nchunk = seq // SSM_T
    nb = min(4, batch)
    r = nb * nchunk
    ucol0 = (width - SSM_WIDTH) // 128
    half2 = 2 * SSM_GPB * SSM_STATE
    once = pl.Buffered(1)
    return pl.pallas_call(
        functools.partial(_s5_body, nb=nb, nchunk=nchunk),
        grid=(SSM_NJ, batch // nb),
        in_specs=[
            pl.BlockSpec((r * SSM_T, 128), lambda j, i: (i, ucol0 + j)),
            pl.BlockSpec((None, SSM_T * 128 + half2, SSM_T * 128), lambda j, i: (j, 0, 0), pipeline_mode=once),
            pl.BlockSpec((None, SSM_T * 128, half2), lambda j, i: (j, 0, 0), pipeline_mode=once),
            pl.BlockSpec((None, 2, half2 // 2), lambda j, i: (j, 0, 0)),
        ],
        out_specs=pl.BlockSpec((r * SSM_T, 128), lambda j, i: (i, j)),
        out_shape=jax.ShapeDtypeStruct((n, SSM_WIDTH), F32),
        scratch_shapes=[pltpu.VMEM((half2 // 256, 2 * r, 128), F32), pltpu.VMEM((half2 // 256, 2 * r, 128), F32)],
        compiler_params=_params("arbitrary", "arbitrary"),
        name="s5",
    )(proj, w_y, w_in, a_t)


def _gelu(x):
    return 0.5 * x * (1.0 + jnp.tanh(math.sqrt(2.0 / math.pi) * (x + 0.044715 * (x * x * x))))


def _ab_out_body(ya_ref, ys_ref, u_ref, d_ref, wglu_ref, bglu_ref, woa_ref, wob_ref, h_ref, o_ref):
    y = ys_ref[...] + d_ref[...] * u_ref[...]
    z = _gelu(y)
    gate = jax.nn.sigmoid(_dot(z.astype(BF16), wglu_ref[...]) + bglu_ref[...])
    yb = (z * gate).astype(BF16)
    o_ref[...] = h_ref[...] + (_dot(ya_ref[...], woa_ref[...]) + _dot(yb, wob_ref[...]))


def _ab_out(y_a, y_ssm, proj, d_skip, w_glu, b_glu, w_out, h):
    n, d = h.shape
    tm = min(512, n)
    ucol = (proj.shape[1] - SSM_WIDTH) // SSM_WIDTH
    row = lambda i: (i, 0)
    fixed = lambda i: (0, 0)
    return pl.pallas_call(
        _ab_out_body,
        grid=(n // tm,),
        in_specs=[
            pl.BlockSpec((tm, RET_WIDTH), row),
            pl.BlockSpec((tm, SSM_WIDTH), row),
            pl.BlockSpec((tm, SSM_WIDTH), lambda i: (i, ucol)),
            pl.BlockSpec((1, SSM_WIDTH), fixed),
            pl.BlockSpec((SSM_WIDTH, SSM_WIDTH), fixed),
            pl.BlockSpec((1, SSM_WIDTH), fixed),
            pl.BlockSpec((RET_WIDTH, d), fixed),
            pl.BlockSpec((SSM_WIDTH, d), lambda i: (1, 0)),
            pl.BlockSpec((tm, d), row),
        ],
        out_specs=pl.BlockSpec((tm, d), row),
        out_shape=jax.ShapeDtypeStruct((n, d), F32),
        compiler_params=_params("parallel"),
        name="ab_out",
    )(y_a, y_ssm, proj, d_skip.reshape(1, -1), w_glu, b_glu.reshape(1, -1), w_out, w_out, h)


def _att_body(lam_ref, q_ref, k_ref, v_ref, sub_ref, o_ref, *, seq, lambda_init):
    tq = min(ATT_QBLOCK, seq)
    dh = DIFF_HEAD_DIM
    k2 = dh ** -0.5 * math.log2(math.e)
    lam = lam_ref[0]
    neg = jnp.finfo(F32).min
    rc = lax.broadcasted_iota(jnp.int32, (tq, tq), 0) // CHUNK
    cc = lax.broadcasted_iota(jnp.int32, (tq, tq), 1) // CHUNK
    visible = rc >= cc
    for qb in range(seq // tq):
        q0 = qb * tq
        q = q_ref[q0:q0 + tq, :]
        v_diag = v_ref[q0:q0 + tq, :]
        comps = []
        for c in range(2):
            qc = q[:, c * dh:(c + 1) * dh]
            s_diag = _dot_nt(qc, k_ref[q0:q0 + tq, c * dh:(c + 1) * dh])
            s_diag = jnp.where(visible, s_diag, neg)
            m = jnp.max(s_diag, axis=-1, keepdims=True)
            if qb > 0:
                s_off = _dot_nt(qc, k_ref[0:q0, c * dh:(c + 1) * dh])
                m = jnp.maximum(m, jnp.max(s_off, axis=-1, keepdims=True))
            mk = m * k2
            p_diag = jnp.exp2(s_diag * k2 - mk)
            l = jnp.sum(p_diag, axis=-1, keepdims=True)
            acc = _dot(p_diag.astype(BF16), v_diag)
            if qb > 0:
                p_off = jnp.exp2(s_off * k2 - mk)
                l = l + jnp.sum(p_off, axis=-1, keepdims=True)
                acc = acc + _dot(p_off.astype(BF16), v_ref[0:q0, :])
            comps.append(acc / l)
        o = comps[0] - lam * comps[1]
        o = _rms(o, sub_ref[...]) * (1.0 - lambda_init)
        o_ref[q0:q0 + tq, :] = o.astype(BF16)


def _attention(qkv, lam, subln, batch, seq, lambda_init):
    n = qkv.shape[0]
    w = 2 * DIFF_HEAD_DIM

    def col(off):
        return pl.BlockSpec((seq, w), lambda b, h: (b, off + h))

    return pl.pallas_call(
        functools.partial(_att_body, seq=seq, lambda_init=lambda_init),
        grid=(batch, DIFF_HEADS),
        in_specs=[
            pl.BlockSpec(memory_space=pltpu.SMEM),
            col(0), col(DIFF_HEADS), col(2 * DIFF_HEADS),
            pl.BlockSpec((1, w), lambda b, h: (0, 0)),
        ],
        out_specs=pl.BlockSpec((seq, w), lambda b, h: (b, h)),
        out_shape=jax.ShapeDtypeStruct((n, D_MODEL), BF16),
        compiler_params=_params("parallel", "parallel"),
        name="diff_attention",
    )(lam.reshape(1), qkv, qkv, qkv, subln.reshape(1, w))


def _out_proj_body(o_ref, w_ref, h_ref, y_ref):
    y_ref[...] = h_ref[...] + _dot(o_ref[...], w_ref[...])


def _out_proj(o, w, h):
    n, d = h.shape
    tm = min(512, n)
    return pl.pallas_call(
        _out_proj_body,
        grid=(n // tm,),
        in_specs=[
            pl.BlockSpec((tm, d), lambda i: (i, 0)),
            pl.BlockSpec((d, d), lambda i: (0, 0)),
            pl.BlockSpec((tm, d), lambda i: (i, 0)),
        ],
        out_specs=pl.BlockSpec((tm, d), lambda i: (i, 0)),
        out_shape=jax.ShapeDtypeStruct((n, d), F32),
        compiler_params=_params("parallel"),
        name="out_proj",
    )(o, w, h)


def _rope_tables(seq, rot_dim, theta):
    inv = 1.0 / (theta ** (np.arange(0, rot_dim, 2, dtype=np.float64) / rot_dim))
    ang = np.arange(seq, dtype=np.float64)[:, None] * inv[None, :]
    return np.cos(ang), np.sin(ang)


def kernel(x, ffn_norm, ffn_w_gate, ffn_w_up, ffn_w_down, mix_norm, ab_w_in, ab_w_out, ssm_lambda_re, ssm_lambda_im, ssm_log_step, ssm_b_re, ssm_b_im, ssm_c_re, ssm_c_im, ssm_d, ssm_w_glu, ssm_b_glu, c_w_qkv, c_w_out, c_lambda_q1, c_lambda_k1, c_lambda_q2, c_lambda_k2, c_subln, final_norm):
    batch, seq, d = x.shape
    n = batch * seq
    h = x.reshape(n, d)
    wg = _cast_col_tiles(ffn_w_gate)
    wu = _cast_col_tiles(ffn_w_up)
    wd = _cast_stack(ffn_w_down)

    def ffn(h, layer, half, final=False):
        return _ffn(h, ffn_norm[layer, half], wg, wu, wd, final_norm, layer, half, final=final)

    h = ffn(h, 0, 0)
    ret_cos, ret_sin = _rope_tables(seq, RET_HEAD_DIM, RET_THETA)
    k_scale = RET_HEAD_DIM ** -0.5
    cos_tab = np.stack([ret_cos, ret_cos * k_scale, np.ones_like(ret_cos)]).astype(np.float32)
    sin_tab = np.stack([ret_sin, ret_sin * k_scale, np.zeros_like(ret_sin)]).astype(np.float32)
    proj = _proj_ab(h, mix_norm[0], ab_w_in[0].astype(BF16), cos_tab, sin_tab, seq)
    y_a = _retention(proj, batch, seq)
    ks, wo, wi, a_t = _s5_weights(ssm_lambda_re[0], ssm_lambda_im[0], ssm_log_step[0], ssm_b_re[0],
                                  ssm_b_im[0], ssm_c_re[0], ssm_c_im[0])
    w_y, w_in = _s5_expand(ks, wo, wi)
    y_ssm = _s5(proj, w_y, w_in, a_t, batch, seq)
    h = _ab_out(y_a, y_ssm, proj, ssm_d[0], ssm_w_glu[0].astype(BF16), ssm_b_glu[0],
                ab_w_out[0].astype(BF16), h)
    h = ffn(h, 0, 1)

    h = ffn(h, 1, 0)
    att_cos, att_sin = _rope_tables(seq, 2 * ROT_HALF, ROPE_THETA)
    rest = np.zeros((seq, DIFF_HEAD_DIM // 2 - ROT_HALF))
    ident = np.zeros((seq, DIFF_HEAD_DIM))
    cf = np.stack([np.concatenate([att_cos, rest + 1.0, att_cos, rest + 1.0], axis=1), ident + 1.0])
    sg = np.stack([np.concatenate([-att_sin, rest, att_sin, rest], axis=1), ident])
    cf, sg = cf.astype(np.float32), sg.astype(np.float32)
    qkv = _proj_c(h, mix_norm[1], _cast_qkv(c_w_qkv[0]), cf, sg, seq)
    lambda_init = 0.8 - 0.6 * math.exp(-0.3 * 1)
    lam = (jnp.exp(jnp.sum(c_lambda_q1[0] * c_lambda_k1[0]))
           - jnp.exp(jnp.sum(c_lambda_q2[0] * c_lambda_k2[0])) + lambda_init)
    o = _attention(qkv, lam, c_subln[0], batch, seq, lambda_init)
    h = _out_proj(o, c_w_out[0].astype(BF16), h)
    h = ffn(h, 1, 1, final=True)
    return h.reshape(batch, seq, d)
```

```python
import functools
import math

import jax
import jax.numpy as jnp
import numpy as np
from jax import lax
from jax.experimental import pallas as pl
from jax.experimental.pallas import tpu as pltpu

F32 = jnp.float32
BF16 = jnp.bfloat16

D_MODEL = 2048
CHUNK = 64
NORM_EPS = 1e-6

RET_WIDTH = 1024
RET_HEADS = 4
RET_HEAD_DIM = 256
RET_THETA = 10000.0
RET_BLOCK = 256

SSM_WIDTH = 1024
SSM_GROUP = 16
SSM_GROUPS = 64
SSM_STATE = 64
SSM_T = 16
SSM_GPB = 128 // SSM_GROUP
SSM_NJ = SSM_WIDTH // 128

DIFF_HEAD_DIM = 128
DIFF_HEADS = 8
ROPE_THETA = 500000.0
ROT_HALF = DIFF_HEAD_DIM // 4 // 2
ATT_QBLOCK = 512

FF_TILE = 1024
NARROW_STEP = 1
CAST_ROWS = 512
PROJ_TN = 1024
VMEM_LIMIT = 56 * 1024 * 1024


def _params(*sem):
    return pltpu.CompilerParams(dimension_semantics=sem, vmem_limit_bytes=VMEM_LIMIT)


def _rms(x, g):
    y = x * lax.rsqrt(jnp.mean(x * x, axis=-1, keepdims=True) + NORM_EPS)
    return y * g


def _dot(a, b):
    return jnp.dot(a, b, preferred_element_type=F32)


def _dot_nt(a, b):
    return lax.dot_general(a, b, (((1,), (1,)), ((), ())), preferred_element_type=F32)


def _dot_tn(a, b):
    return lax.dot_general(a, b, (((0,), (0,)), ((), ())), preferred_element_type=F32)


def _cast_tiles_body(w_ref, o_ref):
    cols = w_ref.shape[1]
    for k in range(o_ref.shape[0]):
        lo = k * FF_TILE
        width = min(FF_TILE, cols - lo)
        o_ref[k, :, :width] = w_ref[:, lo:lo + width].astype(BF16)
        if width < FF_TILE:
            o_ref[k, :, width:] = jnp.zeros((o_ref.shape[1], FF_TILE - width), BF16)


def _cast_col_tiles(w):
    a, b, r, c = w.shape
    steps = pl.cdiv(c, FF_TILE)
    return pl.pallas_call(
        _cast_tiles_body,
        grid=(a * b, r // CAST_ROWS),
        in_specs=[pl.BlockSpec((None, None, CAST_ROWS, c), lambda s, i: (s // b, s % b, i, 0))],
        out_specs=pl.BlockSpec((None, None, steps, CAST_ROWS, FF_TILE), lambda s, i: (s // b, s % b, 0, i, 0)),
        out_shape=jax.ShapeDtypeStruct((a, b, steps, r, FF_TILE), BF16),
        compiler_params=_params("parallel", "parallel"),
        name="cast_col_tiles",
    )(w)


def _cast_row_tiles_body(w_ref, o_ref, *, rows):
    valid = rows - pl.program_id(1) * FF_TILE
    row = lax.broadcasted_iota(jnp.int32, w_ref.shape, 0)
    o_ref[...] = jnp.where(row < valid, w_ref[...], 0.0).astype(BF16)


def _cast_row_tiles(w):
    a, b, r, c = w.shape
    steps = pl.cdiv(r, FF_TILE)
    return pl.pallas_call(
        functools.partial(_cast_row_tiles_body, rows=r),
        grid=(a * b, steps),
        in_specs=[pl.BlockSpec((None, None, FF_TILE, c), lambda s, k: (s // b, s % b, k, 0))],
        out_specs=pl.BlockSpec((None, None, None, FF_TILE, c), lambda s, k: (s // b, s % b, k, 0, 0)),
        out_shape=jax.ShapeDtypeStruct((a, b, steps, FF_TILE, c), BF16),
        compiler_params=_params("parallel", "parallel"),
        name="cast_row_tiles",
    )(w)


def _ffn_body(x_ref, g_ref, wg_ref, wu_ref, wd_ref, wgn_ref, wun_ref, wdn_ref, fn_ref, o_ref, xn_ref, *, final):
    j = pl.program_id(1)
    last = pl.num_programs(1) - 1

    def tile(wg, wu, wd):
        xn = xn_ref[...]
        gate = _dot(xn, wg[...])
        up = _dot(xn, wu[...])
        act = (gate * jax.nn.sigmoid(gate) * up).astype(BF16)
        return _dot(act, wd[...])

    @pl.when(j == 0)
    def _():
        xn_ref[...] = _rms(x_ref[...], g_ref[...]).astype(BF16)
        o_ref[...] = tile(wg_ref, wu_ref, wd_ref)

    @pl.when(j == NARROW_STEP)
    def _():
        o_ref[...] += tile(wgn_ref, wun_ref, wdn_ref)

    @pl.when((j > 0) & (j < last) & (j != NARROW_STEP))
    def _():
        o_ref[...] += tile(wg_ref, wu_ref, wd_ref)

    @pl.when(j == last)
    def _():
        h = x_ref[...] + 0.5 * (o_ref[...] + tile(wg_ref, wu_ref, wd_ref))
        if final:
            h = _rms(h, fn_ref[...])
        o_ref[...] = h


def _ffn(h, norm_g, wg, wu, wd, final_g, layer, half, ff, *, final):
    n, d = h.shape
    tm = min(512, n)
    tiles = wg.shape[2]
    narrow = ff - (tiles - 1) * FF_TILE
    assert 0 < NARROW_STEP < tiles - 1 and narrow % 128 == 0

    def streamed(j):
        return jnp.where(j <= NARROW_STEP, j, j - 1)

    once = pl.Buffered(1)
    return pl.pallas_call(
        functools.partial(_ffn_body, final=final),
        grid=(n // tm, tiles),
        in_specs=[
            pl.BlockSpec((tm, d), lambda i, j: (i, 0)),
            pl.BlockSpec((1, d), lambda i, j: (0, 0)),
            pl.BlockSpec((None, None, None, d, FF_TILE), lambda i, j: (layer, half, streamed(j), 0, 0)),
            pl.BlockSpec((None, None, None, d, FF_TILE), lambda i, j: (layer, half, streamed(j), 0, 0)),
            pl.BlockSpec((None, None, None, FF_TILE, d), lambda i, j: (layer, half, streamed(j), 0, 0)),
            pl.BlockSpec((None, None, None, d, narrow), lambda i, j: (layer, half, tiles - 1, 0, 0), pipeline_mode=once),
            pl.BlockSpec((None, None, None, d, narrow), lambda i, j: (layer, half, tiles - 1, 0, 0), pipeline_mode=once),
            pl.BlockSpec((None, None, None, narrow, d), lambda i, j: (layer, half, tiles - 1, 0, 0), pipeline_mode=once),
            pl.BlockSpec((1, d), lambda i, j: (0, 0)),
        ],
        out_specs=pl.BlockSpec((tm, d), lambda i, j: (i, 0)),
        out_shape=jax.ShapeDtypeStruct((n, d), F32),
        scratch_shapes=[pltpu.VMEM((tm, d), BF16)],
        compiler_params=_params("parallel", "arbitrary"),
        name="ffn",
    )(h, norm_g.reshape(1, d), wg, wu, wd, wg, wu, wd, final_g.reshape(1, d))


def _proj_ab_body(x_ref, g_ref, w_ref, cos_ref, sin_ref, o_ref, xn_ref):
    @pl.when(pl.program_id(1) == 0)
    def _():
        xn_ref[...] = _rms(x_ref[...], g_ref[...]).astype(BF16)

    y = _dot(xn_ref[...], w_ref[...])
    c = cos_ref[...]
    s = sin_ref[...]
    half = RET_HEAD_DIM // 2
    parts = []
    for hd in range(PROJ_TN // RET_HEAD_DIM):
        lo = hd * RET_HEAD_DIM
        x1 = y[:, lo:lo + half]
        x2 = y[:, lo + half:lo + 2 * half]
        parts += [x1 * c - x2 * s, x2 * c + x1 * s]
    o_ref[...] = jnp.concatenate(parts, axis=1)


def _proj_ab(h, norm_g, w, cos_tab, sin_tab, seq):
    n, d = h.shape
    nout = w.shape[1]
    tm = min(1024, seq)
    per_seq = seq // tm
    tab = pl.BlockSpec((None, tm, RET_HEAD_DIM // 2), lambda i, j: (jnp.minimum(j, 2), i % per_seq, 0))
    return pl.pallas_call(
        _proj_ab_body,
        grid=(n // tm, nout // PROJ_TN),
        in_specs=[
            pl.BlockSpec((tm, d), lambda i, j: (i, 0)),
            pl.BlockSpec((1, d), lambda i, j: (0, 0)),
            pl.BlockSpec((d, PROJ_TN), lambda i, j: (0, j)),
            tab, tab,
        ],
        out_specs=pl.BlockSpec((tm, PROJ_TN), lambda i, j: (i, j)),
        out_shape=jax.ShapeDtypeStruct((n, nout), F32),
        scratch_shapes=[pltpu.VMEM((tm, d), BF16)],
        compiler_params=_params("parallel", "arbitrary"),
        name="proj_ab",
    )(h, norm_g.reshape(1, d), w, cos_tab, sin_tab)


def _cast_qkv_body(w_ref, o_ref):
    dh = DIFF_HEAD_DIM
    old = lax.broadcasted_iota(jnp.int32, (dh, dh), 0)
    new = lax.broadcasted_iota(jnp.int32, (dh, dh), 1)
    src = jnp.where(new < ROT_HALF, new,
                    jnp.where(new < dh // 2, new + ROT_HALF,
                              jnp.where(new < dh // 2 + ROT_HALF, new - (dh // 2 - ROT_HALF), new)))
    src = jnp.where(pl.program_id(1) < 2 * D_MODEL // PROJ_TN, src, new)
    perm = jnp.where(old == src, 1.0, 0.0).astype(BF16)
    w = w_ref[...].astype(BF16)
    parts = [_dot(w[:, hd * dh:(hd + 1) * dh], perm) for hd in range(PROJ_TN // dh)]
    o_ref[...] = jnp.concatenate(parts, axis=1).astype(BF16)


def _cast_qkv(w):
    r, c = w.shape
    spec = pl.BlockSpec((CAST_ROWS, PROJ_TN), lambda i, j: (i, j))
    return pl.pallas_call(
        _cast_qkv_body,
        grid=(r // CAST_ROWS, c // PROJ_TN),
        in_specs=[spec],
        out_specs=spec,
        out_shape=jax.ShapeDtypeStruct(w.shape, BF16),
        compiler_params=_params("parallel", "parallel"),
        name="cast_qkv",
    )(w)


def _proj_c_body(x_ref, g_ref, w_ref, cf_ref, sg_ref, o_ref, xn_ref):
    @pl.when(pl.program_id(1) == 0)
    def _():
        xn_ref[...] = _rms(x_ref[...], g_ref[...]).astype(BF16)

    y = _dot(xn_ref[...], w_ref[...])
    cf = cf_ref[...]
    sg = sg_ref[...]
    parts = []
    for hd in range(PROJ_TN // DIFF_HEAD_DIM):
        lo = hd * DIFF_HEAD_DIM
        yh = y[:, lo:lo + DIFF_HEAD_DIM]
        r = yh * cf + pltpu.roll(yh, DIFF_HEAD_DIM // 2, axis=1) * sg
        parts.append(r.astype(BF16))
    o_ref[...] = jnp.concatenate(parts, axis=1)


def _proj_c(h, norm_g, w, cf, sg, seq):
    n, d = h.shape
    nout = w.shape[1]
    tm = min(1024, seq)
    per_seq = seq // tm
    qk_tiles = 2 * D_MODEL // PROJ_TN
    tab = pl.BlockSpec((None, tm, DIFF_HEAD_DIM), lambda i, j: (j // qk_tiles, i % per_seq, 0))
    return pl.pallas_call(
        _proj_c_body,
        grid=(n // tm, nout // PROJ_TN),
        in_specs=[
            pl.BlockSpec((tm, d), lambda i, j: (i, 0)),
            pl.BlockSpec((1, d), lambda i, j: (0, 0)),
            pl.BlockSpec((d, PROJ_TN), lambda i, j: (0, j)),
            tab, tab,
        ],
        out_specs=pl.BlockSpec((tm, PROJ_TN), lambda i, j: (i, j)),
        out_shape=jax.ShapeDtypeStruct((n, nout), BF16),
        scratch_shapes=[pltpu.VMEM((tm, d), BF16)],
        compiler_params=_params("parallel", "arbitrary"),
        name="proj_c",
    )(h, norm_g.reshape(1, d), w, cf, sg)


def _ret_body(q_ref, k_ref, v_ref, g_ref, dm_ref, qd_ref, kd_ref, cd_ref, o_ref, *, nblk):
    t = RET_BLOCK
    dmask = dm_ref[...]
    qdec = qd_ref[...]
    kdec = kd_ref[...]
    cdec = cd_ref[...]

    def rows(ref, n):
        return ref[pl.ds(pl.multiple_of(n * t, t), t), :]

    def step(n, state):
        q = rows(q_ref, n)
        k = rows(k_ref, n)
        vb = rows(v_ref, n).astype(BF16)
        scores = _dot_nt(q.astype(BF16), k.astype(BF16)) * dmask
        intra = _dot(scores.astype(BF16), vb)
        inter = _dot((q * qdec).astype(BF16), state.astype(BF16))
        new_state = state * cdec + _dot_tn((k * kdec).astype(BF16), vb)
        o = intra + inter
        o = o * lax.rsqrt(jnp.mean(o * o, axis=-1, keepdims=True) + NORM_EPS)
        gate = rows(g_ref, n)
        r0 = pl.multiple_of(n * t, t)
        o_ref[pl.ds(r0, t), :] = (o * (gate * jax.nn.sigmoid(gate))).astype(BF16)
        return new_state

    lax.fori_loop(0, nblk, step, jnp.zeros((RET_HEAD_DIM, RET_HEAD_DIM), F32), unroll=True)


def _retention(proj, batch, seq):
    n = proj.shape[0]
    t = RET_BLOCK
    hd = RET_HEAD_DIM
    log_g = np.log(1.0 - 2.0 ** (-5.0 - np.arange(RET_HEADS, dtype=np.float64)))
    idx = np.arange(t, dtype=np.float64)
    chunk = np.arange(t) // CHUNK
    visible = chunk[:, None] >= chunk[None, :]
    dist = np.abs(idx[:, None] - idx[None, :])
    dmask = np.where(visible[None], np.exp(log_g[:, None, None] * dist[None]), 0.0)
    f32 = np.float32
    dmask = dmask.astype(f32)
    qdec = np.broadcast_to(np.exp(log_g[:, None] * (idx[None] + 1.0))[:, :, None], (RET_HEADS, t, hd)).astype(f32)
    kdec = np.broadcast_to(np.exp(log_g[:, None] * (t - 1.0 - idx[None]))[:, :, None], (RET_HEADS, t, hd)).astype(f32)
    cdec = np.broadcast_to(np.exp(log_g * t)[:, None, None], (RET_HEADS, 1, hd)).astype(f32)

    def col(off):
        return pl.BlockSpec((seq, hd), lambda b, h: (b, off + h))

    def tab(rows, cols):
        return pl.BlockSpec((None, rows, cols), lambda b, h: (h, 0, 0))

    return pl.pallas_call(
        functools.partial(_ret_body, nblk=seq // t),
        grid=(batch, RET_HEADS),
        in_specs=[col(0), col(RET_HEADS), col(2 * RET_HEADS), col(3 * RET_HEADS),
                  tab(t, t), tab(t, hd), tab(t, hd), tab(1, hd)],
        out_specs=pl.BlockSpec((seq, hd), lambda b, h: (b, h)),
        out_shape=jax.ShapeDtypeStruct((n, RET_WIDTH), BF16),
        compiler_params=_params("parallel", "parallel"),
        name="retention",
    )(proj, proj, proj, proj, dmask, qdec, kdec, cdec)


def _s5_weights(lam_re, lam_im, log_step, b_re, b_im, c_re, c_im):
    hp = lax.Precision.HIGHEST
    step = jnp.exp(log_step)[:, None]
    mag = jnp.exp(lam_re * step)
    a_re = mag * jnp.cos(lam_im * step)
    a_im = mag * jnp.sin(lam_im * step)
    den = lam_re * lam_re + lam_im * lam_im
    nr = a_re - 1.0
    f_re = (nr * lam_re + a_im * lam_im) / den
    f_im = (a_im * lam_re - nr * lam_im) / den
    bb_re = f_re[..., None] * b_re - f_im[..., None] * b_im
    bb_im = f_re[..., None] * b_im + f_im[..., None] * b_re
    prs, pis = [jnp.ones_like(a_re)], [jnp.zeros_like(a_im)]
    for _ in range(SSM_T):
        prs.append(prs[-1] * a_re - pis[-1] * a_im)
        pis.append(prs[-2] * a_im + pis[-1] * a_re)
    pr = jnp.stack(prs)
    pi = jnp.stack(pis)
    ca_re = c_re[None] * pr[:, :, None, :] - c_im[None] * pi[:, :, None, :]
    ca_im = c_re[None] * pi[:, :, None, :] + c_im[None] * pr[:, :, None, :]
    kern = jnp.einsum('kgpm,gmq->gkpq', jnp.concatenate([ca_re[:SSM_T], -ca_im[:SSM_T]], axis=-1),
                      jnp.concatenate([bb_re, bb_im], axis=1), precision=hp)
    nj = SSM_NJ
    ks = kern.transpose(0, 3, 1, 2).reshape(nj, 128, SSM_T * SSM_GROUP)
    ro = jnp.stack([ca_re[1:], -ca_im[1:]])
    ro = ro.reshape(2, SSM_T, nj, SSM_GPB, SSM_GROUP, SSM_STATE).transpose(2, 0, 3, 5, 1, 4)
    wo = ro.reshape(nj, 2 * SSM_GPB * SSM_STATE, SSM_T * SSM_GROUP)
    rev_re = jnp.stack([prs[SSM_T - 1 - s] for s in range(SSM_T)])
    rev_im = jnp.stack([pis[SSM_T - 1 - s] for s in range(SSM_T)])
    rev_re, rev_im = rev_re[:, :, None, :], rev_im[:, :, None, :]
    bt_re, bt_im = bb_re.transpose(0, 2, 1)[None], bb_im.transpose(0, 2, 1)[None]
    wi = jnp.concatenate([rev_re * bt_re - rev_im * bt_im, rev_re * bt_im + rev_im * bt_re], axis=-1)
    wi = wi.reshape(SSM_T, nj, 128, 2 * SSM_STATE).transpose(1, 0, 2, 3).reshape(nj, SSM_T * 128, 2 * SSM_STATE)
    a_t = jnp.stack([pr[SSM_T], pi[SSM_T]]).reshape(2, nj, SSM_GPB * SSM_STATE)
    a_t = jnp.moveaxis(a_t, 0, 1)
    return ks, wo, wi, a_t


def _s5_expand_body(ks_ref, wo_ref, wi_ref, wy_ref, win_ref):
    tp = SSM_T * SSM_GROUP
    wide = SSM_T * 128
    half = SSM_GPB * SSM_STATE

    def iota(shape, dim):
        return lax.broadcasted_iota(jnp.int32, shape, dim)

    r = iota((tp, wide), 0)
    c = iota((tp, wide), 1)
    rep = jnp.where((r // SSM_GROUP == c // 128) & (r % SSM_GROUP == c % SSM_GROUP), 1.0, 0.0).astype(BF16)
    col_b = (iota((1, wide), 1) % 128) // SSM_GROUP

    row_a = iota((128, 1), 0) // SSM_GROUP
    bdk = jnp.where(row_a == col_b, _dot(ks_ref[...].astype(BF16), rep), 0.0).astype(BF16)
    for s in range(SSM_T):
        if s:
            wy_ref[s * 128:(s + 1) * 128, :s * 128] = jnp.zeros((128, s * 128), BF16)
        wy_ref[s * 128:(s + 1) * 128, s * 128:] = bdk[:, :wide - s * 128]

    row_a = (iota((2 * half, 1), 0) % half) // SSM_STATE
    ro = _dot(wo_ref[...].astype(BF16), rep)
    wy_ref[wide:, :] = jnp.where(row_a == col_b, ro, 0.0).astype(BF16)

    r = iota((2 * SSM_STATE, 2 * half), 0)
    c = iota((2 * SSM_STATE, 2 * half), 1)
    rep_in = jnp.where((r // SSM_STATE == c // half) & (r % SSM_STATE == c % SSM_STATE), 1.0, 0.0).astype(BF16)
    row_a = (iota((wide, 1), 0) % 128) // SSM_GROUP
    col_b = (iota((1, 2 * half), 1) % half) // SSM_STATE
    win = _dot(wi_ref[...].astype(BF16), rep_in)
    win_ref[...] = jnp.where(row_a == col_b, win, 0.0).astype(BF16)


def _s5_expand(ks, wo, wi):
    nj = ks.shape[0]
    wide = SSM_T * 128
    half2 = 2 * SSM_GPB * SSM_STATE

    def blk(a):
        return pl.BlockSpec((None,) + a.shape[1:], lambda j: (j, 0, 0))

    return pl.pallas_call(
        _s5_expand_body,
        grid=(nj,),
        in_specs=[blk(ks), blk(wo), blk(wi)],
        out_specs=[pl.BlockSpec((None, wide + half2, wide), lambda j: (j, 0, 0)),
                   pl.BlockSpec((None, wide, half2), lambda j: (j, 0, 0))],
        out_shape=[jax.ShapeDtypeStruct((nj, wide + half2, wide), BF16),
                   jax.ShapeDtypeStruct((nj, wide, half2), BF16)],
        compiler_params=_params("parallel"),
        name="s5_expand",
    )(ks, wo, wi)


def _s5_body(u_ref, wy_ref, win_ref, at_ref, y_ref, s_scr, hp_scr, *, nb, nchunk):
    half = SSM_GPB * SSM_STATE
    r = nb * nchunk
    u = jnp.concatenate([u_ref[pl.ds(s, r, stride=SSM_T), :] for s in range(SSM_T)],
                        axis=1).astype(BF16)
    nslab = half // 128
    s_all = _dot(u, win_ref[...])
    for k in range(nslab):
        s_scr[k, 0:r, :] = s_all[:, k * 128:(k + 1) * 128]
        s_scr[k, r:2 * r, :] = s_all[:, half + k * 128:half + (k + 1) * 128]
    is_re = lax.broadcasted_iota(jnp.int32, (2 * nb, 128), 0) < nb
    coef_same = [jnp.broadcast_to(at_ref[0:1, k * 128:(k + 1) * 128], (2 * nb, 128)) for k in range(nslab)]
    coef_swap = [jnp.where(is_re, -at_ref[1:2, k * 128:(k + 1) * 128], at_ref[1:2, k * 128:(k + 1) * 128])
                 for k in range(nslab)]
    h = [jnp.zeros((2 * nb, 128), F32)] * nslab
    for c in range(nchunk):
        chunk_rows = pl.ds(c, 2 * nb, stride=nchunk)
        for k in range(nslab):
            hp_scr[k, chunk_rows, :] = h[k]
            h[k] = (coef_same[k] * h[k] + coef_swap[k] * pltpu.roll(h[k], nb, axis=0)
                    + s_scr[k, chunk_rows, :])
    hp = jnp.concatenate([hp_scr[k, 0:r, :] for k in range(nslab)]
                         + [hp_scr[k, r:2 * r, :] for k in range(nslab)], axis=1).astype(BF16)
    wide = SSM_T * 128
    cb = 256
    for lo in range(0, wide, cb):
        y = (_dot(u[:, :lo + cb], wy_ref[:lo + cb, lo:lo + cb])
             + _dot(hp, wy_ref[wide:, lo:lo + cb]))
        for t in range(lo // 128, (lo + cb) // 128):
            y_ref[pl.ds(t, r, stride=SSM_T), :] = y[:, t * 128 - lo:(t + 1) * 128 - lo]


def _s5(proj, w_y, w_in, a_t, batch, seq):
    n, width = proj.shape
    nchunk = seq // SSM_T
    nb = min(4, batch)
    r = nb * nchunk
    ucol0 = (width - SSM_WIDTH) // 128
    half2 = 2 * SSM_GPB * SSM_STATE
    once = pl.Buffered(1)
    return pl.pallas_call(
        functools.partial(_s5_body, nb=nb, nchunk=nchunk),
        grid=(SSM_NJ, batch // nb),
        in_specs=[
            pl.BlockSpec((r * SSM_T, 128), lambda j, i: (i, ucol0 + j)),
            pl.BlockSpec((None, SSM_T * 128 + half2, SSM_T * 128), lambda j, i: (j, 0, 0), pipeline_mode=once),
            pl.BlockSpec((None, SSM_T * 128, half2), lambda j, i: (j, 0, 0), pipeline_mode=once),
            pl.BlockSpec((None, 2, half2 // 2), lambda j, i: (j, 0, 0)),
        ],
        out_specs=pl.BlockSpec((r * SSM_T, 128), lambda j, i: (i, j)),
        out_shape=jax.ShapeDtypeStruct((n, SSM_WIDTH), F32),
        scratch_shapes=[pltpu.VMEM((half2 // 256, 2 * r, 128), F32), pltpu.VMEM((half2 // 256, 2 * r, 128), F32)],
        compiler_params=_params("arbitrary", "arbitrary"),
        name="s5",
    )(proj, w_y, w_in, a_t)


def _gelu(x):
    return 0.5 * x * (1.0 + jnp.tanh(math.sqrt(2.0 / math.pi) * (x + 0.044715 * (x * x * x))))


def _ab_out_body(ya_ref, ys_ref, u_ref, d_ref, wglu_ref, bglu_ref, woa_ref, wob_ref, h_ref, o_ref):
    y = ys_ref[...] + d_ref[...] * u_ref[...]
    z = _gelu(y)
    gate = jax.nn.sigmoid(_dot(z.astype(BF16), wglu_ref[...]) + bglu_ref[...])
    yb = (z * gate).astype(BF16)
    o_ref[...] = h_ref[...] + (_dot(ya_ref[...], woa_ref[...]) + _dot(yb, wob_ref[...]))


def _ab_out(y_a, y_ssm, proj, d_skip, w_glu, b_glu, w_out, h):
    n, d = h.shape
    tm = min(512, n)
    ucol = (proj.shape[1] - SSM_WIDTH) // SSM_WIDTH
    row = lambda i: (i, 0)
    fixed = lambda i: (0, 0)
    return pl.pallas_call(
        _ab_out_body,
        grid=(n // tm,),
        in_specs=[
            pl.BlockSpec((tm, RET_WIDTH), row),
            pl.BlockSpec((tm, SSM_WIDTH), row),
            pl.BlockSpec((tm, SSM_WIDTH), lambda i: (i, ucol)),
            pl.BlockSpec((1, SSM_WIDTH), fixed),
            pl.BlockSpec((SSM_WIDTH, SSM_WIDTH), fixed),
            pl.BlockSpec((1, SSM_WIDTH), fixed),
            pl.BlockSpec((RET_WIDTH, d), fixed),
            pl.BlockSpec((SSM_WIDTH, d), lambda i: (1, 0)),
            pl.BlockSpec((tm, d), row),
        ],
        out_specs=pl.BlockSpec((tm, d), row),
        out_shape=jax.ShapeDtypeStruct((n, d), F32),
        compiler_params=_params("parallel"),
        name="ab_out",
    )(y_a, y_ssm, proj, d_skip.reshape(1, -1), w_glu, b_glu.reshape(1, -1), w_out, w_out, h)


def _att_body(lam_ref, q_ref, k_ref, v_ref, sub_ref, o_ref, *, seq, lambda_init):
    tq = min(ATT_QBLOCK, seq)
    dh = DIFF_HEAD_DIM
    k2 = dh ** -0.5 * math.log2(math.e)
    lam = lam_ref[0]
    neg = jnp.finfo(F32).min
    rc = lax.broadcasted_iota(jnp.int32, (tq, tq), 0) // CHUNK
    cc = lax.broadcasted_iota(jnp.int32, (tq, tq), 1) // CHUNK
    visible = rc >= cc
    for qb in range(seq // tq):
        q0 = qb * tq
        q = q_ref[q0:q0 + tq, :]
        v_diag = v_ref[q0:q0 + tq, :]
        comps = []
        for c in range(2):
            qc = q[:, c * dh:(c + 1) * dh]
            s_diag = _dot_nt(qc, k_ref[q0:q0 + tq, c * dh:(c + 1) * dh])
            s_diag = jnp.where(visible, s_diag, neg)
            m = jnp.max(s_diag, axis=-1, keepdims=True)
            if qb > 0:
                s_off = _dot_nt(qc, k_ref[0:q0, c * dh:(c + 1) * dh])
                m = jnp.maximum(m, jnp.max(s_off, axis=-1, keepdims=True))
            mk = m * k2
            p_diag = jnp.exp2(s_diag * k2 - mk)
            l = jnp.sum(p_diag, axis=-1, keepdims=True)
            acc = _dot(p_diag.astype(BF16), v_diag)
            if qb > 0:
                p_off = jnp.exp2(s_off * k2 - mk)
                l = l + jnp.sum(p_off, axis=-1, keepdims=True)
                acc = acc + _dot(p_off.astype(BF16), v_ref[0:q0, :])
            comps.append(acc / l)
        o = comps[0] - lam * comps[1]
        o = _rms(o, sub_ref[...]) * (1.0 - lambda_init)
        o_ref[q0:q0 + tq, :] = o.astype(BF16)


def _attention(qkv, lam, subln, batch, seq, lambda_init):
    n = qkv.shape[0]
    w = 2 * DIFF_HEAD_DIM

    def col(off):
        return pl.BlockSpec((seq, w), lambda b, h: (b, off + h))

    return pl.pallas_call(
        functools.partial(_att_body, seq=seq, lambda_init=lambda_init),
        grid=(batch, DIFF_HEADS),
        in_specs=[
            pl.BlockSpec(memory_space=pltpu.SMEM),
            col(0), col(DIFF_HEADS), col(2 * DIFF_HEADS),
            pl.BlockSpec((1, w), lambda b, h: (0, 0)),
        ],
        out_specs=pl.BlockSpec((seq, w), lambda b, h: (b, h)),
        out_shape=jax.ShapeDtypeStruct((n, D_MODEL), BF16),
        compiler_params=_params("parallel", "parallel"),
        name="diff_attention",
    )(lam.reshape(1), qkv, qkv, qkv, subln.reshape(1, w))


def _out_proj_body(o_ref, w_ref, h_ref, y_ref):
    y_ref[...] = h_ref[...] + _dot(o_ref[...], w_ref[...])


def _out_proj(o, w, h):
    n, d = h.shape
    tm = min(512, n)
    return pl.pallas_call(
        _out_proj_body,
        grid=(n // tm,),
        in_specs=[
            pl.BlockSpec((tm, d), lambda i: (i, 0)),
            pl.BlockSpec((d, d), lambda i: (0, 0)),
            pl.BlockSpec((tm, d), lambda i: (i, 0)),
        ],
        out_specs=pl.BlockSpec((tm, d), lambda i: (i, 0)),
        out_shape=jax.ShapeDtypeStruct((n, d), F32),
        compiler_params=_params("parallel"),
        name="out_proj",
    )(o, w, h)


def _rope_tables(seq, rot_dim, theta):
    inv = 1.0 / (theta ** (np.arange(0, rot_dim, 2, dtype=np.float64) / rot_dim))
    ang = np.arange(seq, dtype=np.float64)[:, None] * inv[None, :]
    return np.cos(ang), np.sin(ang)


def kernel(x, ffn_norm, ffn_w_gate, ffn_w_up, ffn_w_down, mix_norm, ab_w_in, ab_w_out, ssm_lambda_re, ssm_lambda_im, ssm_log_step, ssm_b_re, ssm_b_im, ssm_c_re, ssm_c_im, ssm_d, ssm_w_glu, ssm_b_glu, c_w_qkv, c_w_out, c_lambda_q1, c_lambda_k1, c_lambda_q2, c_lambda_k2, c_subln, final_norm):
    batch, seq, d = x.shape
    n = batch * seq
    h = x.reshape(n, d)
    wg = _cast_col_tiles(ffn_w_gate)
    wu = _cast_col_tiles(ffn_w_up)
    wd = _cast_row_tiles(ffn_w_down)
    d_ff = ffn_w_down.shape[2]

    def ffn(h, layer, half, final=False):
        return _ffn(h, ffn_norm[layer, half], wg, wu, wd, final_norm, layer, half, d_ff, final=final)

    h = ffn(h, 0, 0)
    ret_cos, ret_sin = _rope_tables(seq, RET_HEAD_DIM, RET_THETA)
    k_scale = RET_HEAD_DIM ** -0.5
    cos_tab = np.stack([ret_cos, ret_cos * k_scale, np.ones_like(ret_cos)]).astype(np.float32)
    sin_tab = np.stack([ret_sin, ret_sin * k_scale, np.zeros_like(ret_sin)]).astype(np.float32)
    proj = _proj_ab(h, mix_norm[0], ab_w_in[0].astype(BF16), cos_tab, sin_tab, seq)
    y_a = _retention(proj, batch, seq)
    ks, wo, wi, a_t = _s5_weights(ssm_lambda_re[0], ssm_lambda_im[0], ssm_log_step[0], ssm_b_re[0],
                                  ssm_b_im[0], ssm_c_re[0], ssm_c_im[0])
    w_y, w_in = _s5_expand(ks, wo, wi)
    y_ssm = _s5(proj, w_y, w_in, a_t, batch, seq)
    h = _ab_out(y_a, y_ssm, proj, ssm_d[0], ssm_w_glu[0].astype(BF16), ssm_b_glu[0],
                ab_w_out[0].astype(BF16), h)
    h = ffn(h, 0, 1)

    h = ffn(h, 1, 0)
    att_cos, att_sin = _rope_tables(seq, 2 * ROT_HALF, ROPE_THETA)
    rest = np.zeros((seq, DIFF_HEAD_DIM // 2 - ROT_HALF))
    ident = np.zeros((seq, DIFF_HEAD_DIM))
    cf = np.stack([np.concatenate([att_cos, rest + 1.0, att_cos, rest + 1.0], axis=1), ident + 1.0])
    sg = np.stack([np.concatenate([-att_sin, rest, att_sin, rest], axis=1), ident])
    cf, sg = cf.astype(np.float32), sg.astype(np.float32)
    qkv = _proj_c(h, mix_norm[1], _cast_qkv(c_w_qkv[0]), cf, sg, seq)
    lambda_init = 0.8 - 0.6 * math.exp(-0.3 * 1)
    lam = (jnp.exp(jnp.sum(c_lambda_q1[0] * c_lambda_k1[0]))
           - jnp.exp(jnp.sum(c_lambda_q2[0] * c_lambda_k2[0])) + lambda_init)
    o = _attention(qkv, lam, c_subln[0], batch, seq, lambda_init)
    h = _out_proj(o, c_w_out[0].astype(BF16), h)
    h = ffn(h, 1, 1, final=True)
    return h.reshape(batch, seq, d)
```

```python
import functools
import math

import jax
import jax.numpy as jnp
import numpy as np
from jax import lax
from jax.experimental import pallas as pl
from jax.experimental.pallas import tpu as pltpu

F32 = jnp.float32
BF16 = jnp.bfloat16

D_MODEL = 2048
CHUNK = 64
NORM_EPS = 1e-6

RET_WIDTH = 1024
RET_HEADS = 4
RET_HEAD_DIM = 256
RET_THETA = 10000.0
RET_BLOCK = 256

SSM_WIDTH = 1024
SSM_GROUP = 16
SSM_GROUPS = 64
SSM_STATE = 64
SSM_T = 16
SSM_GPB = 128 // SSM_GROUP
SSM_NJ = SSM_WIDTH // 128

DIFF_HEAD_DIM = 128
DIFF_HEADS = 8
ROPE_THETA = 500000.0
ROT_HALF = DIFF_HEAD_DIM // 4 // 2
ATT_QBLOCK = 512

FF_TILE = 1024
NARROW_STEP = 1
CAST_ROWS = 512
PROJ_TN = 1024
VMEM_LIMIT = 56 * 1024 * 1024


def _params(*sem):
    return pltpu.CompilerParams(dimension_semantics=sem, vmem_limit_bytes=VMEM_LIMIT)


def _rms(x, g):
    y = x * lax.rsqrt(jnp.mean(x * x, axis=-1, keepdims=True) + NORM_EPS)
    return y * g


def _dot(a, b):
    return jnp.dot(a, b, preferred_element_type=F32)


def _dot_nt(a, b):
    return lax.dot_general(a, b, (((1,), (1,)), ((), ())), preferred_element_type=F32)


def _dot_tn(a, b):
    return lax.dot_general(a, b, (((0,), (0,)), ((), ())), preferred_element_type=F32)


def _cast_tiles_body(w_ref, o_ref):
    cols = w_ref.shape[1]
    for k in range(o_ref.shape[0]):
        lo = k * FF_TILE
        width = min(FF_TILE, cols - lo)
        o_ref[k, :, :width] = w_ref[:, lo:lo + width].astype(BF16)
        if width < FF_TILE:
            o_ref[k, :, width:] = jnp.zeros((o_ref.shape[1], FF_TILE - width), BF16)


def _cast_col_tiles(w):
    a, b, r, c = w.shape
    steps = pl.cdiv(c, FF_TILE)
    return pl.pallas_call(
        _cast_tiles_body,
        grid=(a * b, r // CAST_ROWS),
        in_specs=[pl.BlockSpec((None, None, CAST_ROWS, c), lambda s, i: (s // b, s % b, i, 0))],
        out_specs=pl.BlockSpec((None, None, steps, CAST_ROWS, FF_TILE), lambda s, i: (s // b, s % b, 0, i, 0)),
        out_shape=jax.ShapeDtypeStruct((a, b, steps, r, FF_TILE), BF16),
        compiler_params=_params("parallel", "parallel"),
        name="cast_col_tiles",
    )(w)


def _cast_row_tiles_body(w_ref, o_ref, *, rows):
    valid = rows - pl.program_id(1) * FF_TILE
    row = lax.broadcasted_iota(jnp.int32, w_ref.shape, 0)
    o_ref[...] = jnp.where(row < valid, w_ref[...], 0.0).astype(BF16)


def _cast_row_tiles(w):
    a, b, r, c = w.shape
    steps = pl.cdiv(r, FF_TILE)
    return pl.pallas_call(
        functools.partial(_cast_row_tiles_body, rows=r),
        grid=(a * b, steps),
        in_specs=[pl.BlockSpec((None, None, FF_TILE, c), lambda s, k: (s // b, s % b, k, 0))],
        out_specs=pl.BlockSpec((None, None, None, FF_TILE, c), lambda s, k: (s // b, s % b, k, 0, 0)),
        out_shape=jax.ShapeDtypeStruct((a, b, steps, FF_TILE, c), BF16),
        compiler_params=_params("parallel", "parallel"),
        name="cast_row_tiles",
    )(w)


def _ffn_body(x_ref, g_ref, wg_ref, wu_ref, wd_ref, wgn_ref, wun_ref, wdn_ref, fn_ref, o_ref, xn_ref, *, final):
    j = pl.program_id(1)
    last = pl.num_programs(1) - 1

    def tile(wg, wu, wd):
        xn = xn_ref[...]
        gate = _dot(xn, wg[...])
        up = _dot(xn, wu[...])
        act = (gate * jax.nn.sigmoid(gate) * up).astype(BF16)
        return _dot(act, wd[...])

    @pl.when(j == 0)
    def _():
        xn_ref[...] = _rms(x_ref[...], g_ref[...]).astype(BF16)
        o_ref[...] = tile(wg_ref, wu_ref, wd_ref)

    @pl.when(j == NARROW_STEP)
    def _():
        o_ref[...] += tile(wgn_ref, wun_ref, wdn_ref)

    @pl.when((j > 0) & (j < last) & (j != NARROW_STEP))
    def _():
        o_ref[...] += tile(wg_ref, wu_ref, wd_ref)

    @pl.when(j == last)
    def _():
        h = x_ref[...] + 0.5 * (o_ref[...] + tile(wg_ref, wu_ref, wd_ref))
        if final:
            h = _rms(h, fn_ref[...])
        o_ref[...] = h


def _ffn(h, norm_g, wg, wu, wd, final_g, layer, half, ff, *, final):
    n, d = h.shape
    tm = min(512, n)
    tiles = wg.shape[2]
    narrow = ff - (tiles - 1) * FF_TILE
    assert 0 < NARROW_STEP < tiles - 1 and narrow % 128 == 0

    def streamed(j):
        return jnp.where(j <= NARROW_STEP, j, j - 1)

    once = pl.Buffered(1)
    return pl.pallas_call(
        functools.partial(_ffn_body, final=final),
        grid=(n // tm, tiles),
        in_specs=[
            pl.BlockSpec((tm, d), lambda i, j: (i, 0)),
            pl.BlockSpec((1, d), lambda i, j: (0, 0)),
            pl.BlockSpec((None, None, None, d, FF_TILE), lambda i, j: (layer, half, streamed(j), 0, 0)),
            pl.BlockSpec((None, None, None, d, FF_TILE), lambda i, j: (layer, half, streamed(j), 0, 0)),
            pl.BlockSpec((None, None, None, FF_TILE, d), lambda i, j: (layer, half, streamed(j), 0, 0)),
            pl.BlockSpec((None, None, None, d, narrow), lambda i, j: (layer, half, tiles - 1, 0, 0), pipeline_mode=once),
            pl.BlockSpec((None, None, None, d, narrow), lambda i, j: (layer, half, tiles - 1, 0, 0), pipeline_mode=once),
            pl.BlockSpec((None, None, None, narrow, d), lambda i, j: (layer, half, tiles - 1, 0, 0), pipeline_mode=once),
            pl.BlockSpec((1, d), lambda i, j: (0, 0)),
        ],
        out_specs=pl.BlockSpec((tm, d), lambda i, j: (i, 0)),
        out_shape=jax.ShapeDtypeStruct((n, d), F32),
        scratch_shapes=[pltpu.VMEM((tm, d), BF16)],
        compiler_params=_params("parallel", "arbitrary"),
        name="ffn",
    )(h, norm_g.reshape(1, d), wg, wu, wd, wg, wu, wd, final_g.reshape(1, d))


def _proj_ab_body(x_ref, g_ref, w_ref, cos_ref, sin_ref, o_ref, xn_ref):
    @pl.when(pl.program_id(1) == 0)
    def _():
        xn_ref[...] = _rms(x_ref[...], g_ref[...]).astype(BF16)

    col = pl.multiple_of(pl.program_id(1) * PROJ_TN, PROJ_TN)
    y = _dot(xn_ref[...], w_ref[:, pl.ds(col, PROJ_TN)])
    c = cos_ref[...]
    s = sin_ref[...]
    half = RET_HEAD_DIM // 2
    parts = []
    for hd in range(PROJ_TN // RET_HEAD_DIM):
        lo = hd * RET_HEAD_DIM
        x1 = y[:, lo:lo + half]
        x2 = y[:, lo + half:lo + 2 * half]
        parts += [x1 * c - x2 * s, x2 * c + x1 * s]
    o_ref[...] = jnp.concatenate(parts, axis=1)


def _proj_ab(h, norm_g, w, cos_tab, sin_tab, seq):
    n, d = h.shape
    nout = w.shape[1]
    tm = min(1024, seq)
    per_seq = seq // tm
    tab = pl.BlockSpec((None, tm, RET_HEAD_DIM // 2), lambda i, j: (jnp.minimum(j, 2), i % per_seq, 0))
    return pl.pallas_call(
        _proj_ab_body,
        grid=(n // tm, nout // PROJ_TN),
        in_specs=[
            pl.BlockSpec((tm, d), lambda i, j: (i, 0)),
            pl.BlockSpec((1, d), lambda i, j: (0, 0)),
            pl.BlockSpec((d, nout), lambda i, j: (0, 0), pipeline_mode=pl.Buffered(1)),
            tab, tab,
        ],
        out_specs=pl.BlockSpec((tm, PROJ_TN), lambda i, j: (i, j)),
        out_shape=jax.ShapeDtypeStruct((n, nout), F32),
        scratch_shapes=[pltpu.VMEM((tm, d), BF16)],
        compiler_params=_params("parallel", "arbitrary"),
        name="proj_ab",
    )(h, norm_g.reshape(1, d), w, cos_tab, sin_tab)


def _cast_qkv_body(w_ref, o_ref):
    dh = DIFF_HEAD_DIM
    old = lax.broadcasted_iota(jnp.int32, (dh, dh), 0)
    new = lax.broadcasted_iota(jnp.int32, (dh, dh), 1)
    src = jnp.where(new < ROT_HALF, new,
                    jnp.where(new < dh // 2, new + ROT_HALF,
                              jnp.where(new < dh // 2 + ROT_HALF, new - (dh // 2 - ROT_HALF), new)))
    src = jnp.where(pl.program_id(1) < 2 * D_MODEL // PROJ_TN, src, new)
    perm = jnp.where(old == src, 1.0, 0.0).astype(BF16)
    w = w_ref[...].astype(BF16)
    parts = [_dot(w[:, hd * dh:(hd + 1) * dh], perm) for hd in range(PROJ_TN // dh)]
    o_ref[...] = jnp.concatenate(parts, axis=1).astype(BF16)


def _cast_qkv(w):
    r, c = w.shape
    spec = pl.BlockSpec((CAST_ROWS, PROJ_TN), lambda i, j: (i, j))
    return pl.pallas_call(
        _cast_qkv_body,
        grid=(r // CAST_ROWS, c // PROJ_TN),
        in_specs=[spec],
        out_specs=spec,
        out_shape=jax.ShapeDtypeStruct(w.shape, BF16),
        compiler_params=_params("parallel", "parallel"),
        name="cast_qkv",
    )(w)


def _proj_c_body(x_ref, g_ref, w_ref, cf_ref, sg_ref, o_ref, xn_ref):
    @pl.when(pl.program_id(1) == 0)
    def _():
        xn_ref[...] = _rms(x_ref[...], g_ref[...]).astype(BF16)

    col = pl.multiple_of(pl.program_id(1) * PROJ_TN, PROJ_TN)
    y = _dot(xn_ref[...], w_ref[:, pl.ds(col, PROJ_TN)])
    cf = cf_ref[...]
    sg = sg_ref[...]
    parts = []
    for hd in range(PROJ_TN // DIFF_HEAD_DIM):
        lo = hd * DIFF_HEAD_DIM
        yh = y[:, lo:lo + DIFF_HEAD_DIM]
        r = yh * cf + pltpu.roll(yh, DIFF_HEAD_DIM // 2, axis=1) * sg
        parts.append(r.astype(BF16))
    o_ref[...] = jnp.concatenate(parts, axis=1)


def _proj_c(h, norm_g, w, cf, sg, seq):
    n, d = h.shape
    nout = w.shape[1]
    tm = min(1024, seq)
    per_seq = seq // tm
    qk_tiles = 2 * D_MODEL // PROJ_TN
    tab = pl.BlockSpec((None, tm, DIFF_HEAD_DIM), lambda i, j: (j // qk_tiles, i % per_seq, 0))
    return pl.pallas_call(
        _proj_c_body,
        grid=(n // tm, nout // PROJ_TN),
        in_specs=[
            pl.BlockSpec((tm, d), lambda i, j: (i, 0)),
            pl.BlockSpec((1, d), lambda i, j: (0, 0)),
            pl.BlockSpec((d, nout), lambda i, j: (0, 0), pipeline_mode=pl.Buffered(1)),
            tab, tab,
        ],
        out_specs=pl.BlockSpec((tm, PROJ_TN), lambda i, j: (i, j)),
        out_shape=jax.ShapeDtypeStruct((n, nout), BF16),
        scratch_shapes=[pltpu.VMEM((tm, d), BF16)],
        compiler_params=_params("parallel", "arbitrary"),
        name="proj_c",
    )(h, norm_g.reshape(1, d), w, cf, sg)


def _ret_body(q_ref, k_ref, v_ref, g_ref, dm_ref, qd_ref, kd_ref, cd_ref, o_ref, *, nblk):
    t = RET_BLOCK
    dmask = dm_ref[...]
    qdec = qd_ref[...]
    kdec = kd_ref[...]
    cdec = cd_ref[...]

    def rows(ref, n):
        return ref[pl.ds(pl.multiple_of(n * t, t), t), :]

    def step(n, state):
        q = rows(q_ref, n)
        k = rows(k_ref, n)
        vb = rows(v_ref, n).astype(BF16)
        scores = _dot_nt(q.astype(BF16), k.astype(BF16)) * dmask
        intra = _dot(scores.astype(BF16), vb)
        inter = _dot((q * qdec).astype(BF16), state.astype(BF16))
        new_state = state * cdec + _dot_tn((k * kdec).astype(BF16), vb)
        o = intra + inter
        o = o * lax.rsqrt(jnp.mean(o * o, axis=-1, keepdims=True) + NORM_EPS)
        gate = rows(g_ref, n)
        r0 = pl.multiple_of(n * t, t)
        o_ref[pl.ds(r0, t), :] = (o * (gate * jax.nn.sigmoid(gate))).astype(BF16)
        return new_state

    lax.fori_loop(0, nblk, step, jnp.zeros((RET_HEAD_DIM, RET_HEAD_DIM), F32), unroll=True)


def _retention(proj, batch, seq):
    n = proj.shape[0]
    t = RET_BLOCK
    hd = RET_HEAD_DIM
    log_g = np.log(1.0 - 2.0 ** (-5.0 - np.arange(RET_HEADS, dtype=np.float64)))
    idx = np.arange(t, dtype=np.float64)
    chunk = np.arange(t) // CHUNK
    visible = chunk[:, None] >= chunk[None, :]
    dist = np.abs(idx[:, None] - idx[None, :])
    dmask = np.where(visible[None], np.exp(log_g[:, None, None] * dist[None]), 0.0)
    f32 = np.float32
    dmask = dmask.astype(f32)
    qdec = np.broadcast_to(np.exp(log_g[:, None] * (idx[None] + 1.0))[:, :, None], (RET_HEADS, t, hd)).astype(f32)
    kdec = np.broadcast_to(np.exp(log_g[:, None] * (t - 1.0 - idx[None]))[:, :, None], (RET_HEADS, t, hd)).astype(f32)
    cdec = np.broadcast_to(np.exp(log_g * t)[:, None, None], (RET_HEADS, 1, hd)).astype(f32)

    def col(off):
        return pl.BlockSpec((seq, hd), lambda b, h: (b, off + h))

    def tab(rows, cols):
        return pl.BlockSpec((None, rows, cols), lambda b, h: (h, 0, 0))

    return pl.pallas_call(
        functools.partial(_ret_body, nblk=seq // t),
        grid=(batch, RET_HEADS),
        in_specs=[col(0), col(RET_HEADS), col(2 * RET_HEADS), col(3 * RET_HEADS),
                  tab(t, t), tab(t, hd), tab(t, hd), tab(1, hd)],
        out_specs=pl.BlockSpec((seq, hd), lambda b, h: (b, h)),
        out_shape=jax.ShapeDtypeStruct((n, RET_WIDTH), BF16),
        compiler_params=_params("parallel", "parallel"),
        name="retention",
    )(proj, proj, proj, proj, dmask, qdec, kdec, cdec)


def _s5_weights(lam_re, lam_im, log_step, b_re, b_im, c_re, c_im):
    hp = lax.Precision.HIGHEST
    step = jnp.exp(log_step)[:, None]
    mag = jnp.exp(lam_re * step)
    a_re = mag * jnp.cos(lam_im * step)
    a_im = mag * jnp.sin(lam_im * step)
    den = lam_re * lam_re + lam_im * lam_im
    nr = a_re - 1.0
    f_re = (nr * lam_re + a_im * lam_im) / den
    f_im = (a_im * lam_re - nr * lam_im) / den
    bb_re = f_re[..., None] * b_re - f_im[..., None] * b_im
    bb_im = f_re[..., None] * b_im + f_im[..., None] * b_re
    prs, pis = [jnp.ones_like(a_re)], [jnp.zeros_like(a_im)]
    for _ in range(SSM_T):
        prs.append(prs[-1] * a_re - pis[-1] * a_im)
        pis.append(prs[-2] * a_im + pis[-1] * a_re)
    pr = jnp.stack(prs)
    pi = jnp.stack(pis)
    ca_re = c_re[None] * pr[:, :, None, :] - c_im[None] * pi[:, :, None, :]
    ca_im = c_re[None] * pi[:, :, None, :] + c_im[None] * pr[:, :, None, :]
    kern = jnp.einsum('kgpm,gmq->gkpq', jnp.concatenate([ca_re[:SSM_T], -ca_im[:SSM_T]], axis=-1),
                      jnp.concatenate([bb_re, bb_im], axis=1), precision=hp)
    nj = SSM_NJ
    ks = kern.transpose(0, 3, 1, 2).reshape(nj, 128, SSM_T * SSM_GROUP)
    ro = jnp.stack([ca_re[1:], -ca_im[1:]])
    ro = ro.reshape(2, SSM_T, nj, SSM_GPB, SSM_GROUP, SSM_STATE).transpose(2, 0, 3, 5, 1, 4)
    wo = ro.reshape(nj, 2 * SSM_GPB * SSM_STATE, SSM_T * SSM_GROUP)
    rev_re = jnp.stack([prs[SSM_T - 1 - s] for s in range(SSM_T)])
    rev_im = jnp.stack([pis[SSM_T - 1 - s] for s in range(SSM_T)])
    rev_re, rev_im = rev_re[:, :, None, :], rev_im[:, :, None, :]
    bt_re, bt_im = bb_re.transpose(0, 2, 1)[None], bb_im.transpose(0, 2, 1)[None]
    wi = jnp.concatenate([rev_re * bt_re - rev_im * bt_im, rev_re * bt_im + rev_im * bt_re], axis=-1)
    wi = wi.reshape(SSM_T, nj, 128, 2 * SSM_STATE).transpose(1, 0, 2, 3).reshape(nj, SSM_T * 128, 2 * SSM_STATE)
    a_t = jnp.stack([pr[SSM_T], pi[SSM_T]]).reshape(2, nj, SSM_GPB * SSM_STATE)
    a_t = jnp.moveaxis(a_t, 0, 1)
    return ks, wo, wi, a_t


def _s5_expand_body(ks_ref, wo_ref, wi_ref, wy_ref, win_ref):
    tp = SSM_T * SSM_GROUP
    wide = SSM_T * 128
    half = SSM_GPB * SSM_STATE

    def iota(shape, dim):
        return lax.broadcasted_iota(jnp.int32, shape, dim)

    r = iota((tp, wide), 0)
    c = iota((tp, wide), 1)
    rep = jnp.where((r // SSM_GROUP == c // 128) & (r % SSM_GROUP == c % SSM_GROUP), 1.0, 0.0).astype(BF16)
    col_b = (iota((1, wide), 1) % 128) // SSM_GROUP

    row_a = iota((128, 1), 0) // SSM_GROUP
    bdk = jnp.where(row_a == col_b, _dot(ks_ref[...].astype(BF16), rep), 0.0).astype(BF16)
    for s in range(SSM_T):
        if s:
            wy_ref[s * 128:(s + 1) * 128, :s * 128] = jnp.zeros((128, s * 128), BF16)
        wy_ref[s * 128:(s + 1) * 128, s * 128:] = bdk[:, :wide - s * 128]

    row_a = (iota((2 * half, 1), 0) % half) // SSM_STATE
    ro = _dot(wo_ref[...].astype(BF16), rep)
    wy_ref[wide:, :] = jnp.where(row_a == col_b, ro, 0.0).astype(BF16)

    r = iota((2 * SSM_STATE, 2 * half), 0)
    c = iota((2 * SSM_STATE, 2 * half), 1)
    rep_in = jnp.where((r // SSM_STATE == c // half) & (r % SSM_STATE == c % SSM_STATE), 1.0, 0.0).astype(BF16)
    row_a = (iota((wide, 1), 0) % 128) // SSM_GROUP
    col_b = (iota((1, 2 * half), 1) % half) // SSM_STATE
    win = _dot(wi_ref[...].astype(BF16), rep_in)
    win_ref[...] = jnp.where(row_a == col_b, win, 0.0).astype(BF16)


def _s5_expand(ks, wo, wi):
    nj = ks.shape[0]
    wide = SSM_T * 128
    half2 = 2 * SSM_GPB * SSM_STATE

    def blk(a):
        return pl.BlockSpec((None,) + a.shape[1:], lambda j: (j, 0, 0))

    return pl.pallas_call(
        _s5_expand_body,
        grid=(nj,),
        in_specs=[blk(ks), blk(wo), blk(wi)],
        out_specs=[pl.BlockSpec((None, wide + half2, wide), lambda j: (j, 0, 0)),
                   pl.BlockSpec((None, wide, half2), lambda j: (j, 0, 0))],
        out_shape=[jax.ShapeDtypeStruct((nj, wide + half2, wide), BF16),
                   jax.ShapeDtypeStruct((nj, wide, half2), BF16)],
        compiler_params=_params("parallel"),
        name="s5_expand",
    )(ks, wo, wi)


def _s5_body(u_ref, wy_ref, win_ref, at_ref, y_ref, s_scr, hp_scr, *, nb, nchunk):
    half = SSM_GPB * SSM_STATE
    r = nb * nchunk
    u = jnp.concatenate([u_ref[pl.ds(s, r, stride=SSM_T), :] for s in range(SSM_T)],
                        axis=1).astype(BF16)
    nslab = half // 128
    s_all = _dot(u, win_ref[...])
    for k in range(nslab):
        s_scr[k, 0:r, :] = s_all[:, k * 128:(k + 1) * 128]
        s_scr[k, r:2 * r, :] = s_all[:, half + k * 128:half + (k + 1) * 128]
    is_re = lax.broadcasted_iota(jnp.int32, (2 * nb, 128), 0) < nb
    coef_same = [jnp.broadcast_to(at_ref[0:1, k * 128:(k + 1) * 128], (2 * nb, 128)) for k in range(nslab)]
    coef_swap = [jnp.where(is_re, -at_ref[1:2, k * 128:(k + 1) * 128], at_ref[1:2, k * 128:(k + 1) * 128])
                 for k in range(nslab)]
    h = [jnp.zeros((2 * nb, 128), F32)] * nslab
    for c in range(nchunk):
        chunk_rows = pl.ds(c, 2 * nb, stride=nchunk)
        for k in range(nslab):
            hp_scr[k, chunk_rows, :] = h[k]
            h[k] = (coef_same[k] * h[k] + coef_swap[k] * pltpu.roll(h[k], nb, axis=0)
                    + s_scr[k, chunk_rows, :])
    hp = jnp.concatenate([hp_scr[k, 0:r, :] for k in range(nslab)]
                         + [hp_scr[k, r:2 * r, :] for k in range(nslab)], axis=1).astype(BF16)
    wide = SSM_T * 128
    cb = 256
    for lo in range(0, wide, cb):
        y = (_dot(u[:, :lo + cb], wy_ref[:lo + cb, lo:lo + cb])
             + _dot(hp, wy_ref[wide:, lo:lo + cb]))
        for t in range(lo // 128, (lo + cb) // 128):
            y_ref[pl.ds(t, r, stride=SSM_T), :] = y[:, t * 128 - lo:(t + 1) * 128 - lo]


def _s5(proj, w_y, w_in, a_t, batch, seq):
    n, width = proj.shape
    nchunk = seq // SSM_T
    nb = min(4, batch)
    r = nb * nchunk
    ucol0 = (width - SSM_WIDTH) // 128
    half2 = 2 * SSM_GPB * SSM_STATE
    once = pl.Buffered(1)
    return pl.pallas_call(
        functools.partial(_s5_body, nb=nb, nchunk=nchunk),
        grid=(SSM_NJ, batch // nb),
        in_specs=[
            pl.BlockSpec((r * SSM_T, 128), lambda j, i: (i, ucol0 + j)),
            pl.BlockSpec((None, SSM_T * 128 + half2, SSM_T * 128), lambda j, i: (j, 0, 0), pipeline_mode=once),
            pl.BlockSpec((None, SSM_T * 128, half2), lambda j, i: (j, 0, 0), pipeline_mode=once),
            pl.BlockSpec((None, 2, half2 // 2), lambda j, i: (j, 0, 0)),
        ],
        out_specs=pl.BlockSpec((r * SSM_T, 128), lambda j, i: (i, j)),
        out_shape=jax.ShapeDtypeStruct((n, SSM_WIDTH), F32),
        scratch_shapes=[pltpu.VMEM((half2 // 256, 2 * r, 128), F32), pltpu.VMEM((half2 // 256, 2 * r, 128), F32)],
        compiler_params=_params("arbitrary", "arbitrary"),
        name="s5",
    )(proj, w_y, w_in, a_t)


def _gelu(x):
    return 0.5 * x * (1.0 + jnp.tanh(math.sqrt(2.0 / math.pi) * (x + 0.044715 * (x * x * x))))


def _ab_out_body(ya_ref, ys_ref, u_ref, d_ref, wglu_ref, bglu_ref, woa_ref, wob_ref, h_ref, o_ref):
    y = ys_ref[...] + d_ref[...] * u_ref[...]
    z = _gelu(y)
    gate = jax.nn.sigmoid(_dot(z.astype(BF16), wglu_ref[...]) + bglu_ref[...])
    yb = (z * gate).astype(BF16)
    o_ref[...] = h_ref[...] + (_dot(ya_ref[...], woa_ref[...]) + _dot(yb, wob_ref[...]))


def _ab_out(y_a, y_ssm, proj, d_skip, w_glu, b_glu, w_out, h):
    n, d = h.shape
    tm = min(512, n)
    ucol = (proj.shape[1] - SSM_WIDTH) // SSM_WIDTH
    row = lambda i: (i, 0)
    fixed = lambda i: (0, 0)
    return pl.pallas_call(
        _ab_out_body,
        grid=(n // tm,),
        in_specs=[
            pl.BlockSpec((tm, RET_WIDTH), row),
            pl.BlockSpec((tm, SSM_WIDTH), row),
            pl.BlockSpec((tm, SSM_WIDTH), lambda i: (i, ucol)),
            pl.BlockSpec((1, SSM_WIDTH), fixed),
            pl.BlockSpec((SSM_WIDTH, SSM_WIDTH), fixed),
            pl.BlockSpec((1, SSM_WIDTH), fixed),
            pl.BlockSpec((RET_WIDTH, d), fixed),
            pl.BlockSpec((SSM_WIDTH, d), lambda i: (1, 0)),
            pl.BlockSpec((tm, d), row),
        ],
        out_specs=pl.BlockSpec((tm, d), row),
        out_shape=jax.ShapeDtypeStruct((n, d), F32),
        compiler_params=_params("parallel"),
        name="ab_out",
    )(y_a, y_ssm, proj, d_skip.reshape(1, -1), w_glu, b_glu.reshape(1, -1), w_out, w_out, h)


def _att_body(lam_ref, q_ref, k_ref, v_ref, sub_ref, o_ref, *, seq, lambda_init):
    tq = min(ATT_QBLOCK, seq)
    dh = DIFF_HEAD_DIM
    k2 = dh ** -0.5 * math.log2(math.e)
    lam = lam_ref[0]
    neg = jnp.finfo(F32).min
    rc = lax.broadcasted_iota(jnp.int32, (tq, tq), 0) // CHUNK
    cc = lax.broadcasted_iota(jnp.int32, (tq, tq), 1) // CHUNK
    visible = rc >= cc
    for qb in range(seq // tq):
        q0 = qb * tq
        q = q_ref[q0:q0 + tq, :]
        v_diag = v_ref[q0:q0 + tq, :]
        comps = []
        for c in range(2):
            qc = q[:, c * dh:(c + 1) * dh]
            s_diag = _dot_nt(qc, k_ref[q0:q0 + tq, c * dh:(c + 1) * dh])
            s_diag = jnp.where(visible, s_diag, neg)
            m = jnp.max(s_diag, axis=-1, keepdims=True)
            if qb > 0:
                s_off = _dot_nt(qc, k_ref[0:q0, c * dh:(c + 1) * dh])
                m = jnp.maximum(m, jnp.max(s_off, axis=-1, keepdims=True))
            mk = m * k2
            p_diag = jnp.exp2(s_diag * k2 - mk)
            l = jnp.sum(p_diag, axis=-1, keepdims=True)
            acc = _dot(p_diag.astype(BF16), v_diag)
            if qb > 0:
                p_off = jnp.exp2(s_off * k2 - mk)
                l = l + jnp.sum(p_off, axis=-1, keepdims=True)
                acc = acc + _dot(p_off.astype(BF16), v_ref[0:q0, :])
            comps.append(acc / l)
        o = comps[0] - lam * comps[1]
        o = _rms(o, sub_ref[...]) * (1.0 - lambda_init)
        o_ref[q0:q0 + tq, :] = o.astype(BF16)


def _attention(qkv, lam, subln, batch, seq, lambda_init):
    n = qkv.shape[0]
    w = 2 * DIFF_HEAD_DIM

    def col(off):
        return pl.BlockSpec((seq, w), lambda b, h: (b, off + h))

    return pl.pallas_call(
        functools.partial(_att_body, seq=seq, lambda_init=lambda_init),
        grid=(batch, DIFF_HEADS),
        in_specs=[
            pl.BlockSpec(memory_space=pltpu.SMEM),
            col(0), col(DIFF_HEADS), col(2 * DIFF_HEADS),
            pl.BlockSpec((1, w), lambda b, h: (0, 0)),
        ],
        out_specs=pl.BlockSpec((seq, w), lambda b, h: (b, h)),
        out_shape=jax.ShapeDtypeStruct((n, D_MODEL), BF16),
        compiler_params=_params("parallel", "parallel"),
        name="diff_attention",
    )(lam.reshape(1), qkv, qkv, qkv, subln.reshape(1, w))


def _out_proj_body(o_ref, w_ref, h_ref, y_ref):
    y_ref[...] = h_ref[...] + _dot(o_ref[...], w_ref[...])


def _out_proj(o, w, h):
    n, d = h.shape
    tm = min(512, n)
    return pl.pallas_call(
        _out_proj_body,
        grid=(n // tm,),
        in_specs=[
            pl.BlockSpec((tm, d), lambda i: (i, 0)),
            pl.BlockSpec((d, d), lambda i: (0, 0)),
            pl.BlockSpec((tm, d), lambda i: (i, 0)),
        ],
        out_specs=pl.BlockSpec((tm, d), lambda i: (i, 0)),
        out_shape=jax.ShapeDtypeStruct((n, d), F32),
        compiler_params=_params("parallel"),
        name="out_proj",
    )(o, w, h)


def _rope_tables(seq, rot_dim, theta):
    inv = 1.0 / (theta ** (np.arange(0, rot_dim, 2, dtype=np.float64) / rot_dim))
    ang = np.arange(seq, dtype=np.float64)[:, None] * inv[None, :]
    return np.cos(ang), np.sin(ang)


def kernel(x, ffn_norm, ffn_w_gate, ffn_w_up, ffn_w_down, mix_norm, ab_w_in, ab_w_out, ssm_lambda_re, ssm_lambda_im, ssm_log_step, ssm_b_re, ssm_b_im, ssm_c_re, ssm_c_im, ssm_d, ssm_w_glu, ssm_b_glu, c_w_qkv, c_w_out, c_lambda_q1, c_lambda_k1, c_lambda_q2, c_lambda_k2, c_subln, final_norm):
    batch, seq, d = x.shape
    n = batch * seq
    h = x.reshape(n, d)
    wg = _cast_col_tiles(ffn_w_gate)
    wu = _cast_col_tiles(ffn_w_up)
    wd = _cast_row_tiles(ffn_w_down)
    d_ff = ffn_w_down.shape[2]

    def ffn(h, layer, half, final=False):
        return _ffn(h, ffn_norm[layer, half], wg, wu, wd, final_norm, layer, half, d_ff, final=final)

    h = ffn(h, 0, 0)
    ret_cos, ret_sin = _rope_tables(seq, RET_HEAD_DIM, RET_THETA)
    k_scale = RET_HEAD_DIM ** -0.5
    cos_tab = np.stack([ret_cos, ret_cos * k_scale, np.ones_like(ret_cos)]).astype(np.float32)
    sin_tab = np.stack([ret_sin, ret_sin * k_scale, np.zeros_like(ret_sin)]).astype(np.float32)
    proj = _proj_ab(h, mix_norm[0], ab_w_in[0].astype(BF16), cos_tab, sin_tab, seq)
    y_a = _retention(proj, batch, seq)
    ks, wo, wi, a_t = _s5_weights(ssm_lambda_re[0], ssm_lambda_im[0], ssm_log_step[0], ssm_b_re[0],
                                  ssm_b_im[0], ssm_c_re[0], ssm_c_im[0])
    w_y, w_in = _s5_expand(ks, wo, wi)
    y_ssm = _s5(proj, w_y, w_in, a_t, batch, seq)
    h = _ab_out(y_a, y_ssm, proj, ssm_d[0], ssm_w_glu[0].astype(BF16), ssm_b_glu[0],
                ab_w_out[0].astype(BF16), h)
    h = ffn(h, 0, 1)

    h = ffn(h, 1, 0)
    att_cos, att_sin = _rope_tables(seq, 2 * ROT_HALF, ROPE_THETA)
    rest = np.zeros((seq, DIFF_HEAD_DIM // 2 - ROT_HALF))
    ident = np.zeros((seq, DIFF_HEAD_DIM))
    cf = np.stack([np.concatenate([att_cos, rest + 1.0, att_cos, rest + 1.0], axis=1), ident + 1.0])
    sg = np.stack([np.concatenate([-att_sin, rest, att_sin, rest], axis=1), ident])
    cf, sg = cf.astype(np.float32), sg.astype(np.float32)
    qkv = _proj_c(h, mix_norm[1], _cast_qkv(c_w_qkv[0]), cf, sg, seq)
    lambda_init = 0.8 - 0.6 * math.exp(-0.3 * 1)
    lam = (jnp.exp(jnp.sum(c_lambda_q1[0] * c_lambda_k1[0]))
           - jnp.exp(jnp.sum(c_lambda_q2[0] * c_lambda_k2[0])) + lambda_init)
    o = _attention(qkv, lam, c_subln[0], batch, seq, lambda_init)
    h = _out_proj(o, c_w_out[0].astype(BF16), h)
    h = ffn(h, 1, 1, final=True)
    return h.reshape(batch, seq, d)
```

```python
import functools
import math

import jax
import jax.numpy as jnp
import numpy as np
from jax import lax
from jax.experimental import pallas as pl
from jax.experimental.pallas import tpu as pltpu

F32 = jnp.float32
BF16 = jnp.bfloat16

D_MODEL = 2048
CHUNK = 64
NORM_EPS = 1e-6

RET_WIDTH = 1024
RET_HEADS = 4
RET_HEAD_DIM = 256
RET_THETA = 10000.0
RET_BLOCK = 256

SSM_WIDTH = 1024
SSM_GROUP = 16
SSM_GROUPS = 64
SSM_STATE = 64
SSM_T = 16
SSM_GPB = 128 // SSM_GROUP
SSM_NJ = SSM_WIDTH // 128

DIFF_HEAD_DIM = 128
DIFF_HEADS = 8
ROPE_THETA = 500000.0
ROT_HALF = DIFF_HEAD_DIM // 4 // 2
ATT_QBLOCK = 512

FF_TILE = 1024
NARROW_STEP = 1
CAST_ROWS = 512
PROJ_TN = 1024
VMEM_LIMIT = 58 * 1024 * 1024


def _params(*sem):
    return pltpu.CompilerParams(dimension_semantics=sem, vmem_limit_bytes=VMEM_LIMIT)


def _rms(x, g):
    y = x * lax.rsqrt(jnp.mean(x * x, axis=-1, keepdims=True) + NORM_EPS)
    return y * g


def _dot(a, b):
    return jnp.dot(a, b, preferred_element_type=F32)


def _dot_nt(a, b):
    return lax.dot_general(a, b, (((1,), (1,)), ((), ())), preferred_element_type=F32)


def _dot_tn(a, b):
    return lax.dot_general(a, b, (((0,), (0,)), ((), ())), preferred_element_type=F32)


def _cast_tiles_body(w_ref, o_ref):
    cols = w_ref.shape[1]
    for k in range(o_ref.shape[0]):
        lo = k * FF_TILE
        width = min(FF_TILE, cols - lo)
        o_ref[k, :, :width] = w_ref[:, lo:lo + width].astype(BF16)
        if width < FF_TILE:
            o_ref[k, :, width:] = jnp.zeros((o_ref.shape[1], FF_TILE - width), BF16)


def _cast_col_tiles(w, layer, half):
    r, c = w.shape[2:]
    tiles = pl.cdiv(c, FF_TILE)
    return pl.pallas_call(
        _cast_tiles_body,
        grid=(r // CAST_ROWS,),
        in_specs=[pl.BlockSpec((None, None, CAST_ROWS, c), lambda i: (layer, half, i, 0))],
        out_specs=pl.BlockSpec((tiles, CAST_ROWS, FF_TILE), lambda i: (0, i, 0)),
        out_shape=jax.ShapeDtypeStruct((tiles, r, FF_TILE), BF16),
        compiler_params=_params("parallel"),
        name="cast_col_tiles",
    )(w)


def _cast_row_tiles_body(w_ref, o_ref, *, rows):
    valid = rows - pl.program_id(0) * FF_TILE
    row = lax.broadcasted_iota(jnp.int32, w_ref.shape, 0)
    o_ref[...] = jnp.where(row < valid, w_ref[...], 0.0).astype(BF16)


def _cast_row_tiles(w, layer, half):
    r, c = w.shape[2:]
    tiles = pl.cdiv(r, FF_TILE)
    return pl.pallas_call(
        functools.partial(_cast_row_tiles_body, rows=r),
        grid=(tiles,),
        in_specs=[pl.BlockSpec((None, None, FF_TILE, c), lambda k: (layer, half, k, 0))],
        out_specs=pl.BlockSpec((None, FF_TILE, c), lambda k: (k, 0, 0)),
        out_shape=jax.ShapeDtypeStruct((tiles, FF_TILE, c), BF16),
        compiler_params=_params("parallel"),
        name="cast_row_tiles",
    )(w)


def _cast_ffn_weights(w_gate, w_up, w_down, layer, half):
    return (_cast_col_tiles(w_gate, layer, half), _cast_col_tiles(w_up, layer, half),
            _cast_row_tiles(w_down, layer, half))


class _Rider:
    def __init__(self, w_gate, w_up, w_down, layer, half, steps, step_of):
        r, c = w_gate.shape[2:]
        rows_dn, d = w_down.shape[2:]
        tiles = pl.cdiv(c, FF_TILE)
        sub = 16
        col_rows = sub * pl.cdiv(r // sub, steps)
        row_rows = next((k for k in (64, 128, 256, 512, FF_TILE)
                         if FF_TILE % k == 0 and rows_dn % k == 0 and rows_dn // k <= steps), None)
        self.ok = r % col_rows == 0 and row_rows is not None
        if not self.ok:
            return
        ncol, nrow, per_tile = r // col_rows, rows_dn // row_rows, FF_TILE // row_rows

        def col_slab(*g):
            return jnp.minimum(step_of(*g), ncol - 1)

        def row_slab(*g):
            return jnp.minimum(step_of(*g), nrow - 1)

        col_in = pl.BlockSpec((None, None, col_rows, c), lambda *g: (layer, half, col_slab(*g), 0))
        self.args = (w_gate, w_up, w_down)
        self.in_specs = [col_in, col_in,
                         pl.BlockSpec((None, None, row_rows, d), lambda *g: (layer, half, row_slab(*g), 0))]
        col_out = pl.BlockSpec((tiles, col_rows, FF_TILE), lambda *g: (0, col_slab(*g), 0))
        self.out_specs = [col_out, col_out,
                          pl.BlockSpec((None, row_rows, d),
                                       lambda *g: (row_slab(*g) // per_tile, row_slab(*g) % per_tile, 0))]
        self.out_shape = [jax.ShapeDtypeStruct((tiles, r, FF_TILE), BF16)] * 2 + [
            jax.ShapeDtypeStruct((pl.cdiv(rows_dn, FF_TILE), FF_TILE, d), BF16)]


def _rider_body(gate_ref, up_ref, down_ref, gate_out, up_out, down_out):
    _cast_tiles_body(gate_ref, gate_out)
    _cast_tiles_body(up_ref, up_out)
    down_out[...] = down_ref[...].astype(BF16)


def _ffn_body(x_ref, g_ref, wg_ref, wu_ref, wd_ref, wgn_ref, wun_ref, wdn_ref, fn_ref, o_ref, xn_ref, *, final):
    j = pl.program_id(1)
    last = pl.num_programs(1) - 1

    def tile(wg, wu, wd):
        xn = xn_ref[...]
        gate = _dot(xn, wg[...])
        up = _dot(xn, wu[...])
        act = (gate * jax.nn.sigmoid(gate) * up).astype(BF16)
        return _dot(act, wd[...])

    @pl.when(j == 0)
    def _():
        xn_ref[...] = _rms(x_ref[...], g_ref[...]).astype(BF16)
        o_ref[...] = tile(wg_ref, wu_ref, wd_ref)

    @pl.when(j == NARROW_STEP)
    def _():
        o_ref[...] += tile(wgn_ref, wun_ref, wdn_ref)

    @pl.when((j > 0) & (j < last) & (j != NARROW_STEP))
    def _():
        o_ref[...] += tile(wg_ref, wu_ref, wd_ref)

    @pl.when(j == last)
    def _():
        h = x_ref[...] + 0.5 * (o_ref[...] + tile(wg_ref, wu_ref, wd_ref))
        if final:
            h = _rms(h, fn_ref[...])
        o_ref[...] = h


def _ffn(h, norm_g, wg, wu, wd, final_g, ff, *, final):
    n, d = h.shape
    tm = min(512, n)
    tiles = wg.shape[0]
    narrow = ff - (tiles - 1) * FF_TILE
    assert 0 < NARROW_STEP < tiles - 1 and narrow % 128 == 0

    def streamed(j):
        return jnp.where(j <= NARROW_STEP, j, j - 1)

    once = pl.Buffered(1)
    return pl.pallas_call(
        functools.partial(_ffn_body, final=final),
        grid=(n // tm, tiles),
        in_specs=[
            pl.BlockSpec((tm, d), lambda i, j: (i, 0)),
            pl.BlockSpec((1, d), lambda i, j: (0, 0)),
            pl.BlockSpec((None, d, FF_TILE), lambda i, j: (streamed(j), 0, 0)),
            pl.BlockSpec((None, d, FF_TILE), lambda i, j: (streamed(j), 0, 0)),
            pl.BlockSpec((None, FF_TILE, d), lambda i, j: (streamed(j), 0, 0)),
            pl.BlockSpec((None, d, narrow), lambda i, j: (tiles - 1, 0, 0), pipeline_mode=once),
            pl.BlockSpec((None, d, narrow), lambda i, j: (tiles - 1, 0, 0), pipeline_mode=once),
            pl.BlockSpec((None, narrow, d), lambda i, j: (tiles - 1, 0, 0), pipeline_mode=once),
            pl.BlockSpec((1, d), lambda i, j: (0, 0)),
        ],
        out_specs=pl.BlockSpec((tm, d), lambda i, j: (i, 0)),
        out_shape=jax.ShapeDtypeStruct((n, d), F32),
        scratch_shapes=[pltpu.VMEM((tm, d), BF16)],
        compiler_params=_params("parallel", "arbitrary"),
        name="ffn",
    )(h, norm_g.reshape(1, d), wg, wu, wd, wg, wu, wd, final_g.reshape(1, d))


def _proj_ab_body(x_ref, g_ref, w_ref, cos_ref, sin_ref, *rest):
    riding = len(rest) > 2
    o_ref, xn_ref = rest[3 if riding else 0], rest[-1]
    if riding:
        _rider_body(*rest[:3], *rest[4:7])

    @pl.when(pl.program_id(1) == 0)
    def _():
        xn_ref[...] = _rms(x_ref[...], g_ref[...]).astype(BF16)

    col = pl.multiple_of(pl.program_id(1) * PROJ_TN, PROJ_TN)
    y = _dot(xn_ref[...], w_ref[:, pl.ds(col, PROJ_TN)])
    c = cos_ref[...]
    s = sin_ref[...]
    half = RET_HEAD_DIM // 2
    parts = []
    for hd in range(PROJ_TN // RET_HEAD_DIM):
        lo = hd * RET_HEAD_DIM
        x1 = y[:, lo:lo + half]
        x2 = y[:, lo + half:lo + 2 * half]
        parts += [x1 * c - x2 * s, x2 * c + x1 * s]
    o_ref[...] = jnp.concatenate(parts, axis=1)


def _proj_ab(h, norm_g, w, cos_tab, sin_tab, seq, ride):
    n, d = h.shape
    nout = w.shape[1]
    tm = min(1024, seq)
    per_seq = seq // tm
    ncols = nout // PROJ_TN
    rider = _Rider(*ride, (n // tm) * ncols, lambda i, j: i * ncols + j)
    extra = rider if rider.ok else None
    tab = pl.BlockSpec((None, tm, RET_HEAD_DIM // 2), lambda i, j: (jnp.minimum(j, 2), i % per_seq, 0))
    out = pl.pallas_call(
        _proj_ab_body,
        grid=(n // tm, nout // PROJ_TN),
        in_specs=[
            pl.BlockSpec((tm, d), lambda i, j: (i, 0)),
            pl.BlockSpec((1, d), lambda i, j: (0, 0)),
            pl.BlockSpec((d, nout), lambda i, j: (0, 0), pipeline_mode=pl.Buffered(1)),
            tab, tab,
        ] + (extra.in_specs if extra else []),
        out_specs=[pl.BlockSpec((tm, PROJ_TN), lambda i, j: (i, j))] + (extra.out_specs if extra else []),
        out_shape=[jax.ShapeDtypeStruct((n, nout), F32)] + (extra.out_shape if extra else []),
        scratch_shapes=[pltpu.VMEM((tm, d), BF16)],
        compiler_params=_params("arbitrary", "arbitrary"),
        name="proj_ab",
    )(h, norm_g.reshape(1, d), w, cos_tab, sin_tab, *(extra.args if extra else ()))
    return out[0], (tuple(out[1:]) if extra else None)


def _cast_qkv_body(w_ref, o_ref):
    dh = DIFF_HEAD_DIM
    old = lax.broadcasted_iota(jnp.int32, (dh, dh), 0)
    new = lax.broadcasted_iota(jnp.int32, (dh, dh), 1)
    src = jnp.where(new < ROT_HALF, new,
                    jnp.where(new < dh // 2, new + ROT_HALF,
                              jnp.where(new < dh // 2 + ROT_HALF, new - (dh // 2 - ROT_HALF), new)))
    src = jnp.where(pl.program_id(1) < 2 * D_MODEL // PROJ_TN, src, new)
    perm = jnp.where(old == src, 1.0, 0.0).astype(BF16)
    w = w_ref[...].astype(BF16)
    parts = [_dot(w[:, hd * dh:(hd + 1) * dh], perm) for hd in range(PROJ_TN // dh)]
    o_ref[...] = jnp.concatenate(parts, axis=1).astype(BF16)


def _cast_qkv(w):
    r, c = w.shape
    spec = pl.BlockSpec((CAST_ROWS, PROJ_TN), lambda i, j: (i, j))
    return pl.pallas_call(
        _cast_qkv_body,
        grid=(r // CAST_ROWS, c // PROJ_TN),
        in_specs=[spec],
        out_specs=spec,
        out_shape=jax.ShapeDtypeStruct(w.shape, BF16),
        compiler_params=_params("parallel", "parallel"),
        name="cast_qkv",
    )(w)


def _proj_c_body(x_ref, g_ref, w_ref, cf_ref, sg_ref, o_ref, xn_ref):
    def column_tile(w_tile):
        y = _dot(xn_ref[...], w_tile)
        cf = cf_ref[...]
        sg = sg_ref[...]
        parts = []
        for hd in range(PROJ_TN // DIFF_HEAD_DIM):
            lo = hd * DIFF_HEAD_DIM
            yh = y[:, lo:lo + DIFF_HEAD_DIM]
            r = yh * cf + pltpu.roll(yh, DIFF_HEAD_DIM // 2, axis=1) * sg
            parts.append(r.astype(BF16))
        o_ref[...] = jnp.concatenate(parts, axis=1)

    @pl.when(pl.program_id(1) == 0)
    def _():
        xn_ref[...] = _rms(x_ref[...], g_ref[...]).astype(BF16)
        column_tile(w_ref[:, 0:PROJ_TN])

    @pl.when(pl.program_id(1) > 0)
    def _():
        col = pl.multiple_of(pl.program_id(1) * PROJ_TN, PROJ_TN)
        column_tile(w_ref[:, pl.ds(col, PROJ_TN)])


def _proj_c(h, norm_g, w, cf, sg, seq):
    n, d = h.shape
    nout = w.shape[1]
    tm = min(1024, seq)
    per_seq = seq // tm
    qk_tiles = 2 * D_MODEL // PROJ_TN
    tab = pl.BlockSpec((None, tm, DIFF_HEAD_DIM), lambda i, j: (j // qk_tiles, i % per_seq, 0))
    return pl.pallas_call(
        _proj_c_body,
        grid=(n // tm, nout // PROJ_TN),
        in_specs=[
            pl.BlockSpec((tm, d), lambda i, j: (i, 0)),
            pl.BlockSpec((1, d), lambda i, j: (0, 0)),
            pl.BlockSpec((d, nout), lambda i, j: (0, 0), pipeline_mode=pl.Buffered(1)),
            tab, tab,
        ],
        out_specs=pl.BlockSpec((tm, PROJ_TN), lambda i, j: (i, j)),
        out_shape=jax.ShapeDtypeStruct((n, nout), BF16),
        scratch_shapes=[pltpu.VMEM((tm, d), BF16)],
        compiler_params=_params("parallel", "arbitrary"),
        name="proj_c",
    )(h, norm_g.reshape(1, d), w, cf, sg)


def _ret_body(q_ref, k_ref, v_ref, g_ref, dm_ref, qd_ref, kd_ref, cd_ref, o_ref, *, nblk):
    t = RET_BLOCK
    dmask = dm_ref[...]
    qdec = qd_ref[...]
    kdec = kd_ref[...]
    cdec = cd_ref[...]

    def rows(ref, n):
        return ref[pl.ds(pl.multiple_of(n * t, t), t), :]

    def step(n, state):
        q = rows(q_ref, n)
        k = rows(k_ref, n)
        vb = rows(v_ref, n).astype(BF16)
        scores = _dot_nt(q.astype(BF16), k.astype(BF16)) * dmask
        intra = _dot(scores.astype(BF16), vb)
        inter = _dot((q * qdec).astype(BF16), state.astype(BF16))
        new_state = state * cdec + _dot_tn((k * kdec).astype(BF16), vb)
        o = intra + inter
        o = o * lax.rsqrt(jnp.mean(o * o, axis=-1, keepdims=True) + NORM_EPS)
        gate = rows(g_ref, n)
        r0 = pl.multiple_of(n * t, t)
        o_ref[pl.ds(r0, t), :] = (o * (gate * jax.nn.sigmoid(gate))).astype(BF16)
        return new_state

    lax.fori_loop(0, nblk, step, jnp.zeros((RET_HEAD_DIM, RET_HEAD_DIM), F32), unroll=True)


def _retention(proj, batch, seq):
    n = proj.shape[0]
    t = RET_BLOCK
    hd = RET_HEAD_DIM
    log_g = np.log(1.0 - 2.0 ** (-5.0 - np.arange(RET_HEADS, dtype=np.float64)))
    idx = np.arange(t, dtype=np.float64)
    chunk = np.arange(t) // CHUNK
    visible = chunk[:, None] >= chunk[None, :]
    dist = np.abs(idx[:, None] - idx[None, :])
    dmask = np.where(visible[None], np.exp(log_g[:, None, None] * dist[None]), 0.0)
    f32 = np.float32
    dmask = dmask.astype(f32)
    qdec = np.broadcast_to(np.exp(log_g[:, None] * (idx[None] + 1.0))[:, :, None], (RET_HEADS, t, hd)).astype(f32)
    kdec = np.broadcast_to(np.exp(log_g[:, None] * (t - 1.0 - idx[None]))[:, :, None], (RET_HEADS, t, hd)).astype(f32)
    cdec = np.broadcast_to(np.exp(log_g * t)[:, None, None], (RET_HEADS, 1, hd)).astype(f32)

    def col(off):
        return pl.BlockSpec((seq, hd), lambda b, h: (b, off + h))

    def tab(rows, cols):
        return pl.BlockSpec((None, rows, cols), lambda b, h: (h, 0, 0))

    return pl.pallas_call(
        functools.partial(_ret_body, nblk=seq // t),
        grid=(batch, RET_HEADS),
        in_specs=[col(0), col(RET_HEADS), col(2 * RET_HEADS), col(3 * RET_HEADS),
                  tab(t, t), tab(t, hd), tab(t, hd), tab(1, hd)],
        out_specs=pl.BlockSpec((seq, hd), lambda b, h: (b, h)),
        out_shape=jax.ShapeDtypeStruct((n, RET_WIDTH), BF16),
        compiler_params=_params("parallel", "parallel"),
        name="retention",
    )(proj, proj, proj, proj, dmask, qdec, kdec, cdec)


def _s5_weights(lam_re, lam_im, log_step, b_re, b_im, c_re, c_im):
    hp = lax.Precision.HIGH
    step = jnp.exp(log_step)[:, None]
    mag = jnp.exp(lam_re * step)
    a_re = mag * jnp.cos(lam_im * step)
    a_im = mag * jnp.sin(lam_im * step)
    den = lam_re * lam_re + lam_im * lam_im
    nr = a_re - 1.0
    f_re = (nr * lam_re + a_im * lam_im) / den
    f_im = (a_im * lam_re - nr * lam_im) / den
    bb_re = f_re[..., None] * b_re - f_im[..., None] * b_im
    bb_im = f_re[..., None] * b_im + f_im[..., None] * b_re
    prs, pis = [jnp.ones_like(a_re)], [jnp.zeros_like(a_im)]
    for _ in range(SSM_T):
        prs.append(prs[-1] * a_re - pis[-1] * a_im)
        pis.append(prs[-2] * a_im + pis[-1] * a_re)
    pr = jnp.stack(prs)
    pi = jnp.stack(pis)
    ca_re = c_re[None] * pr[:, :, None, :] - c_im[None] * pi[:, :, None, :]
    ca_im = c_re[None] * pi[:, :, None, :] + c_im[None] * pr[:, :, None, :]
    kern = jnp.einsum('kgpm,gmq->gkpq', jnp.concatenate([ca_re[:SSM_T], -ca_im[:SSM_T]], axis=-1),
                      jnp.concatenate([bb_re, bb_im], axis=1), precision=hp)
    nj = SSM_NJ
    ks = kern.transpose(0, 3, 1, 2).reshape(nj, 128, SSM_T * SSM_GROUP)
    ro = jnp.stack([ca_re[1:], -ca_im[1:]])
    ro = ro.reshape(2, SSM_T, nj, SSM_GPB, SSM_GROUP, SSM_STATE).transpose(2, 0, 3, 5, 1, 4)
    wo = ro.reshape(nj, 2 * SSM_GPB * SSM_STATE, SSM_T * SSM_GROUP)
    rev_re = jnp.stack([prs[SSM_T - 1 - s] for s in range(SSM_T)])
    rev_im = jnp.stack([pis[SSM_T - 1 - s] for s in range(SSM_T)])
    rev_re, rev_im = rev_re[:, :, None, :], rev_im[:, :, None, :]
    bt_re, bt_im = bb_re.transpose(0, 2, 1)[None], bb_im.transpose(0, 2, 1)[None]
    wi = jnp.concatenate([rev_re * bt_re - rev_im * bt_im, rev_re * bt_im + rev_im * bt_re], axis=-1)
    wi = wi.reshape(SSM_T, nj, 128, 2 * SSM_STATE).transpose(1, 0, 2, 3).reshape(nj, SSM_T * 128, 2 * SSM_STATE)
    a_t = jnp.stack([pr[SSM_T], pi[SSM_T]]).reshape(2, nj, SSM_GPB * SSM_STATE)
    a_t = jnp.moveaxis(a_t, 0, 1)
    return ks, wo, wi, a_t


def _s5_expand_body(ks_ref, wo_ref, wi_ref, wy_ref, win_ref):
    tp = SSM_T * SSM_GROUP
    wide = SSM_T * 128
    half = SSM_GPB * SSM_STATE

    def iota(shape, dim):
        return lax.broadcasted_iota(jnp.int32, shape, dim)

    r = iota((tp, wide), 0)
    c = iota((tp, wide), 1)
    rep = jnp.where((r // SSM_GROUP == c // 128) & (r % SSM_GROUP == c % SSM_GROUP), 1.0, 0.0).astype(BF16)
    col_b = (iota((1, wide), 1) % 128) // SSM_GROUP

    row_a = iota((128, 1), 0) // SSM_GROUP
    bdk = jnp.where(row_a == col_b, _dot(ks_ref[...].astype(BF16), rep), 0.0).astype(BF16)
    for s in range(SSM_T):
        if s:
            wy_ref[s * 128:(s + 1) * 128, :s * 128] = jnp.zeros((128, s * 128), BF16)
        wy_ref[s * 128:(s + 1) * 128, s * 128:] = bdk[:, :wide - s * 128]

    row_a = (iota((2 * half, 1), 0) % half) // SSM_STATE
    ro = _dot(wo_ref[...].astype(BF16), rep)
    wy_ref[wide:, :] = jnp.where(row_a == col_b, ro, 0.0).astype(BF16)

    r = iota((2 * SSM_STATE, 2 * half), 0)
    c = iota((2 * SSM_STATE, 2 * half), 1)
    rep_in = jnp.where((r // SSM_STATE == c // half) & (r % SSM_STATE == c % SSM_STATE), 1.0, 0.0).astype(BF16)
    row_a = (iota((wide, 1), 0) % 128) // SSM_GROUP
    col_b = (iota((1, 2 * half), 1) % half) // SSM_STATE
    win = _dot(wi_ref[...].astype(BF16), rep_in)
    win_ref[...] = jnp.where(row_a == col_b, win, 0.0).astype(BF16)


def _s5_expand(ks, wo, wi):
    nj = ks.shape[0]
    wide = SSM_T * 128
    half2 = 2 * SSM_GPB * SSM_STATE

    def blk(a):
        return pl.BlockSpec((None,) + a.shape[1:], lambda j: (j, 0, 0))

    return pl.pallas_call(
        _s5_expand_body,
        grid=(nj,),
        in_specs=[blk(ks), blk(wo), blk(wi)],
        out_specs=[pl.BlockSpec((None, wide + half2, wide), lambda j: (j, 0, 0)),
                   pl.BlockSpec((None, wide, half2), lambda j: (j, 0, 0))],
        out_shape=[jax.ShapeDtypeStruct((nj, wide + half2, wide), BF16),
                   jax.ShapeDtypeStruct((nj, wide, half2), BF16)],
        compiler_params=_params("parallel"),
        name="s5_expand",
    )(ks, wo, wi)


def _s5_body(u_ref, wy_ref, win_ref, at_ref, y_ref, s_scr, hp_scr, *, nb, nchunk):
    half = SSM_GPB * SSM_STATE
    r = nb * nchunk
    u = jnp.concatenate([u_ref[pl.ds(s, r, stride=SSM_T), :] for s in range(SSM_T)],
                        axis=1).astype(BF16)
    nslab = half // 128
    s_all = _dot(u, win_ref[...])
    for k in range(nslab):
        s_scr[k, 0:r, :] = s_all[:, k * 128:(k + 1) * 128]
        s_scr[k, r:2 * r, :] = s_all[:, half + k * 128:half + (k + 1) * 128]
    is_re = lax.broadcasted_iota(jnp.int32, (2 * nb, 128), 0) < nb
    coef_same = [jnp.broadcast_to(at_ref[0:1, k * 128:(k + 1) * 128], (2 * nb, 128)) for k in range(nslab)]
    coef_swap = [jnp.where(is_re, -at_ref[1:2, k * 128:(k + 1) * 128], at_ref[1:2, k * 128:(k + 1) * 128])
                 for k in range(nslab)]
    h = [jnp.zeros((2 * nb, 128), F32)] * nslab
    for c in range(nchunk):
        chunk_rows = pl.ds(c, 2 * nb, stride=nchunk)
        for k in range(nslab):
            hp_scr[k, chunk_rows, :] = h[k]
            h[k] = (coef_same[k] * h[k] + coef_swap[k] * pltpu.roll(h[k], nb, axis=0)
                    + s_scr[k, chunk_rows, :])
    hp = jnp.concatenate([hp_scr[k, 0:r, :] for k in range(nslab)]
                         + [hp_scr[k, r:2 * r, :] for k in range(nslab)], axis=1).astype(BF16)
    wide = SSM_T * 128
    cb = 256
    for lo in range(0, wide, cb):
        y = (_dot(u[:, :lo + cb], wy_ref[:lo + cb, lo:lo + cb])
             + _dot(hp, wy_ref[wide:, lo:lo + cb]))
        for t in range(lo // 128, (lo + cb) // 128):
            y_ref[pl.ds(t, r, stride=SSM_T), :] = y[:, t * 128 - lo:(t + 1) * 128 - lo]


def _s5(proj, w_y, w_in, a_t, batch, seq):
    n, width = proj.shape
    nchunk = seq // SSM_T
    nb = min(4, batch)
    r = nb * nchunk
    ucol0 = (width - SSM_WIDTH) // 128
    half2 = 2 * SSM_GPB * SSM_STATE
    once = pl.Buffered(1)
    return pl.pallas_call(
        functools.partial(_s5_body, nb=nb, nchunk=nchunk),
        grid=(SSM_NJ, batch // nb),
        in_specs=[
            pl.BlockSpec((r * SSM_T, 128), lambda j, i: (i, ucol0 + j)),
            pl.BlockSpec((None, SSM_T * 128 + half2, SSM_T * 128), lambda j, i: (j, 0, 0)),
            pl.BlockSpec((None, SSM_T * 128, half2), lambda j, i: (j, 0, 0)),
            pl.BlockSpec((None, 2, half2 // 2), lambda j, i: (j, 0, 0)),
        ],
        out_specs=pl.BlockSpec((r * SSM_T, 128), lambda j, i: (i, j)),
        out_shape=jax.ShapeDtypeStruct((n, SSM_WIDTH), F32),
        scratch_shapes=[pltpu.VMEM((half2 // 256, 2 * r, 128), F32), pltpu.VMEM((half2 // 256, 2 * r, 128), F32)],
        compiler_params=_params("arbitrary", "arbitrary"),
        name="s5",
    )(proj, w_y, w_in, a_t)


def _gelu(x):
    return 0.5 * x * (1.0 + jnp.tanh(math.sqrt(2.0 / math.pi) * (x + 0.044715 * (x * x * x))))


def _ab_out_body(ya_ref, ys_ref, u_ref, d_ref, wglu_ref, bglu_ref, woa_ref, wob_ref, h_ref, o_ref):
    y = ys_ref[...] + d_ref[...] * u_ref[...]
    z = _gelu(y)
    gate = jax.nn.sigmoid(_dot(z.astype(BF16), wglu_ref[...]) + bglu_ref[...])
    yb = (z * gate).astype(BF16)
    o_ref[...] = h_ref[...] + (_dot(ya_ref[...], woa_ref[...]) + _dot(yb, wob_ref[...]))


def _ab_out(y_a, y_ssm, proj, d_skip, w_glu, b_glu, w_out, h):
    n, d = h.shape
    tm = min(512, n)
    ucol = (proj.shape[1] - SSM_WIDTH) // SSM_WIDTH
    row = lambda i: (i, 0)
    fixed = lambda i: (0, 0)
    return pl.pallas_call(
        _ab_out_body,
        grid=(n // tm,),
        in_specs=[
            pl.BlockSpec((tm, RET_WIDTH), row),
            pl.BlockSpec((tm, SSM_WIDTH), row),
            pl.BlockSpec((tm, SSM_WIDTH), lambda i: (i, ucol)),
            pl.BlockSpec((1, SSM_WIDTH), fixed),
            pl.BlockSpec((SSM_WIDTH, SSM_WIDTH), fixed),
            pl.BlockSpec((1, SSM_WIDTH), fixed),
            pl.BlockSpec((RET_WIDTH, d), fixed),
            pl.BlockSpec((SSM_WIDTH, d), lambda i: (1, 0)),
            pl.BlockSpec((tm, d), row),
        ],
        out_specs=pl.BlockSpec((tm, d), row),
        out_shape=jax.ShapeDtypeStruct((n, d), F32),
        compiler_params=_params("parallel"),
        name="ab_out",
    )(y_a, y_ssm, proj, d_skip.reshape(1, -1), w_glu, b_glu.reshape(1, -1), w_out, w_out, h)


def _att_body(lam_ref, q_ref, k_ref, v_ref, sub_ref, *rest, seq, lambda_init):
    riding = len(rest) > 1
    o_ref = rest[3 if riding else 0]
    if riding:
        _rider_body(*rest[:3], *rest[4:7])
    tq = min(ATT_QBLOCK, seq)
    dh = DIFF_HEAD_DIM
    k2 = dh ** -0.5 * math.log2(math.e)
    lam = lam_ref[0]
    neg = jnp.finfo(F32).min
    rc = lax.broadcasted_iota(jnp.int32, (tq, tq), 0) // CHUNK
    cc = lax.broadcasted_iota(jnp.int32, (tq, tq), 1) // CHUNK
    visible = rc >= cc
    for qb in range(seq // tq):
        q0 = qb * tq
        q = q_ref[q0:q0 + tq, :]
        v_diag = v_ref[q0:q0 + tq, :]
        comps = []
        for c in range(2):
            qc = q[:, c * dh:(c + 1) * dh]
            s_diag = _dot_nt(qc, k_ref[q0:q0 + tq, c * dh:(c + 1) * dh])
            s_diag = jnp.where(visible, s_diag, neg)
            m = jnp.max(s_diag, axis=-1, keepdims=True)
            if qb > 0:
                s_off = _dot_nt(qc, k_ref[0:q0, c * dh:(c + 1) * dh])
                m = jnp.maximum(m, jnp.max(s_off, axis=-1, keepdims=True))
            mk = m * k2
            p_diag = jnp.exp2(s_diag * k2 - mk)
            l = jnp.sum(p_diag, axis=-1, keepdims=True)
            acc = _dot(p_diag.astype(BF16), v_diag)
            if qb > 0:
                p_off = jnp.exp2(s_off * k2 - mk)
                l = l + jnp.sum(p_off, axis=-1, keepdims=True)
                acc = acc + _dot(p_off.astype(BF16), v_ref[0:q0, :])
            comps.append(acc / l)
        o = comps[0] - lam * comps[1]
        o = _rms(o, sub_ref[...]) * (1.0 - lambda_init)
        o_ref[q0:q0 + tq, :] = o.astype(BF16)


def _attention(qkv, lam, subln, batch, seq, lambda_init, ride):
    n = qkv.shape[0]
    w = 2 * DIFF_HEAD_DIM
    rider = _Rider(*ride, batch * DIFF_HEADS, lambda b, h: b * DIFF_HEADS + h)
    extra = rider if rider.ok else None

    def col(off):
        return pl.BlockSpec((seq, w), lambda b, h: (b, off + h))

    out = pl.pallas_call(
        functools.partial(_att_body, seq=seq, lambda_init=lambda_init),
        grid=(batch, DIFF_HEADS),
        in_specs=[
            pl.BlockSpec(memory_space=pltpu.SMEM),
            col(0), col(DIFF_HEADS), col(2 * DIFF_HEADS),
            pl.BlockSpec((1, w), lambda b, h: (0, 0)),
        ] + (extra.in_specs if extra else []),
        out_specs=[pl.BlockSpec((seq, w), lambda b, h: (b, h))] + (extra.out_specs if extra else []),
        out_shape=[jax.ShapeDtypeStruct((n, D_MODEL), BF16)] + (extra.out_shape if extra else []),
        compiler_params=_params("arbitrary", "arbitrary"),
        name="diff_attention",
    )(lam.reshape(1), qkv, qkv, qkv, subln.reshape(1, w), *(extra.args if extra else ()))
    return out[0], (tuple(out[1:]) if extra else None)


def _out_proj_body(o_ref, w_ref, h_ref, y_ref):
    y_ref[...] = h_ref[...] + _dot(o_ref[...], w_ref[...])


def _out_proj(o, w, h):
    n, d = h.shape
    tm = min(512, n)
    return pl.pallas_call(
        _out_proj_body,
        grid=(n // tm,),
        in_specs=[
            pl.BlockSpec((tm, d), lambda i: (i, 0)),
            pl.BlockSpec((d, d), lambda i: (0, 0)),
            pl.BlockSpec((tm, d), lambda i: (i, 0)),
        ],
        out_specs=pl.BlockSpec((tm, d), lambda i: (i, 0)),
        out_shape=jax.ShapeDtypeStruct((n, d), F32),
        compiler_params=_params("parallel"),
        name="out_proj",
    )(o, w, h)


def _rope_tables(seq, rot_dim, theta):
    inv = 1.0 / (theta ** (np.arange(0, rot_dim, 2, dtype=np.float64) / rot_dim))
    ang = np.arange(seq, dtype=np.float64)[:, None] * inv[None, :]
    return np.cos(ang), np.sin(ang)


def kernel(x, ffn_norm, ffn_w_gate, ffn_w_up, ffn_w_down, mix_norm, ab_w_in, ab_w_out, ssm_lambda_re, ssm_lambda_im, ssm_log_step, ssm_b_re, ssm_b_im, ssm_c_re, ssm_c_im, ssm_d, ssm_w_glu, ssm_b_glu, c_w_qkv, c_w_out, c_lambda_q1, c_lambda_k1, c_lambda_q2, c_lambda_k2, c_subln, final_norm):
    batch, seq, d = x.shape
    n = batch * seq
    h = x.reshape(n, d)
    d_ff = ffn_w_down.shape[2]
    ffn_w = (ffn_w_gate, ffn_w_up, ffn_w_down)

    def ffn(h, layer, half, weights=None, final=False):
        wg, wu, wd = weights or _cast_ffn_weights(*ffn_w, layer, half)
        return _ffn(h, ffn_norm[layer, half], wg, wu, wd, final_norm, d_ff, final=final)

    h = ffn(h, 0, 0)
    ret_cos, ret_sin = _rope_tables(seq, RET_HEAD_DIM, RET_THETA)
    k_scale = RET_HEAD_DIM ** -0.5
    cos_tab = np.stack([ret_cos, ret_cos * k_scale, np.ones_like(ret_cos)]).astype(np.float32)
    sin_tab = np.stack([ret_sin, ret_sin * k_scale, np.zeros_like(ret_sin)]).astype(np.float32)
    proj, w01 = _proj_ab(h, mix_norm[0], ab_w_in[0].astype(BF16), cos_tab, sin_tab, seq, (*ffn_w, 0, 1))
    y_a = _retention(proj, batch, seq)
    ks, wo, wi, a_t = _s5_weights(ssm_lambda_re[0], ssm_lambda_im[0], ssm_log_step[0], ssm_b_re[0],
                                  ssm_b_im[0], ssm_c_re[0], ssm_c_im[0])
    w_y, w_in = _s5_expand(ks, wo, wi)
    y_ssm = _s5(proj, w_y, w_in, a_t, batch, seq)
    h = _ab_out(y_a, y_ssm, proj, ssm_d[0], ssm_w_glu[0].astype(BF16), ssm_b_glu[0],
                ab_w_out[0].astype(BF16), h)
    h = ffn(h, 0, 1, w01)

    h = ffn(h, 1, 0)
    att_cos, att_sin = _rope_tables(seq, 2 * ROT_HALF, ROPE_THETA)
    rest = np.zeros((seq, DIFF_HEAD_DIM // 2 - ROT_HALF))
    ident = np.zeros((seq, DIFF_HEAD_DIM))
    cf = np.stack([np.concatenate([att_cos, rest + 1.0, att_cos, rest + 1.0], axis=1), ident + 1.0])
    sg = np.stack([np.concatenate([-att_sin, rest, att_sin, rest], axis=1), ident])
    cf, sg = cf.astype(np.float32), sg.astype(np.float32)
    qkv = _proj_c(h, mix_norm[1], _cast_qkv(c_w_qkv[0]), cf, sg, seq)
    lambda_init = 0.8 - 0.6 * math.exp(-0.3 * 1)
    lam = (jnp.exp(jnp.sum(c_lambda_q1[0] * c_lambda_k1[0]))
           - jnp.exp(jnp.sum(c_lambda_q2[0] * c_lambda_k2[0])) + lambda_init)
    o, w11 = _attention(qkv, lam, c_subln[0], batch, seq, lambda_init, (*ffn_w, 1, 1))
    h = _out_proj(o, c_w_out[0].astype(BF16), h)
    h = ffn(h, 1, 1, w11, final=True)
    return h.reshape(batch, seq, d)
```

```python
import functools
import math

import jax
import jax.numpy as jnp
import numpy as np
from jax import lax
from jax.experimental import pallas as pl
from jax.experimental.pallas import tpu as pltpu

F32 = jnp.float32
BF16 = jnp.bfloat16

D_MODEL = 2048
CHUNK = 64
NORM_EPS = 1e-6

RET_WIDTH = 1024
RET_HEADS = 4
RET_HEAD_DIM = 256
RET_THETA = 10000.0
RET_BLOCK = 256

SSM_WIDTH = 1024
SSM_GROUP = 16
SSM_GROUPS = 64
SSM_STATE = 64
SSM_T = 16
SSM_GPB = 128 // SSM_GROUP
SSM_NJ = SSM_WIDTH // 128

DIFF_HEAD_DIM = 128
DIFF_HEADS = 8
ROPE_THETA = 500000.0
ROT_HALF = DIFF_HEAD_DIM // 4 // 2
ATT_QBLOCK = 512

FF_TILE = 1024
NARROW_STEP = 1
CAST_ROWS = 512
PROJ_TN = 1024
VMEM_LIMIT = 58 * 1024 * 1024


def _params(*sem):
    return pltpu.CompilerParams(dimension_semantics=sem, vmem_limit_bytes=VMEM_LIMIT)


def _rms(x, g):
    y = x * lax.rsqrt(jnp.mean(x * x, axis=-1, keepdims=True) + NORM_EPS)
    return y * g


def _dot(a, b):
    return jnp.dot(a, b, preferred_element_type=F32)


def _dot_nt(a, b):
    return lax.dot_general(a, b, (((1,), (1,)), ((), ())), preferred_element_type=F32)


def _dot_tn(a, b):
    return lax.dot_general(a, b, (((0,), (0,)), ((), ())), preferred_element_type=F32)


def _cast_tiles_body(w_ref, o_ref):
    cols = w_ref.shape[1]
    for k in range(o_ref.shape[0]):
        lo = k * FF_TILE
        width = min(FF_TILE, cols - lo)
        o_ref[k, :, :width] = w_ref[:, lo:lo + width].astype(BF16)
        if width < FF_TILE:
            o_ref[k, :, width:] = jnp.zeros((o_ref.shape[1], FF_TILE - width), BF16)


def _cast_col_tiles(w, layer, half):
    r, c = w.shape[2:]
    tiles = pl.cdiv(c, FF_TILE)
    return pl.pallas_call(
        _cast_tiles_body,
        grid=(r // CAST_ROWS,),
        in_specs=[pl.BlockSpec((None, None, CAST_ROWS, c), lambda i: (layer, half, i, 0))],
        out_specs=pl.BlockSpec((tiles, CAST_ROWS, FF_TILE), lambda i: (0, i, 0)),
        out_shape=jax.ShapeDtypeStruct((tiles, r, FF_TILE), BF16),
        compiler_params=_params("parallel"),
        name="cast_col_tiles",
    )(w)


def _cast_row_tiles_body(w_ref, o_ref, *, rows):
    valid = rows - pl.program_id(0) * FF_TILE
    row = lax.broadcasted_iota(jnp.int32, w_ref.shape, 0)
    o_ref[...] = jnp.where(row < valid, w_ref[...], 0.0).astype(BF16)


def _cast_row_tiles(w, layer, half):
    r, c = w.shape[2:]
    tiles = pl.cdiv(r, FF_TILE)
    return pl.pallas_call(
        functools.partial(_cast_row_tiles_body, rows=r),
        grid=(tiles,),
        in_specs=[pl.BlockSpec((None, None, FF_TILE, c), lambda k: (layer, half, k, 0))],
        out_specs=pl.BlockSpec((None, FF_TILE, c), lambda k: (k, 0, 0)),
        out_shape=jax.ShapeDtypeStruct((tiles, FF_TILE, c), BF16),
        compiler_params=_params("parallel"),
        name="cast_row_tiles",
    )(w)


def _cast_ffn_weights(w_gate, w_up, w_down, layer, half):
    return (_cast_col_tiles(w_gate, layer, half), _cast_col_tiles(w_up, layer, half),
            _cast_row_tiles(w_down, layer, half))


class _Rider:
    def __init__(self, w_gate, w_up, w_down, layer, half, steps, step_of):
        r, c = w_gate.shape[2:]
        rows_dn, d = w_down.shape[2:]
        tiles = pl.cdiv(c, FF_TILE)
        sub = 16
        col_rows = sub * pl.cdiv(r // sub, steps)
        tiles_dn = pl.cdiv(rows_dn, FF_TILE)
        row_rows = next((k for k in (64, 128, 256, 512, FF_TILE)
                         if FF_TILE % k == 0 and rows_dn % k == 0 and tiles_dn * FF_TILE // k <= steps), None)
        self.ok = r % col_rows == 0 and row_rows is not None
        if not self.ok:
            return
        ncol, per_tile = r // col_rows, FF_TILE // row_rows
        self.live_slabs = rows_dn // row_rows
        all_slabs = tiles_dn * per_tile

        def col_slab(*g):
            return jnp.minimum(step_of(*g), ncol - 1)

        def row_slab(*g):
            return jnp.minimum(step_of(*g), all_slabs - 1)

        col_in = pl.BlockSpec((None, None, col_rows, c), lambda *g: (layer, half, col_slab(*g), 0))
        self.args = (w_gate, w_up, w_down)
        self.in_specs = [col_in, col_in,
                         pl.BlockSpec((None, None, row_rows, d),
                                      lambda *g: (layer, half, jnp.minimum(row_slab(*g), self.live_slabs - 1), 0))]
        col_out = pl.BlockSpec((tiles, col_rows, FF_TILE), lambda *g: (0, col_slab(*g), 0))
        self.out_specs = [col_out, col_out,
                          pl.BlockSpec((None, row_rows, d),
                                       lambda *g: (row_slab(*g) // per_tile, row_slab(*g) % per_tile, 0))]
        self.out_shape = [jax.ShapeDtypeStruct((tiles, r, FF_TILE), BF16)] * 2 + [
            jax.ShapeDtypeStruct((tiles_dn, FF_TILE, d), BF16)]


def _rider_body(gate_ref, up_ref, down_ref, gate_out, up_out, down_out, *, live_slabs):
    step = pl.program_id(0) * pl.num_programs(1) + pl.program_id(1)
    _cast_tiles_body(gate_ref, gate_out)
    _cast_tiles_body(up_ref, up_out)
    down_out[...] = jnp.where(step < live_slabs, down_ref[...], 0.0).astype(BF16)


def _ffn_body(x_ref, g_ref, wg_ref, wu_ref, wd_ref, wgn_ref, wun_ref, wdn_ref, fn_ref, o_ref, xn_ref, *, final):
    j = pl.program_id(1)
    last = pl.num_programs(1) - 1

    def tile(wg, wu, wd):
        xn = xn_ref[...]
        gate = _dot(xn, wg[...])
        up = _dot(xn, wu[...])
        act = (gate * jax.nn.sigmoid(gate) * up).astype(BF16)
        return _dot(act, wd[...])

    @pl.when(j == 0)
    def _():
        xn_ref[...] = _rms(x_ref[...], g_ref[...]).astype(BF16)
        o_ref[...] = tile(wg_ref, wu_ref, wd_ref)

    @pl.when(j == NARROW_STEP)
    def _():
        o_ref[...] += tile(wgn_ref, wun_ref, wdn_ref)

    @pl.when((j > 0) & (j < last) & (j != NARROW_STEP))
    def _():
        o_ref[...] += tile(wg_ref, wu_ref, wd_ref)

    @pl.when(j == last)
    def _():
        h = x_ref[...] + 0.5 * (o_ref[...] + tile(wg_ref, wu_ref, wd_ref))
        if final:
            h = _rms(h, fn_ref[...])
        o_ref[...] = h


def _ffn(h, norm_g, wg, wu, wd, final_g, ff, *, final):
    n, d = h.shape
    tm = min(512, n)
    tiles = wg.shape[0]
    narrow = ff - (tiles - 1) * FF_TILE
    assert 0 < NARROW_STEP < tiles - 1 and narrow % 128 == 0

    def streamed(j):
        return jnp.where(j <= NARROW_STEP, j, j - 1)

    once = pl.Buffered(1)
    return pl.pallas_call(
        functools.partial(_ffn_body, final=final),
        grid=(n // tm, tiles),
        in_specs=[
            pl.BlockSpec((tm, d), lambda i, j: (i, 0)),
            pl.BlockSpec((1, d), lambda i, j: (0, 0)),
            pl.BlockSpec((None, d, FF_TILE), lambda i, j: (streamed(j), 0, 0)),
            pl.BlockSpec((None, d, FF_TILE), lambda i, j: (streamed(j), 0, 0)),
            pl.BlockSpec((None, FF_TILE, d), lambda i, j: (streamed(j), 0, 0)),
            pl.BlockSpec((None, d, narrow), lambda i, j: (tiles - 1, 0, 0), pipeline_mode=once),
            pl.BlockSpec((None, d, narrow), lambda i, j: (tiles - 1, 0, 0), pipeline_mode=once),
            pl.BlockSpec((None, narrow, d), lambda i, j: (tiles - 1, 0, 0), pipeline_mode=once),
            pl.BlockSpec((1, d), lambda i, j: (0, 0)),
        ],
        out_specs=pl.BlockSpec((tm, d), lambda i, j: (i, 0)),
        out_shape=jax.ShapeDtypeStruct((n, d), F32),
        scratch_shapes=[pltpu.VMEM((tm, d), BF16)],
        compiler_params=_params("parallel", "arbitrary"),
        name="ffn",
    )(h, norm_g.reshape(1, d), wg, wu, wd, wg, wu, wd, final_g.reshape(1, d))


def _proj_ab_body(x_ref, g_ref, w_ref, cos_ref, sin_ref, *rest, live_slabs):
    riding = len(rest) > 2
    o_ref, xn_ref = rest[3 if riding else 0], rest[-1]
    if riding:
        _rider_body(*rest[:3], *rest[4:7], live_slabs=live_slabs)

    @pl.when(pl.program_id(1) == 0)
    def _():
        xn_ref[...] = _rms(x_ref[...], g_ref[...]).astype(BF16)

    col = pl.multiple_of(pl.program_id(1) * PROJ_TN, PROJ_TN)
    y = _dot(xn_ref[...], w_ref[:, pl.ds(col, PROJ_TN)])
    c = cos_ref[...]
    s = sin_ref[...]
    half = RET_HEAD_DIM // 2
    parts = []
    for hd in range(PROJ_TN // RET_HEAD_DIM):
        lo = hd * RET_HEAD_DIM
        x1 = y[:, lo:lo + half]
        x2 = y[:, lo + half:lo + 2 * half]
        parts += [x1 * c - x2 * s, x2 * c + x1 * s]
    o_ref[...] = jnp.concatenate(parts, axis=1)


def _proj_ab(h, norm_g, w, cos_tab, sin_tab, seq, ride):
    n, d = h.shape
    nout = w.shape[1]
    tm = min(1024, seq)
    per_seq = seq // tm
    ncols = nout // PROJ_TN
    rider = _Rider(*ride, (n // tm) * ncols, lambda i, j: i * ncols + j)
    extra = rider if rider.ok else None
    tab = pl.BlockSpec((None, tm, RET_HEAD_DIM // 2), lambda i, j: (jnp.minimum(j, 2), i % per_seq, 0))
    out = pl.pallas_call(
        functools.partial(_proj_ab_body, live_slabs=extra.live_slabs if extra else 0),
        grid=(n // tm, nout // PROJ_TN),
        in_specs=[
            pl.BlockSpec((tm, d), lambda i, j: (i, 0)),
            pl.BlockSpec((1, d), lambda i, j: (0, 0)),
            pl.BlockSpec((d, nout), lambda i, j: (0, 0), pipeline_mode=pl.Buffered(1)),
            tab, tab,
        ] + (extra.in_specs if extra else []),
        out_specs=[pl.BlockSpec((tm, PROJ_TN), lambda i, j: (i, j))] + (extra.out_specs if extra else []),
        out_shape=[jax.ShapeDtypeStruct((n, nout), F32)] + (extra.out_shape if extra else []),
        scratch_shapes=[pltpu.VMEM((tm, d), BF16)],
        compiler_params=_params("arbitrary", "arbitrary"),
        name="proj_ab",
    )(h, norm_g.reshape(1, d), w, cos_tab, sin_tab, *(extra.args if extra else ()))
    return out[0], (tuple(out[1:]) if extra else None)


def _cast_qkv_body(w_ref, o_ref):
    dh = DIFF_HEAD_DIM
    old = lax.broadcasted_iota(jnp.int32, (dh, dh), 0)
    new = lax.broadcasted_iota(jnp.int32, (dh, dh), 1)
    src = jnp.where(new < ROT_HALF, new,
                    jnp.where(new < dh // 2, new + ROT_HALF,
                              jnp.where(new < dh // 2 + ROT_HALF, new - (dh // 2 - ROT_HALF), new)))
    src = jnp.where(pl.program_id(1) < 2 * D_MODEL // PROJ_TN, src, new)
    perm = jnp.where(old == src, 1.0, 0.0).astype(BF16)
    w = w_ref[...].astype(BF16)
    parts = [_dot(w[:, hd * dh:(hd + 1) * dh], perm) for hd in range(PROJ_TN // dh)]
    o_ref[...] = jnp.concatenate(parts, axis=1).astype(BF16)


def _cast_qkv(w):
    r, c = w.shape
    spec = pl.BlockSpec((CAST_ROWS, PROJ_TN), lambda i, j: (i, j))
    return pl.pallas_call(
        _cast_qkv_body,
        grid=(r // CAST_ROWS, c // PROJ_TN),
        in_specs=[spec],
        out_specs=spec,
        out_shape=jax.ShapeDtypeStruct(w.shape, BF16),
        compiler_params=_params("parallel", "parallel"),
        name="cast_qkv",
    )(w)


def _proj_c_body(x_ref, g_ref, w_ref, cf_ref, sg_ref, o_ref, xn_ref):
    def column_tile(w_tile):
        y = _dot(xn_ref[...], w_tile)
        cf = cf_ref[...]
        sg = sg_ref[...]
        parts = []
        for hd in range(PROJ_TN // DIFF_HEAD_DIM):
            lo = hd * DIFF_HEAD_DIM
            yh = y[:, lo:lo + DIFF_HEAD_DIM]
            r = yh * cf + pltpu.roll(yh, DIFF_HEAD_DIM // 2, axis=1) * sg
            parts.append(r.astype(BF16))
        o_ref[...] = jnp.concatenate(parts, axis=1)

    @pl.when(pl.program_id(1) == 0)
    def _():
        xn_ref[...] = _rms(x_ref[...], g_ref[...]).astype(BF16)
        column_tile(w_ref[:, 0:PROJ_TN])

    @pl.when(pl.program_id(1) > 0)
    def _():
        col = pl.multiple_of(pl.program_id(1) * PROJ_TN, PROJ_TN)
        column_tile(w_ref[:, pl.ds(col, PROJ_TN)])


def _proj_c(h, norm_g, w, cf, sg, seq):
    n, d = h.shape
    nout = w.shape[1]
    tm = min(1024, seq)
    per_seq = seq // tm
    qk_tiles = 2 * D_MODEL // PROJ_TN
    tab = pl.BlockSpec((None, tm, DIFF_HEAD_DIM), lambda i, j: (j // qk_tiles, i % per_seq, 0))
    return pl.pallas_call(
        _proj_c_body,
        grid=(n // tm, nout // PROJ_TN),
        in_specs=[
            pl.BlockSpec((tm, d), lambda i, j: (i, 0)),
            pl.BlockSpec((1, d), lambda i, j: (0, 0)),
            pl.BlockSpec((d, nout), lambda i, j: (0, 0), pipeline_mode=pl.Buffered(1)),
            tab, tab,
        ],
        out_specs=pl.BlockSpec((tm, PROJ_TN), lambda i, j: (i, j)),
        out_shape=jax.ShapeDtypeStruct((n, nout), BF16),
        scratch_shapes=[pltpu.VMEM((tm, d), BF16)],
        compiler_params=_params("parallel", "arbitrary"),
        name="proj_c",
    )(h, norm_g.reshape(1, d), w, cf, sg)


def _ret_body(q_ref, k_ref, v_ref, g_ref, dm_ref, qd_ref, kd_ref, cd_ref, o_ref, *, nblk):
    t = RET_BLOCK
    dmask = dm_ref[...]
    qdec = qd_ref[...]
    kdec = kd_ref[...]
    cdec = cd_ref[...]

    def rows(ref, n):
        return ref[pl.ds(pl.multiple_of(n * t, t), t), :]

    def step(n, state):
        q = rows(q_ref, n)
        k = rows(k_ref, n)
        vb = rows(v_ref, n).astype(BF16)
        scores = _dot_nt(q.astype(BF16), k.astype(BF16)) * dmask
        intra = _dot(scores.astype(BF16), vb)
        inter = _dot((q * qdec).astype(BF16), state.astype(BF16))
        new_state = state * cdec + _dot_tn((k * kdec).astype(BF16), vb)
        o = intra + inter
        o = o * lax.rsqrt(jnp.mean(o * o, axis=-1, keepdims=True) + NORM_EPS)
        gate = rows(g_ref, n)
        r0 = pl.multiple_of(n * t, t)
        o_ref[pl.ds(r0, t), :] = (o * (gate * jax.nn.sigmoid(gate))).astype(BF16)
        return new_state

    lax.fori_loop(0, nblk, step, jnp.zeros((RET_HEAD_DIM, RET_HEAD_DIM), F32), unroll=True)


def _retention(proj, batch, seq):
    n = proj.shape[0]
    t = RET_BLOCK
    hd = RET_HEAD_DIM
    log_g = np.log(1.0 - 2.0 ** (-5.0 - np.arange(RET_HEADS, dtype=np.float64)))
    idx = np.arange(t, dtype=np.float64)
    chunk = np.arange(t) // CHUNK
    visible = chunk[:, None] >= chunk[None, :]
    dist = np.abs(idx[:, None] - idx[None, :])
    dmask = np.where(visible[None], np.exp(log_g[:, None, None] * dist[None]), 0.0)
    f32 = np.float32
    dmask = dmask.astype(f32)
    qdec = np.broadcast_to(np.exp(log_g[:, None] * (idx[None] + 1.0))[:, :, None], (RET_HEADS, t, hd)).astype(f32)
    kdec = np.broadcast_to(np.exp(log_g[:, None] * (t - 1.0 - idx[None]))[:, :, None], (RET_HEADS, t, hd)).astype(f32)
    cdec = np.broadcast_to(np.exp(log_g * t)[:, None, None], (RET_HEADS, 1, hd)).astype(f32)

    def col(off):
        return pl.BlockSpec((seq, hd), lambda b, h: (b, off + h))

    def tab(rows, cols):
        return pl.BlockSpec((None, rows, cols), lambda b, h: (h, 0, 0))

    return pl.pallas_call(
        functools.partial(_ret_body, nblk=seq // t),
        grid=(batch, RET_HEADS),
        in_specs=[col(0), col(RET_HEADS), col(2 * RET_HEADS), col(3 * RET_HEADS),
                  tab(t, t), tab(t, hd), tab(t, hd), tab(1, hd)],
        out_specs=pl.BlockSpec((seq, hd), lambda b, h: (b, h)),
        out_shape=jax.ShapeDtypeStruct((n, RET_WIDTH), BF16),
        compiler_params=_params("parallel", "parallel"),
        name="retention",
    )(proj, proj, proj, proj, dmask, qdec, kdec, cdec)


def _s5_weights(lam_re, lam_im, log_step, b_re, b_im, c_re, c_im):
    hp = lax.Precision.HIGH
    step = jnp.exp(log_step)[:, None]
    mag = jnp.exp(lam_re * step)
    a_re = mag * jnp.cos(lam_im * step)
    a_im = mag * jnp.sin(lam_im * step)
    den = lam_re * lam_re + lam_im * lam_im
    nr = a_re - 1.0
    f_re = (nr * lam_re + a_im * lam_im) / den
    f_im = (a_im * lam_re - nr * lam_im) / den
    bb_re = f_re[..., None] * b_re - f_im[..., None] * b_im
    bb_im = f_re[..., None] * b_im + f_im[..., None] * b_re
    prs, pis = [jnp.ones_like(a_re)], [jnp.zeros_like(a_im)]
    for _ in range(SSM_T):
        prs.append(prs[-1] * a_re - pis[-1] * a_im)
        pis.append(prs[-2] * a_im + pis[-1] * a_re)
    pr = jnp.stack(prs)
    pi = jnp.stack(pis)
    ca_re = c_re[None] * pr[:, :, None, :] - c_im[None] * pi[:, :, None, :]
    ca_im = c_re[None] * pi[:, :, None, :] + c_im[None] * pr[:, :, None, :]
    kern = jnp.einsum('kgpm,gmq->gkpq', jnp.concatenate([ca_re[:SSM_T], -ca_im[:SSM_T]], axis=-1),
                      jnp.concatenate([bb_re, bb_im], axis=1), precision=hp)
    nj = SSM_NJ
    ks = kern.transpose(0, 3, 1, 2).reshape(nj, 128, SSM_T * SSM_GROUP)
    ro = jnp.stack([ca_re[1:], -ca_im[1:]])
    ro = ro.reshape(2, SSM_T, nj, SSM_GPB, SSM_GROUP, SSM_STATE).transpose(2, 0, 3, 5, 1, 4)
    wo = ro.reshape(nj, 2 * SSM_GPB * SSM_STATE, SSM_T * SSM_GROUP)
    rev_re = jnp.stack([prs[SSM_T - 1 - s] for s in range(SSM_T)])
    rev_im = jnp.stack([pis[SSM_T - 1 - s] for s in range(SSM_T)])
    rev_re, rev_im = rev_re[:, :, None, :], rev_im[:, :, None, :]
    bt_re, bt_im = bb_re.transpose(0, 2, 1)[None], bb_im.transpose(0, 2, 1)[None]
    wi = jnp.concatenate([rev_re * bt_re - rev_im * bt_im, rev_re * bt_im + rev_im * bt_re], axis=-1)
    wi = wi.reshape(SSM_T, nj, 128, 2 * SSM_STATE).transpose(1, 0, 2, 3).reshape(nj, SSM_T * 128, 2 * SSM_STATE)
    a_t = jnp.stack([pr[SSM_T], pi[SSM_T]]).reshape(2, nj, SSM_GPB * SSM_STATE)
    a_t = jnp.moveaxis(a_t, 0, 1)
    return ks, wo, wi, a_t


def _s5_expand_body(ks_ref, wo_ref, wi_ref, wy_ref, win_ref):
    tp = SSM_T * SSM_GROUP
    wide = SSM_T * 128
    half = SSM_GPB * SSM_STATE

    def iota(shape, dim):
        return lax.broadcasted_iota(jnp.int32, shape, dim)

    r = iota((tp, wide), 0)
    c = iota((tp, wide), 1)
    rep = jnp.where((r // SSM_GROUP == c // 128) & (r % SSM_GROUP == c % SSM_GROUP), 1.0, 0.0).astype(BF16)
    col_b = (iota((1, wide), 1) % 128) // SSM_GROUP

    row_a = iota((128, 1), 0) // SSM_GROUP
    bdk = jnp.where(row_a == col_b, _dot(ks_ref[...].astype(BF16), rep), 0.0).astype(BF16)
    for s in range(SSM_T):
        if s:
            wy_ref[s * 128:(s + 1) * 128, :s * 128] = jnp.zeros((128, s * 128), BF16)
        wy_ref[s * 128:(s + 1) * 128, s * 128:] = bdk[:, :wide - s * 128]

    row_a = (iota((2 * half, 1), 0) % half) // SSM_STATE
    ro = _dot(wo_ref[...].astype(BF16), rep)
    wy_ref[wide:, :] = jnp.where(row_a == col_b, ro, 0.0).astype(BF16)

    r = iota((2 * SSM_STATE, 2 * half), 0)
    c = iota((2 * SSM_STATE, 2 * half), 1)
    rep_in = jnp.where((r // SSM_STATE == c // half) & (r % SSM_STATE == c % SSM_STATE), 1.0, 0.0).astype(BF16)
    row_a = (iota((wide, 1), 0) % 128) // SSM_GROUP
    col_b = (iota((1, 2 * half), 1) % half) // SSM_STATE
    win = _dot(wi_ref[...].astype(BF16), rep_in)
    win_ref[...] = jnp.where(row_a == col_b, win, 0.0).astype(BF16)


def _s5_expand(ks, wo, wi):
    nj = ks.shape[0]
    wide = SSM_T * 128
    half2 = 2 * SSM_GPB * SSM_STATE

    def blk(a):
        return pl.BlockSpec((None,) + a.shape[1:], lambda j: (j, 0, 0))

    return pl.pallas_call(
        _s5_expand_body,
        grid=(nj,),
        in_specs=[blk(ks), blk(wo), blk(wi)],
        out_specs=[pl.BlockSpec((None, wide + half2, wide), lambda j: (j, 0, 0)),
                   pl.BlockSpec((None, wide, half2), lambda j: (j, 0, 0))],
        out_shape=[jax.ShapeDtypeStruct((nj, wide + half2, wide), BF16),
                   jax.ShapeDtypeStruct((nj, wide, half2), BF16)],
        compiler_params=_params("parallel"),
        name="s5_expand",
    )(ks, wo, wi)


def _s5_body(u_ref, wy_ref, win_ref, at_ref, y_ref, s_scr, hp_scr, *, nb, nchunk):
    half = SSM_GPB * SSM_STATE
    r = nb * nchunk
    u = jnp.concatenate([u_ref[pl.ds(s, r, stride=SSM_T), :] for s in range(SSM_T)],
                        axis=1).astype(BF16)
    nslab = half // 128
    s_all = _dot(u, win_ref[...])
    for k in range(nslab):
        s_scr[k, 0:r, :] = s_all[:, k * 128:(k + 1) * 128]
        s_scr[k, r:2 * r, :] = s_all[:, half + k * 128:half + (k + 1) * 128]
    is_re = lax.broadcasted_iota(jnp.int32, (2 * nb, 128), 0) < nb
    coef_same = [jnp.broadcast_to(at_ref[0:1, k * 128:(k + 1) * 128], (2 * nb, 128)) for k in range(nslab)]
    coef_swap = [jnp.where(is_re, -at_ref[1:2, k * 128:(k + 1) * 128], at_ref[1:2, k * 128:(k + 1) * 128])
                 for k in range(nslab)]
    h = [jnp.zeros((2 * nb, 128), F32)] * nslab
    for c in range(nchunk):
        chunk_rows = pl.ds(c, 2 * nb, stride=nchunk)
        for k in range(nslab):
            hp_scr[k, chunk_rows, :] = h[k]
            h[k] = (coef_same[k] * h[k] + coef_swap[k] * pltpu.roll(h[k], nb, axis=0)
                    + s_scr[k, chunk_rows, :])
    hp = jnp.concatenate([hp_scr[k, 0:r, :] for k in range(nslab)]
                         + [hp_scr[k, r:2 * r, :] for k in range(nslab)], axis=1).astype(BF16)
    wide = SSM_T * 128
    cb = 256
    for lo in range(0, wide, cb):
        y = (_dot(u[:, :lo + cb], wy_ref[:lo + cb, lo:lo + cb])
             + _dot(hp, wy_ref[wide:, lo:lo + cb]))
        for t in range(lo // 128, (lo + cb) // 128):
            y_ref[pl.ds(t, r, stride=SSM_T), :] = y[:, t * 128 - lo:(t + 1) * 128 - lo]


def _s5(proj, w_y, w_in, a_t, batch, seq):
    n, width = proj.shape
    nchunk = seq // SSM_T
    nb = min(4, batch)
    r = nb * nchunk
    ucol0 = (width - SSM_WIDTH) // 128
    half2 = 2 * SSM_GPB * SSM_STATE
    once = pl.Buffered(1)
    return pl.pallas_call(
        functools.partial(_s5_body, nb=nb, nchunk=nchunk),
        grid=(SSM_NJ, batch // nb),
        in_specs=[
            pl.BlockSpec((r * SSM_T, 128), lambda j, i: (i, ucol0 + j)),
            pl.BlockSpec((None, SSM_T * 128 + half2, SSM_T * 128), lambda j, i: (j, 0, 0)),
            pl.BlockSpec((None, SSM_T * 128, half2), lambda j, i: (j, 0, 0)),
            pl.BlockSpec((None, 2, half2 // 2), lambda j, i: (j, 0, 0)),
        ],
        out_specs=pl.BlockSpec((r * SSM_T, 128), lambda j, i: (i, j)),
        out_shape=jax.ShapeDtypeStruct((n, SSM_WIDTH), F32),
        scratch_shapes=[pltpu.VMEM((half2 // 256, 2 * r, 128), F32), pltpu.VMEM((half2 // 256, 2 * r, 128), F32)],
        compiler_params=_params("arbitrary", "arbitrary"),
        name="s5",
    )(proj, w_y, w_in, a_t)


def _gelu(x):
    return 0.5 * x * (1.0 + jnp.tanh(math.sqrt(2.0 / math.pi) * (x + 0.044715 * (x * x * x))))


def _ab_out_body(ya_ref, ys_ref, u_ref, d_ref, wglu_ref, bglu_ref, woa_ref, wob_ref, h_ref, o_ref):
    y = ys_ref[...] + d_ref[...] * u_ref[...]
    z = _gelu(y)
    gate = jax.nn.sigmoid(_dot(z.astype(BF16), wglu_ref[...]) + bglu_ref[...])
    yb = (z * gate).astype(BF16)
    o_ref[...] = h_ref[...] + (_dot(ya_ref[...], woa_ref[...]) + _dot(yb, wob_ref[...]))


def _ab_out(y_a, y_ssm, proj, d_skip, w_glu, b_glu, w_out, h):
    n, d = h.shape
    tm = min(512, n)
    ucol = (proj.shape[1] - SSM_WIDTH) // SSM_WIDTH
    row = lambda i: (i, 0)
    fixed = lambda i: (0, 0)
    return pl.pallas_call(
        _ab_out_body,
        grid=(n // tm,),
        in_specs=[
            pl.BlockSpec((tm, RET_WIDTH), row),
            pl.BlockSpec((tm, SSM_WIDTH), row),
            pl.BlockSpec((tm, SSM_WIDTH), lambda i: (i, ucol)),
            pl.BlockSpec((1, SSM_WIDTH), fixed),
            pl.BlockSpec((SSM_WIDTH, SSM_WIDTH), fixed),
            pl.BlockSpec((1, SSM_WIDTH), fixed),
            pl.BlockSpec((RET_WIDTH, d), fixed),
            pl.BlockSpec((SSM_WIDTH, d), lambda i: (1, 0)),
            pl.BlockSpec((tm, d), row),
        ],
        out_specs=pl.BlockSpec((tm, d), row),
        out_shape=jax.ShapeDtypeStruct((n, d), F32),
        compiler_params=_params("parallel"),
        name="ab_out",
    )(y_a, y_ssm, proj, d_skip.reshape(1, -1), w_glu, b_glu.reshape(1, -1), w_out, w_out, h)


def _att_body(lam_ref, q_ref, k_ref, v_ref, sub_ref, *rest, seq, lambda_init, live_slabs):
    riding = len(rest) > 1
    o_ref = rest[3 if riding else 0]
    if riding:
        _rider_body(*rest[:3], *rest[4:7], live_slabs=live_slabs)
    tq = min(ATT_QBLOCK, seq)
    dh = DIFF_HEAD_DIM
    k2 = dh ** -0.5 * math.log2(math.e)
    lam = lam_ref[0]
    neg = jnp.finfo(F32).min
    rc = lax.broadcasted_iota(jnp.int32, (tq, tq), 0) // CHUNK
    cc = lax.broadcasted_iota(jnp.int32, (tq, tq), 1) // CHUNK
    visible = rc >= cc
    for qb in range(seq // tq):
        q0 = qb * tq
        q = q_ref[q0:q0 + tq, :]
        v_diag = v_ref[q0:q0 + tq, :]
        comps = []
        for c in range(2):
            qc = q[:, c * dh:(c + 1) * dh]
            s_diag = _dot_nt(qc, k_ref[q0:q0 + tq, c * dh:(c + 1) * dh])
            s_diag = jnp.where(visible, s_diag, neg)
            m = jnp.max(s_diag, axis=-1, keepdims=True)
            if qb > 0:
                s_off = _dot_nt(qc, k_ref[0:q0, c * dh:(c + 1) * dh])
                m = jnp.maximum(m, jnp.max(s_off, axis=-1, keepdims=True))
            mk = m * k2
            p_diag = jnp.exp2(s_diag * k2 - mk)
            l = jnp.sum(p_diag, axis=-1, keepdims=True)
            acc = _dot(p_diag.astype(BF16), v_diag)
            if qb > 0:
                p_off = jnp.exp2(s_off * k2 - mk)
                l = l + jnp.sum(p_off, axis=-1, keepdims=True)
                acc = acc + _dot(p_off.astype(BF16), v_ref[0:q0, :])
            comps.append(acc / l)
        o = comps[0] - lam * comps[1]
        o = _rms(o, sub_ref[...]) * (1.0 - lambda_init)
        o_ref[q0:q0 + tq, :] = o.astype(BF16)


def _attention(qkv, lam, subln, batch, seq, lambda_init, ride):
    n = qkv.shape[0]
    w = 2 * DIFF_HEAD_DIM
    rider = _Rider(*ride, batch * DIFF_HEADS, lambda b, h: b * DIFF_HEADS + h)
    extra = rider if rider.ok else None

    def col(off):
        return pl.BlockSpec((seq, w), lambda b, h: (b, off + h))

    out = pl.pallas_call(
        functools.partial(_att_body, seq=seq, lambda_init=lambda_init,
                          live_slabs=extra.live_slabs if extra else 0),
        grid=(batch, DIFF_HEADS),
        in_specs=[
            pl.BlockSpec(memory_space=pltpu.SMEM),
            col(0), col(DIFF_HEADS), col(2 * DIFF_HEADS),
            pl.BlockSpec((1, w), lambda b, h: (0, 0)),
        ] + (extra.in_specs if extra else []),
        out_specs=[pl.BlockSpec((seq, w), lambda b, h: (b, h))] + (extra.out_specs if extra else []),
        out_shape=[jax.ShapeDtypeStruct((n, D_MODEL), BF16)] + (extra.out_shape if extra else []),
        compiler_params=_params("arbitrary", "arbitrary"),
        name="diff_attention",
    )(lam.reshape(1), qkv, qkv, qkv, subln.reshape(1, w), *(extra.args if extra else ()))
    return out[0], (tuple(out[1:]) if extra else None)


def _out_proj_body(o_ref, w_ref, h_ref, y_ref):
    y_ref[...] = h_ref[...] + _dot(o_ref[...], w_ref[...])


def _out_proj(o, w, h):
    n, d = h.shape
    tm = min(512, n)
    return pl.pallas_call(
        _out_proj_body,
        grid=(n // tm,),
        in_specs=[
            pl.BlockSpec((tm, d), lambda i: (i, 0)),
            pl.BlockSpec((d, d), lambda i: (0, 0)),
            pl.BlockSpec((tm, d), lambda i: (i, 0)),
        ],
        out_specs=pl.BlockSpec((tm, d), lambda i: (i, 0)),
        out_shape=jax.ShapeDtypeStruct((n, d), F32),
        compiler_params=_params("parallel"),
        name="out_proj",
    )(o, w, h)


def _rope_tables(seq, rot_dim, theta):
    inv = 1.0 / (theta ** (np.arange(0, rot_dim, 2, dtype=np.float64) / rot_dim))
    ang = np.arange(seq, dtype=np.float64)[:, None] * inv[None, :]
    return np.cos(ang), np.sin(ang)


def kernel(x, ffn_norm, ffn_w_gate, ffn_w_up, ffn_w_down, mix_norm, ab_w_in, ab_w_out, ssm_lambda_re, ssm_lambda_im, ssm_log_step, ssm_b_re, ssm_b_im, ssm_c_re, ssm_c_im, ssm_d, ssm_w_glu, ssm_b_glu, c_w_qkv, c_w_out, c_lambda_q1, c_lambda_k1, c_lambda_q2, c_lambda_k2, c_subln, final_norm):
    batch, seq, d = x.shape
    n = batch * seq
    h = x.reshape(n, d)
    d_ff = ffn_w_down.shape[2]
    ffn_w = (ffn_w_gate, ffn_w_up, ffn_w_down)

    def ffn(h, layer, half, weights=None, final=False):
        wg, wu, wd = weights or _cast_ffn_weights(*ffn_w, layer, half)
        return _ffn(h, ffn_norm[layer, half], wg, wu, wd, final_norm, d_ff, final=final)

    h = ffn(h, 0, 0)
    ret_cos, ret_sin = _rope_tables(seq, RET_HEAD_DIM, RET_THETA)
    k_scale = RET_HEAD_DIM ** -0.5
    cos_tab = np.stack([ret_cos, ret_cos * k_scale, np.ones_like(ret_cos)]).astype(np.float32)
    sin_tab = np.stack([ret_sin, ret_sin * k_scale, np.zeros_like(ret_sin)]).astype(np.float32)
    proj, w01 = _proj_ab(h, mix_norm[0], ab_w_in[0].astype(BF16), cos_tab, sin_tab, seq, (*ffn_w, 0, 1))
    y_a = _retention(proj, batch, seq)
    ks, wo, wi, a_t = _s5_weights(ssm_lambda_re[0], ssm_lambda_im[0], ssm_log_step[0], ssm_b_re[0],
                                  ssm_b_im[0], ssm_c_re[0], ssm_c_im[0])
    w_y, w_in = _s5_expand(ks, wo, wi)
    y_ssm = _s5(proj, w_y, w_in, a_t, batch, seq)
    h = _ab_out(y_a, y_ssm, proj, ssm_d[0], ssm_w_glu[0].astype(BF16), ssm_b_glu[0],
                ab_w_out[0].astype(BF16), h)
    h = ffn(h, 0, 1, w01)

    h = ffn(h, 1, 0)
    att_cos, att_sin = _rope_tables(seq, 2 * ROT_HALF, ROPE_THETA)
    rest = np.zeros((seq, DIFF_HEAD_DIM // 2 - ROT_HALF))
    ident = np.zeros((seq, DIFF_HEAD_DIM))
    cf = np.stack([np.concatenate([att_cos, rest + 1.0, att_cos, rest + 1.0], axis=1), ident + 1.0])
    sg = np.stack([np.concatenate([-att_sin, rest, att_sin, rest], axis=1), ident])
    cf, sg = cf.astype(np.float32), sg.astype(np.float32)
    qkv = _proj_c(h, mix_norm[1], _cast_qkv(c_w_qkv[0]), cf, sg, seq)
    lambda_init = 0.8 - 0.6 * math.exp(-0.3 * 1)
    lam = (jnp.exp(jnp.sum(c_lambda_q1[0] * c_lambda_k1[0]))
           - jnp.exp(jnp.sum(c_lambda_q2[0] * c_lambda_k2[0])) + lambda_init)
    o, w11 = _attention(qkv, lam, c_subln[0], batch, seq, lambda_init, (*ffn_w, 1, 1))
    h = _out_proj(o, c_w_out[0].astype(BF16), h)
    h = ffn(h, 1, 1, w11, final=True)
    return h.reshape(batch, seq, d)
```

```python
import functools
import math

import jax
import jax.numpy as jnp
import numpy as np
from jax import lax
from jax.experimental import pallas as pl
from jax.experimental.pallas import tpu as pltpu

F32 = jnp.float32
BF16 = jnp.bfloat16

D_MODEL = 2048
CHUNK = 64
NORM_EPS = 1e-6

RET_WIDTH = 1024
RET_HEADS = 4
RET_HEAD_DIM = 256
RET_THETA = 10000.0
RET_BLOCK = 256

SSM_WIDTH = 1024
SSM_GROUP = 16
SSM_GROUPS = 64
SSM_STATE = 64
SSM_T = 16
SSM_GPB = 128 // SSM_GROUP
SSM_NJ = SSM_WIDTH // 128

DIFF_HEAD_DIM = 128
DIFF_HEADS = 8
ROPE_THETA = 500000.0
ROT_HALF = DIFF_HEAD_DIM // 4 // 2
ATT_QBLOCK = 512

FF_TILE = 1024
NARROW_STEP = 1
CAST_ROWS = 512
PROJ_TN = 1024
VMEM_LIMIT = 58 * 1024 * 1024


def _params(*sem):
    return pltpu.CompilerParams(dimension_semantics=sem, vmem_limit_bytes=VMEM_LIMIT)


def _rms(x, g):
    y = x * lax.rsqrt(jnp.mean(x * x, axis=-1, keepdims=True) + NORM_EPS)
    return y * g


def _dot(a, b):
    return jnp.dot(a, b, preferred_element_type=F32)


def _dot_nt(a, b):
    return lax.dot_general(a, b, (((1,), (1,)), ((), ())), preferred_element_type=F32)


def _dot_tn(a, b):
    return lax.dot_general(a, b, (((0,), (0,)), ((), ())), preferred_element_type=F32)


def _cast_tiles_body(w_ref, o_ref):
    cols = w_ref.shape[1]
    for k in range(o_ref.shape[0]):
        lo = k * FF_TILE
        width = min(FF_TILE, cols - lo)
        o_ref[k, :, :width] = w_ref[:, lo:lo + width].astype(BF16)
        if width < FF_TILE:
            o_ref[k, :, width:] = jnp.zeros((o_ref.shape[1], FF_TILE - width), BF16)


def _cast_col_tiles(w, layer, half):
    r, c = w.shape[2:]
    tiles = pl.cdiv(c, FF_TILE)
    return pl.pallas_call(
        _cast_tiles_body,
        grid=(r // CAST_ROWS,),
        in_specs=[pl.BlockSpec((None, None, CAST_ROWS, c), lambda i: (layer, half, i, 0))],
        out_specs=pl.BlockSpec((tiles, CAST_ROWS, FF_TILE), lambda i: (0, i, 0)),
        out_shape=jax.ShapeDtypeStruct((tiles, r, FF_TILE), BF16),
        compiler_params=_params("parallel"),
        name="cast_col_tiles",
    )(w)


def _cast_row_tiles_body(w_ref, o_ref, *, rows):
    valid = rows - pl.program_id(0) * FF_TILE
    row = lax.broadcasted_iota(jnp.int32, w_ref.shape, 0)
    o_ref[...] = jnp.where(row < valid, w_ref[...], 0.0).astype(BF16)


def _cast_row_tiles(w, layer, half):
    r, c = w.shape[2:]
    tiles = pl.cdiv(r, FF_TILE)
    return pl.pallas_call(
        functools.partial(_cast_row_tiles_body, rows=r),
        grid=(tiles,),
        in_specs=[pl.BlockSpec((None, None, FF_TILE, c), lambda k: (layer, half, k, 0))],
        out_specs=pl.BlockSpec((None, FF_TILE, c), lambda k: (k, 0, 0)),
        out_shape=jax.ShapeDtypeStruct((tiles, FF_TILE, c), BF16),
        compiler_params=_params("parallel"),
        name="cast_row_tiles",
    )(w)


def _cast_ffn_weights(w_gate, w_up, w_down, layer, half):
    return (_cast_col_tiles(w_gate, layer, half), _cast_col_tiles(w_up, layer, half),
            _cast_row_tiles(w_down, layer, half))


class _Rider:
    def __init__(self, w_gate, w_up, w_down, layer, half, steps, step_of):
        r, c = w_gate.shape[2:]
        rows_dn, d = w_down.shape[2:]
        tiles = pl.cdiv(c, FF_TILE)
        sub = 16
        col_rows = sub * pl.cdiv(r // sub, steps)
        tiles_dn = pl.cdiv(rows_dn, FF_TILE)
        row_rows = next((k for k in (64, 128, 256, 512, FF_TILE)
                         if FF_TILE % k == 0 and rows_dn % k == 0 and tiles_dn * FF_TILE // k <= steps), None)
        self.ok = r % col_rows == 0 and row_rows is not None
        if not self.ok:
            return
        ncol, per_tile = r // col_rows, FF_TILE // row_rows
        self.live_slabs = rows_dn // row_rows
        all_slabs = tiles_dn * per_tile

        def col_slab(*g):
            return jnp.minimum(step_of(*g), ncol - 1)

        def row_slab(*g):
            return jnp.minimum(step_of(*g), all_slabs - 1)

        col_in = pl.BlockSpec((None, None, col_rows, c), lambda *g: (layer, half, col_slab(*g), 0))
        self.args = (w_gate, w_up, w_down)
        self.in_specs = [col_in, col_in,
                         pl.BlockSpec((None, None, row_rows, d),
                                      lambda *g: (layer, half, jnp.minimum(row_slab(*g), self.live_slabs - 1), 0))]
        col_out = pl.BlockSpec((tiles, col_rows, FF_TILE), lambda *g: (0, col_slab(*g), 0))
        self.out_specs = [col_out, col_out,
                          pl.BlockSpec((None, row_rows, d),
                                       lambda *g: (row_slab(*g) // per_tile, row_slab(*g) % per_tile, 0))]
        self.out_shape = [jax.ShapeDtypeStruct((tiles, r, FF_TILE), BF16)] * 2 + [
            jax.ShapeDtypeStruct((tiles_dn, FF_TILE, d), BF16)]


def _rider_body(gate_ref, up_ref, down_ref, gate_out, up_out, down_out, *, live_slabs):
    step = pl.program_id(0) * pl.num_programs(1) + pl.program_id(1)
    _cast_tiles_body(gate_ref, gate_out)
    _cast_tiles_body(up_ref, up_out)
    down_out[...] = jnp.where(step < live_slabs, down_ref[...], 0.0).astype(BF16)


def _ffn_body(x_ref, g_ref, wg_ref, wu_ref, wd_ref, wgn_ref, wun_ref, wdn_ref, fn_ref, o_ref, xn_ref, *, final):
    j = pl.program_id(1)
    last = pl.num_programs(1) - 1

    def tile(wg, wu, wd):
        xn = xn_ref[...]
        gate = _dot(xn, wg[...])
        up = _dot(xn, wu[...])
        act = (gate * jax.nn.sigmoid(gate) * up).astype(BF16)
        return _dot(act, wd[...])

    @pl.when(j == 0)
    def _():
        xn_ref[...] = _rms(x_ref[...], g_ref[...]).astype(BF16)
        o_ref[...] = tile(wg_ref, wu_ref, wd_ref)

    @pl.when(j == NARROW_STEP)
    def _():
        o_ref[...] += tile(wgn_ref, wun_ref, wdn_ref)

    @pl.when((j > 0) & (j < last) & (j != NARROW_STEP))
    def _():
        o_ref[...] += tile(wg_ref, wu_ref, wd_ref)

    @pl.when(j == last)
    def _():
        h = x_ref[...] + 0.5 * (o_ref[...] + tile(wg_ref, wu_ref, wd_ref))
        if final:
            h = _rms(h, fn_ref[...])
        o_ref[...] = h


def _ffn(h, norm_g, wg, wu, wd, final_g, ff, *, final):
    n, d = h.shape
    tm = min(512, n)
    tiles = wg.shape[0]
    narrow = ff - (tiles - 1) * FF_TILE
    assert 0 < NARROW_STEP < tiles - 1 and narrow % 128 == 0

    def streamed(j):
        return jnp.where(j <= NARROW_STEP, j, j - 1)

    once = pl.Buffered(1)
    return pl.pallas_call(
        functools.partial(_ffn_body, final=final),
        grid=(n // tm, tiles),
        in_specs=[
            pl.BlockSpec((tm, d), lambda i, j: (i, 0)),
            pl.BlockSpec((1, d), lambda i, j: (0, 0)),
            pl.BlockSpec((None, d, FF_TILE), lambda i, j: (streamed(j), 0, 0)),
            pl.BlockSpec((None, d, FF_TILE), lambda i, j: (streamed(j), 0, 0)),
            pl.BlockSpec((None, FF_TILE, d), lambda i, j: (streamed(j), 0, 0)),
            pl.BlockSpec((None, d, narrow), lambda i, j: (tiles - 1, 0, 0), pipeline_mode=once),
            pl.BlockSpec((None, d, narrow), lambda i, j: (tiles - 1, 0, 0), pipeline_mode=once),
            pl.BlockSpec((None, narrow, d), lambda i, j: (tiles - 1, 0, 0), pipeline_mode=once),
            pl.BlockSpec((1, d), lambda i, j: (0, 0)),
        ],
        out_specs=pl.BlockSpec((tm, d), lambda i, j: (i, 0)),
        out_shape=jax.ShapeDtypeStruct((n, d), F32),
        scratch_shapes=[pltpu.VMEM((tm, d), BF16)],
        compiler_params=_params("parallel", "arbitrary"),
        name="ffn",
    )(h, norm_g.reshape(1, d), wg, wu, wd, wg, wu, wd, final_g.reshape(1, d))


def _proj_ab_body(x_ref, g_ref, w_ref, cos_ref, sin_ref, *rest, live_slabs):
    riding = len(rest) > 2
    o_ref, xn_ref = rest[3 if riding else 0], rest[-1]
    if riding:
        _rider_body(*rest[:3], *rest[4:7], live_slabs=live_slabs)

    @pl.when(pl.program_id(1) == 0)
    def _():
        xn_ref[...] = _rms(x_ref[...], g_ref[...]).astype(BF16)

    col = pl.multiple_of(pl.program_id(1) * PROJ_TN, PROJ_TN)
    y = _dot(xn_ref[...], w_ref[:, pl.ds(col, PROJ_TN)])
    c = cos_ref[...]
    s = sin_ref[...]
    half = RET_HEAD_DIM // 2
    parts = []
    for hd in range(PROJ_TN // RET_HEAD_DIM):
        lo = hd * RET_HEAD_DIM
        x1 = y[:, lo:lo + half]
        x2 = y[:, lo + half:lo + 2 * half]
        parts += [x1 * c - x2 * s, x2 * c + x1 * s]
    o_ref[...] = jnp.concatenate(parts, axis=1)


def _proj_ab(h, norm_g, w, cos_tab, sin_tab, seq, ride):
    n, d = h.shape
    nout = w.shape[1]
    tm = min(1024, seq)
    per_seq = seq // tm
    ncols = nout // PROJ_TN
    rider = _Rider(*ride, (n // tm) * ncols, lambda i, j: i * ncols + j)
    extra = rider if rider.ok else None
    tab = pl.BlockSpec((None, tm, RET_HEAD_DIM // 2), lambda i, j: (jnp.minimum(j, 2), i % per_seq, 0))
    out = pl.pallas_call(
        functools.partial(_proj_ab_body, live_slabs=extra.live_slabs if extra else 0),
        grid=(n // tm, nout // PROJ_TN),
        in_specs=[
            pl.BlockSpec((tm, d), lambda i, j: (i, 0)),
            pl.BlockSpec((1, d), lambda i, j: (0, 0)),
            pl.BlockSpec((d, nout), lambda i, j: (0, 0), pipeline_mode=pl.Buffered(1)),
            tab, tab,
        ] + (extra.in_specs if extra else []),
        out_specs=[pl.BlockSpec((tm, PROJ_TN), lambda i, j: (i, j))] + (extra.out_specs if extra else []),
        out_shape=[jax.ShapeDtypeStruct((n, nout), F32)] + (extra.out_shape if extra else []),
        scratch_shapes=[pltpu.VMEM((tm, d), BF16)],
        compiler_params=_params("arbitrary", "arbitrary"),
        name="proj_ab",
    )(h, norm_g.reshape(1, d), w, cos_tab, sin_tab, *(extra.args if extra else ()))
    return out[0], (tuple(out[1:]) if extra else None)


def _cast_qkv_body(w_ref, o_ref):
    dh = DIFF_HEAD_DIM
    old = lax.broadcasted_iota(jnp.int32, (dh, dh), 0)
    new = lax.broadcasted_iota(jnp.int32, (dh, dh), 1)
    src = jnp.where(new < ROT_HALF, new,
                    jnp.where(new < dh // 2, new + ROT_HALF,
                              jnp.where(new < dh // 2 + ROT_HALF, new - (dh // 2 - ROT_HALF), new)))
    src = jnp.where(pl.program_id(1) < 2 * D_MODEL // PROJ_TN, src, new)
    perm = jnp.where(old == src, 1.0, 0.0).astype(BF16)
    w = w_ref[...].astype(BF16)
    parts = [_dot(w[:, hd * dh:(hd + 1) * dh], perm) for hd in range(PROJ_TN // dh)]
    o_ref[...] = jnp.concatenate(parts, axis=1).astype(BF16)


def _cast_qkv(w):
    r, c = w.shape
    spec = pl.BlockSpec((CAST_ROWS, PROJ_TN), lambda i, j: (i, j))
    return pl.pallas_call(
        _cast_qkv_body,
        grid=(r // CAST_ROWS, c // PROJ_TN),
        in_specs=[spec],
        out_specs=spec,
        out_shape=jax.ShapeDtypeStruct(w.shape, BF16),
        compiler_params=_params("parallel", "parallel"),
        name="cast_qkv",
    )(w)


def _proj_c_body(x_ref, g_ref, w_ref, cf_ref, sg_ref, o_ref, xn_ref):
    def column_tile(w_tile):
        y = _dot(xn_ref[...], w_tile)
        cf = cf_ref[...]
        sg = sg_ref[...]
        parts = []
        for hd in range(PROJ_TN // DIFF_HEAD_DIM):
            lo = hd * DIFF_HEAD_DIM
            yh = y[:, lo:lo + DIFF_HEAD_DIM]
            r = yh * cf + pltpu.roll(yh, DIFF_HEAD_DIM // 2, axis=1) * sg
            parts.append(r.astype(BF16))
        o_ref[...] = jnp.concatenate(parts, axis=1)

    @pl.when(pl.program_id(1) == 0)
    def _():
        xn_ref[...] = _rms(x_ref[...], g_ref[...]).astype(BF16)
        column_tile(w_ref[:, 0:PROJ_TN])

    @pl.when(pl.program_id(1) > 0)
    def _():
        col = pl.multiple_of(pl.program_id(1) * PROJ_TN, PROJ_TN)
        column_tile(w_ref[:, pl.ds(col, PROJ_TN)])


def _proj_c(h, norm_g, w, cf, sg, seq):
    n, d = h.shape
    nout = w.shape[1]
    tm = min(1024, seq)
    per_seq = seq // tm
    qk_tiles = 2 * D_MODEL // PROJ_TN
    tab = pl.BlockSpec((None, tm, DIFF_HEAD_DIM), lambda i, j: (j // qk_tiles, i % per_seq, 0))
    return pl.pallas_call(
        _proj_c_body,
        grid=(n // tm, nout // PROJ_TN),
        in_specs=[
            pl.BlockSpec((tm, d), lambda i, j: (i, 0)),
            pl.BlockSpec((1, d), lambda i, j: (0, 0)),
            pl.BlockSpec((d, nout), lambda i, j: (0, 0), pipeline_mode=pl.Buffered(1)),
            tab, tab,
        ],
        out_specs=pl.BlockSpec((tm, PROJ_TN), lambda i, j: (i, j)),
        out_shape=jax.ShapeDtypeStruct((n, nout), BF16),
        scratch_shapes=[pltpu.VMEM((tm, d), BF16)],
        compiler_params=_params("parallel", "arbitrary"),
        name="proj_c",
    )(h, norm_g.reshape(1, d), w, cf, sg)


RET_SLOTS = 3
RET_STREAMS = 4


def _ret_body(proj_hbm, dm_ref, qd_ref, kd_ref, cd_ref, o_ref, buf, sem, *, nblk, seq):
    t = RET_BLOCK
    heads = pl.num_programs(1)
    step_id = pl.program_id(0) * heads + pl.program_id(1)
    steps = pl.num_programs(0) * heads

    def copy(s, stream):
        row0 = pl.multiple_of((s // heads) * seq, seq)
        col0 = pl.multiple_of((stream * heads + s % heads) * RET_HEAD_DIM, RET_HEAD_DIM)
        slot = s % RET_SLOTS
        return pltpu.make_async_copy(proj_hbm.at[pl.ds(row0, seq), pl.ds(col0, RET_HEAD_DIM)],
                                     buf.at[slot, stream], sem.at[slot, stream])

    def start(s):
        for stream in range(RET_STREAMS):
            copy(s, stream).start()

    @pl.when(step_id == 0)
    def _():
        start(step_id)

        @pl.when(steps > 1)
        def _():
            start(step_id + 1)

    @pl.when(step_id + 2 < steps)
    def _():
        start(step_id + 2)

    for stream in range(RET_STREAMS):
        copy(step_id, stream).wait()

    slot = step_id % RET_SLOTS
    dmask = dm_ref[...]
    qdec = qd_ref[...]
    kdec = kd_ref[...]
    cdec = cd_ref[...]

    def rows(stream, n):
        return buf[slot, stream, pl.ds(pl.multiple_of(n * t, t), t), :]

    def step(n, state):
        q = rows(0, n)
        k = rows(1, n)
        vb = rows(2, n).astype(BF16)
        scores = _dot_nt(q.astype(BF16), k.astype(BF16)) * dmask
        intra = _dot(scores.astype(BF16), vb)
        inter = _dot((q * qdec).astype(BF16), state.astype(BF16))
        new_state = state * cdec + _dot_tn((k * kdec).astype(BF16), vb)
        o = intra + inter
        o = o * lax.rsqrt(jnp.mean(o * o, axis=-1, keepdims=True) + NORM_EPS)
        gate = rows(3, n)
        r0 = pl.multiple_of(n * t, t)
        o_ref[pl.ds(r0, t), :] = (o * (gate * jax.nn.sigmoid(gate))).astype(BF16)
        return new_state

    lax.fori_loop(0, nblk, step, jnp.zeros((RET_HEAD_DIM, RET_HEAD_DIM), F32), unroll=True)


def _retention(proj, batch, seq):
    n = proj.shape[0]
    t = RET_BLOCK
    hd = RET_HEAD_DIM
    log_g = np.log(1.0 - 2.0 ** (-5.0 - np.arange(RET_HEADS, dtype=np.float64)))
    idx = np.arange(t, dtype=np.float64)
    chunk = np.arange(t) // CHUNK
    visible = chunk[:, None] >= chunk[None, :]
    dist = np.abs(idx[:, None] - idx[None, :])
    dmask = np.where(visible[None], np.exp(log_g[:, None, None] * dist[None]), 0.0)
    f32 = np.float32
    dmask = dmask.astype(f32)
    qdec = np.broadcast_to(np.exp(log_g[:, None] * (idx[None] + 1.0))[:, :, None], (RET_HEADS, t, hd)).astype(f32)
    kdec = np.broadcast_to(np.exp(log_g[:, None] * (t - 1.0 - idx[None]))[:, :, None], (RET_HEADS, t, hd)).astype(f32)
    cdec = np.broadcast_to(np.exp(log_g * t)[:, None, None], (RET_HEADS, 1, hd)).astype(f32)

    def tab(rows, cols):
        return pl.BlockSpec((None, rows, cols), lambda b, h: (h, 0, 0))

    return pl.pallas_call(
        functools.partial(_ret_body, nblk=seq // t, seq=seq),
        grid=(batch, RET_HEADS),
        in_specs=[pl.BlockSpec(memory_space=pl.ANY),
                  tab(t, t), tab(t, hd), tab(t, hd), tab(1, hd)],
        out_specs=pl.BlockSpec((seq, hd), lambda b, h: (b, h)),
        out_shape=jax.ShapeDtypeStruct((n, RET_WIDTH), BF16),
        scratch_shapes=[pltpu.VMEM((RET_SLOTS, RET_STREAMS, seq, hd), F32),
                        pltpu.SemaphoreType.DMA((RET_SLOTS, RET_STREAMS))],
        compiler_params=_params("arbitrary", "arbitrary"),
        name="retention",
    )(proj, dmask, qdec, kdec, cdec)


def _s5_weights(lam_re, lam_im, log_step, b_re, b_im, c_re, c_im):
    hp = lax.Precision.HIGH
    step = jnp.exp(log_step)[:, None]
    mag = jnp.exp(lam_re * step)
    a_re = mag * jnp.cos(lam_im * step)
    a_im = mag * jnp.sin(lam_im * step)
    den = lam_re * lam_re + lam_im * lam_im
    nr = a_re - 1.0
    f_re = (nr * lam_re + a_im * lam_im) / den
    f_im = (a_im * lam_re - nr * lam_im) / den
    bb_re = f_re[..., None] * b_re - f_im[..., None] * b_im
    bb_im = f_re[..., None] * b_im + f_im[..., None] * b_re
    prs, pis = [jnp.ones_like(a_re)], [jnp.zeros_like(a_im)]
    for _ in range(SSM_T):
        prs.append(prs[-1] * a_re - pis[-1] * a_im)
        pis.append(prs[-2] * a_im + pis[-1] * a_re)
    pr = jnp.stack(prs)
    pi = jnp.stack(pis)
    ca_re = c_re[None] * pr[:, :, None, :] - c_im[None] * pi[:, :, None, :]
    ca_im = c_re[None] * pi[:, :, None, :] + c_im[None] * pr[:, :, None, :]
    kern = jnp.einsum('kgpm,gmq->gkpq', jnp.concatenate([ca_re[:SSM_T], -ca_im[:SSM_T]], axis=-1),
                      jnp.concatenate([bb_re, bb_im], axis=1), precision=hp)
    nj = SSM_NJ
    ks = kern.transpose(0, 3, 1, 2).reshape(nj, 128, SSM_T * SSM_GROUP)
    ro = jnp.stack([ca_re[1:], -ca_im[1:]])
    ro = ro.reshape(2, SSM_T, nj, SSM_GPB, SSM_GROUP, SSM_STATE).transpose(2, 0, 3, 5, 1, 4)
    wo = ro.reshape(nj, 2 * SSM_GPB * SSM_STATE, SSM_T * SSM_GROUP)
    rev_re = jnp.stack([prs[SSM_T - 1 - s] for s in range(SSM_T)])
    rev_im = jnp.stack([pis[SSM_T - 1 - s] for s in range(SSM_T)])
    rev_re, rev_im = rev_re[:, :, None, :], rev_im[:, :, None, :]
    bt_re, bt_im = bb_re.transpose(0, 2, 1)[None], bb_im.transpose(0, 2, 1)[None]
    wi = jnp.concatenate([rev_re * bt_re - rev_im * bt_im, rev_re * bt_im + rev_im * bt_re], axis=-1)
    wi = wi.reshape(SSM_T, nj, 128, 2 * SSM_STATE).transpose(1, 0, 2, 3).reshape(nj, SSM_T * 128, 2 * SSM_STATE)
    a_t = jnp.stack([pr[SSM_T], pi[SSM_T]]).reshape(2, nj, SSM_GPB * SSM_STATE)
    a_t = jnp.moveaxis(a_t, 0, 1)
    return ks, wo, wi, a_t


def _s5_expand_body(ks_ref, wo_ref, wi_ref, wy_ref, win_ref):
    tp = SSM_T * SSM_GROUP
    wide = SSM_T * 128
    half = SSM_GPB * SSM_STATE

    def iota(shape, dim):
        return lax.broadcasted_iota(jnp.int32, shape, dim)

    r = iota((tp, wide), 0)
    c = iota((tp, wide), 1)
    rep = jnp.where((r // SSM_GROUP == c // 128) & (r % SSM_GROUP == c % SSM_GROUP), 1.0, 0.0).astype(BF16)
    col_b = (iota((1, wide), 1) % 128) // SSM_GROUP

    row_a = iota((128, 1), 0) // SSM_GROUP
    bdk = jnp.where(row_a == col_b, _dot(ks_ref[...].astype(BF16), rep), 0.0).astype(BF16)
    for s in range(SSM_T):
        if s:
            wy_ref[s * 128:(s + 1) * 128, :s * 128] = jnp.zeros((128, s * 128), BF16)
        wy_ref[s * 128:(s + 1) * 128, s * 128:] = bdk[:, :wide - s * 128]

    row_a = (iota((2 * half, 1), 0) % half) // SSM_STATE
    ro = _dot(wo_ref[...].astype(BF16), rep)
    wy_ref[wide:, :] = jnp.where(row_a == col_b, ro, 0.0).astype(BF16)

    r = iota((2 * SSM_STATE, 2 * half), 0)
    c = iota((2 * SSM_STATE, 2 * half), 1)
    rep_in = jnp.where((r // SSM_STATE == c // half) & (r % SSM_STATE == c % SSM_STATE), 1.0, 0.0).astype(BF16)
    row_a = (iota((wide, 1), 0) % 128) // SSM_GROUP
    col_b = (iota((1, 2 * half), 1) % half) // SSM_STATE
    win = _dot(wi_ref[...].astype(BF16), rep_in)
    win_ref[...] = jnp.where(row_a == col_b, win, 0.0).astype(BF16)


def _s5_expand(ks, wo, wi):
    nj = ks.shape[0]
    wide = SSM_T * 128
    half2 = 2 * SSM_GPB * SSM_STATE

    def blk(a):
        return pl.BlockSpec((None,) + a.shape[1:], lambda j: (j, 0, 0))

    return pl.pallas_call(
        _s5_expand_body,
        grid=(nj,),
        in_specs=[blk(ks), blk(wo), blk(wi)],
        out_specs=[pl.BlockSpec((None, wide + half2, wide), lambda j: (j, 0, 0)),
                   pl.BlockSpec((None, wide, half2), lambda j: (j, 0, 0))],
        out_shape=[jax.ShapeDtypeStruct((nj, wide + half2, wide), BF16),
                   jax.ShapeDtypeStruct((nj, wide, half2), BF16)],
        compiler_params=_params("parallel"),
        name="s5_expand",
    )(ks, wo, wi)


def _s5_body(u_ref, wy_ref, win_ref, at_ref, y_ref, s_scr, hp_scr, *, nb, nchunk):
    half = SSM_GPB * SSM_STATE
    r = nb * nchunk
    u = jnp.concatenate([u_ref[pl.ds(s, r, stride=SSM_T), :] for s in range(SSM_T)],
                        axis=1).astype(BF16)
    nslab = half // 128
    s_all = _dot(u, win_ref[...])
    for k in range(nslab):
        s_scr[k, 0:r, :] = s_all[:, k * 128:(k + 1) * 128]
        s_scr[k, r:2 * r, :] = s_all[:, half + k * 128:half + (k + 1) * 128]
    is_re = lax.broadcasted_iota(jnp.int32, (2 * nb, 128), 0) < nb
    coef_same = [jnp.broadcast_to(at_ref[0:1, k * 128:(k + 1) * 128], (2 * nb, 128)) for k in range(nslab)]
    coef_swap = [jnp.where(is_re, -at_ref[1:2, k * 128:(k + 1) * 128], at_ref[1:2, k * 128:(k + 1) * 128])
                 for k in range(nslab)]
    h = [jnp.zeros((2 * nb, 128), F32)] * nslab
    for c in range(nchunk):
        chunk_rows = pl.ds(c, 2 * nb, stride=nchunk)
        for k in range(nslab):
            hp_scr[k, chunk_rows, :] = h[k]
            h[k] = (coef_same[k] * h[k] + coef_swap[k] * pltpu.roll(h[k], nb, axis=0)
                    + s_scr[k, chunk_rows, :])
    hp = jnp.concatenate([hp_scr[k, 0:r, :] for k in range(nslab)]
                         + [hp_scr[k, r:2 * r, :] for k in range(nslab)], axis=1).astype(BF16)
    wide = SSM_T * 128
    cb = 256
    for lo in range(0, wide, cb):
        y = (_dot(u[:, :lo + cb], wy_ref[:lo + cb, lo:lo + cb])
             + _dot(hp, wy_ref[wide:, lo:lo + cb]))
        for t in range(lo // 128, (lo + cb) // 128):
            y_ref[pl.ds(t, r, stride=SSM_T), :] = y[:, t * 128 - lo:(t + 1) * 128 - lo]


def _s5(proj, w_y, w_in, a_t, batch, seq):
    n, width = proj.shape
    nchunk = seq // SSM_T
    nb = min(4, batch)
    r = nb * nchunk
    ucol0 = (width - SSM_WIDTH) // 128
    half2 = 2 * SSM_GPB * SSM_STATE
    once = pl.Buffered(1)
    return pl.pallas_call(
        functools.partial(_s5_body, nb=nb, nchunk=nchunk),
        grid=(SSM_NJ, batch // nb),
        in_specs=[
            pl.BlockSpec((r * SSM_T, 128), lambda j, i: (i, ucol0 + j)),
            pl.BlockSpec((None, SSM_T * 128 + half2, SSM_T * 128), lambda j, i: (j, 0, 0)),
            pl.BlockSpec((None, SSM_T * 128, half2), lambda j, i: (j, 0, 0)),
            pl.BlockSpec((None, 2, half2 // 2), lambda j, i: (j, 0, 0)),
        ],
        out_specs=pl.BlockSpec((r * SSM_T, 128), lambda j, i: (i, j)),
        out_shape=jax.ShapeDtypeStruct((n, SSM_WIDTH), F32),
        scratch_shapes=[pltpu.VMEM((half2 // 256, 2 * r, 128), F32), pltpu.VMEM((half2 // 256, 2 * r, 128), F32)],
        compiler_params=_params("arbitrary", "arbitrary"),
        name="s5",
    )(proj, w_y, w_in, a_t)


def _gelu(x):
    return 0.5 * x * (1.0 + jnp.tanh(math.sqrt(2.0 / math.pi) * (x + 0.044715 * (x * x * x))))


def _ab_out_body(ya_ref, ys_ref, u_ref, d_ref, wglu_ref, bglu_ref, woa_ref, wob_ref, h_ref, o_ref):
    y = ys_ref[...] + d_ref[...] * u_ref[...]
    z = _gelu(y)
    gate = jax.nn.sigmoid(_dot(z.astype(BF16), wglu_ref[...]) + bglu_ref[...])
    yb = (z * gate).astype(BF16)
    o_ref[...] = h_ref[...] + (_dot(ya_ref[...], woa_ref[...]) + _dot(yb, wob_ref[...]))


def _ab_out(y_a, y_ssm, proj, d_skip, w_glu, b_glu, w_out, h):
    n, d = h.shape
    tm = min(512, n)
    ucol = (proj.shape[1] - SSM_WIDTH) // SSM_WIDTH
    row = lambda i: (i, 0)
    fixed = lambda i: (0, 0)
    return pl.pallas_call(
        _ab_out_body,
        grid=(n // tm,),
        in_specs=[
            pl.BlockSpec((tm, RET_WIDTH), row),
            pl.BlockSpec((tm, SSM_WIDTH), row),
            pl.BlockSpec((tm, SSM_WIDTH), lambda i: (i, ucol)),
            pl.BlockSpec((1, SSM_WIDTH), fixed),
            pl.BlockSpec((SSM_WIDTH, SSM_WIDTH), fixed),
            pl.BlockSpec((1, SSM_WIDTH), fixed),
            pl.BlockSpec((RET_WIDTH, d), fixed),
            pl.BlockSpec((SSM_WIDTH, d), lambda i: (1, 0)),
            pl.BlockSpec((tm, d), row),
        ],
        out_specs=pl.BlockSpec((tm, d), row),
        out_shape=jax.ShapeDtypeStruct((n, d), F32),
        compiler_params=_params("parallel"),
        name="ab_out",
    )(y_a, y_ssm, proj, d_skip.reshape(1, -1), w_glu, b_glu.reshape(1, -1), w_out, w_out, h)


def _att_body(lam_ref, q_ref, k_ref, v_ref, sub_ref, *rest, seq, lambda_init, live_slabs):
    riding = len(rest) > 1
    o_ref = rest[3 if riding else 0]
    if riding:
        _rider_body(*rest[:3], *rest[4:7], live_slabs=live_slabs)
    tq = min(ATT_QBLOCK, seq)
    dh = DIFF_HEAD_DIM
    k2 = dh ** -0.5 * math.log2(math.e)
    lam = lam_ref[0]
    neg = jnp.finfo(F32).min
    rc = lax.broadcasted_iota(jnp.int32, (tq, tq), 0) // CHUNK
    cc = lax.broadcasted_iota(jnp.int32, (tq, tq), 1) // CHUNK
    visible = rc >= cc
    for qb in range(seq // tq):
        q0 = qb * tq
        q = q_ref[q0:q0 + tq, :]
        v_diag = v_ref[q0:q0 + tq, :]
        comps = []
        for c in range(2):
            qc = q[:, c * dh:(c + 1) * dh]
            s_diag = _dot_nt(qc, k_ref[q0:q0 + tq, c * dh:(c + 1) * dh])
            s_diag = jnp.where(visible, s_diag, neg)
            m = jnp.max(s_diag, axis=-1, keepdims=True)
            if qb > 0:
                s_off = _dot_nt(qc, k_ref[0:q0, c * dh:(c + 1) * dh])
                m = jnp.maximum(m, jnp.max(s_off, axis=-1, keepdims=True))
            mk = m * k2
            p_diag = jnp.exp2(s_diag * k2 - mk)
            l = jnp.sum(p_diag, axis=-1, keepdims=True)
            acc = _dot(p_diag.astype(BF16), v_diag)
            if qb > 0:
                p_off = jnp.exp2(s_off * k2 - mk)
                l = l + jnp.sum(p_off, axis=-1, keepdims=True)
                acc = acc + _dot(p_off.astype(BF16), v_ref[0:q0, :])
            comps.append(acc / l)
        o = comps[0] - lam * comps[1]
        o = _rms(o, sub_ref[...]) * (1.0 - lambda_init)
        o_ref[q0:q0 + tq, :] = o.astype(BF16)


def _attention(qkv, lam, subln, batch, seq, lambda_init, ride):
    n = qkv.shape[0]
    w = 2 * DIFF_HEAD_DIM
    rider = _Rider(*ride, batch * DIFF_HEADS, lambda b, h: b * DIFF_HEADS + h)
    extra = rider if rider.ok else None

    def col(off):
        return pl.BlockSpec((seq, w), lambda b, h: (b, off + h))

    out = pl.pallas_call(
        functools.partial(_att_body, seq=seq, lambda_init=lambda_init,
                          live_slabs=extra.live_slabs if extra else 0),
        grid=(batch, DIFF_HEADS),
        in_specs=[
            pl.BlockSpec(memory_space=pltpu.SMEM),
            col(0), col(DIFF_HEADS), col(2 * DIFF_HEADS),
            pl.BlockSpec((1, w), lambda b, h: (0, 0)),
        ] + (extra.in_specs if extra else []),
        out_specs=[pl.BlockSpec((seq, w), lambda b, h: (b, h))] + (extra.out_specs if extra else []),
        out_shape=[jax.ShapeDtypeStruct((n, D_MODEL), BF16)] + (extra.out_shape if extra else []),
        compiler_params=_params("arbitrary", "arbitrary"),
        name="diff_attention",
    )(lam.reshape(1), qkv, qkv, qkv, subln.reshape(1, w), *(extra.args if extra else ()))
    return out[0], (tuple(out[1:]) if extra else None)


def _out_proj_body(o_ref, w_ref, h_ref, y_ref):
    y_ref[...] = h_ref[...] + _dot(o_ref[...], w_ref[...])


def _out_proj(o, w, h):
    n, d = h.shape
    tm = min(512, n)
    return pl.pallas_call(
        _out_proj_body,
        grid=(n // tm,),
        in_specs=[
            pl.BlockSpec((tm, d), lambda i: (i, 0)),
            pl.BlockSpec((d, d), lambda i: (0, 0)),
            pl.BlockSpec((tm, d), lambda i: (i, 0)),
        ],
        out_specs=pl.BlockSpec((tm, d), lambda i: (i, 0)),
        out_shape=jax.ShapeDtypeStruct((n, d), F32),
        compiler_params=_params("parallel"),
        name="out_proj",
    )(o, w, h)


def _rope_tables(seq, rot_dim, theta):
    inv = 1.0 / (theta ** (np.arange(0, rot_dim, 2, dtype=np.float64) / rot_dim))
    ang = np.arange(seq, dtype=np.float64)[:, None] * inv[None, :]
    return np.cos(ang), np.sin(ang)


def kernel(x, ffn_norm, ffn_w_gate, ffn_w_up, ffn_w_down, mix_norm, ab_w_in, ab_w_out, ssm_lambda_re, ssm_lambda_im, ssm_log_step, ssm_b_re, ssm_b_im, ssm_c_re, ssm_c_im, ssm_d, ssm_w_glu, ssm_b_glu, c_w_qkv, c_w_out, c_lambda_q1, c_lambda_k1, c_lambda_q2, c_lambda_k2, c_subln, final_norm):
    batch, seq, d = x.shape
    n = batch * seq
    h = x.reshape(n, d)
    d_ff = ffn_w_down.shape[2]
    ffn_w = (ffn_w_gate, ffn_w_up, ffn_w_down)

    def ffn(h, layer, half, weights=None, final=False):
        wg, wu, wd = weights or _cast_ffn_weights(*ffn_w, layer, half)
        return _ffn(h, ffn_norm[layer, half], wg, wu, wd, final_norm, d_ff, final=final)

    h = ffn(h, 0, 0)
    ret_cos, ret_sin = _rope_tables(seq, RET_HEAD_DIM, RET_THETA)
    k_scale = RET_HEAD_DIM ** -0.5
    cos_tab = np.stack([ret_cos, ret_cos * k_scale, np.ones_like(ret_cos)]).astype(np.float32)
    sin_tab = np.stack([ret_sin, ret_sin * k_scale, np.zeros_like(ret_sin)]).astype(np.float32)
    proj, w01 = _proj_ab(h, mix_norm[0], ab_w_in[0].astype(BF16), cos_tab, sin_tab, seq, (*ffn_w, 0, 1))
    y_a = _retention(proj, batch, seq)
    ks, wo, wi, a_t = _s5_weights(ssm_lambda_re[0], ssm_lambda_im[0], ssm_log_step[0], ssm_b_re[0],
                                  ssm_b_im[0], ssm_c_re[0], ssm_c_im[0])
    w_y, w_in = _s5_expand(ks, wo, wi)
    y_ssm = _s5(proj, w_y, w_in, a_t, batch, seq)
    h = _ab_out(y_a, y_ssm, proj, ssm_d[0], ssm_w_glu[0].astype(BF16), ssm_b_glu[0],
                ab_w_out[0].astype(BF16), h)
    h = ffn(h, 0, 1, w01)

    h = ffn(h, 1, 0)
    att_cos, att_sin = _rope_tables(seq, 2 * ROT_HALF, ROPE_THETA)
    rest = np.zeros((seq, DIFF_HEAD_DIM // 2 - ROT_HALF))
    ident = np.zeros((seq, DIFF_HEAD_DIM))
    cf = np.stack([np.concatenate([att_cos, rest + 1.0, att_cos, rest + 1.0], axis=1), ident + 1.0])
    sg = np.stack([np.concatenate([-att_sin, rest, att_sin, rest], axis=1), ident])
    cf, sg = cf.astype(np.float32), sg.astype(np.float32)
    qkv = _proj_c(h, mix_norm[1], _cast_qkv(c_w_qkv[0]), cf, sg, seq)
    lambda_init = 0.8 - 0.6 * math.exp(-0.3 * 1)
    lam = (jnp.exp(jnp.sum(c_lambda_q1[0] * c_lambda_k1[0]))
           - jnp.exp(jnp.sum(c_lambda_q2[0] * c_lambda_k2[0])) + lambda_init)
    o, w11 = _attention(qkv, lam, c_subln[0], batch, seq, lambda_init, (*ffn_w, 1, 1))
    h = _out_proj(o, c_w_out[0].astype(BF16), h)
    h = ffn(h, 1, 1, w11, final=True)
    return h.reshape(batch, seq, d)
```

```python
import functools
import math

import jax
import jax.numpy as jnp
import numpy as np
from jax import lax
from jax.experimental import pallas as pl
from jax.experimental.pallas import tpu as pltpu

F32 = jnp.float32
BF16 = jnp.bfloat16

D_MODEL = 2048
CHUNK = 64
NORM_EPS = 1e-6

RET_WIDTH = 1024
RET_HEADS = 4
RET_HEAD_DIM = 256
RET_THETA = 10000.0
RET_BLOCK = 256

SSM_WIDTH = 1024
SSM_GROUP = 16
SSM_GROUPS = 64
SSM_STATE = 64
SSM_T = 16
SSM_GPB = 128 // SSM_GROUP
SSM_NJ = SSM_WIDTH // 128

DIFF_HEAD_DIM = 128
DIFF_HEADS = 8
ROPE_THETA = 500000.0
ROT_HALF = DIFF_HEAD_DIM // 4 // 2
ATT_QBLOCK = 512

FF_TILE = 1024
NARROW_STEP = 1
CAST_ROWS = 512
PROJ_TN = 1024
VMEM_LIMIT = 58 * 1024 * 1024


def _params(*sem):
    return pltpu.CompilerParams(dimension_semantics=sem, vmem_limit_bytes=VMEM_LIMIT)


def _rms(x, g):
    y = x * lax.rsqrt(jnp.mean(x * x, axis=-1, keepdims=True) + NORM_EPS)
    return y * g


def _dot(a, b):
    return jnp.dot(a, b, preferred_element_type=F32)


def _dot_nt(a, b):
    return lax.dot_general(a, b, (((1,), (1,)), ((), ())), preferred_element_type=F32)


def _dot_tn(a, b):
    return lax.dot_general(a, b, (((0,), (0,)), ((), ())), preferred_element_type=F32)


def _cast_tiles_body(w_ref, o_ref):
    cols = w_ref.shape[1]
    for k in range(o_ref.shape[0]):
        lo = k * FF_TILE
        width = min(FF_TILE, cols - lo)
        o_ref[k, :, :width] = w_ref[:, lo:lo + width].astype(BF16)
        if width < FF_TILE:
            o_ref[k, :, width:] = jnp.zeros((o_ref.shape[1], FF_TILE - width), BF16)


def _cast_col_tiles(w, layer, half):
    r, c = w.shape[2:]
    tiles = pl.cdiv(c, FF_TILE)
    return pl.pallas_call(
        _cast_tiles_body,
        grid=(r // CAST_ROWS,),
        in_specs=[pl.BlockSpec((None, None, CAST_ROWS, c), lambda i: (layer, half, i, 0))],
        out_specs=pl.BlockSpec((tiles, CAST_ROWS, FF_TILE), lambda i: (0, i, 0)),
        out_shape=jax.ShapeDtypeStruct((tiles, r, FF_TILE), BF16),
        compiler_params=_params("parallel"),
        name="cast_col_tiles",
    )(w)


def _cast_row_tiles_body(w_ref, o_ref, *, rows):
    valid = rows - pl.program_id(0) * FF_TILE
    row = lax.broadcasted_iota(jnp.int32, w_ref.shape, 0)
    o_ref[...] = jnp.where(row < valid, w_ref[...], 0.0).astype(BF16)


def _cast_row_tiles(w, layer, half):
    r, c = w.shape[2:]
    tiles = pl.cdiv(r, FF_TILE)
    return pl.pallas_call(
        functools.partial(_cast_row_tiles_body, rows=r),
        grid=(tiles,),
        in_specs=[pl.BlockSpec((None, None, FF_TILE, c), lambda k: (layer, half, k, 0))],
        out_specs=pl.BlockSpec((None, FF_TILE, c), lambda k: (k, 0, 0)),
        out_shape=jax.ShapeDtypeStruct((tiles, FF_TILE, c), BF16),
        compiler_params=_params("parallel"),
        name="cast_row_tiles",
    )(w)


def _cast_ffn_weights(w_gate, w_up, w_down, layer, half):
    return (_cast_col_tiles(w_gate, layer, half), _cast_col_tiles(w_up, layer, half),
            _cast_row_tiles(w_down, layer, half))


class _Rider:
    def __init__(self, w_gate, w_up, w_down, layer, half, steps, step_of):
        r, c = w_gate.shape[2:]
        rows_dn, d = w_down.shape[2:]
        tiles = pl.cdiv(c, FF_TILE)
        sub = 16
        col_rows = sub * pl.cdiv(r // sub, steps)
        tiles_dn = pl.cdiv(rows_dn, FF_TILE)
        row_rows = next((k for k in (64, 128, 256, 512, FF_TILE)
                         if FF_TILE % k == 0 and rows_dn % k == 0 and tiles_dn * FF_TILE // k <= steps), None)
        self.ok = r % col_rows == 0 and row_rows is not None
        if not self.ok:
            return
        ncol, per_tile = r // col_rows, FF_TILE // row_rows
        self.live_slabs = rows_dn // row_rows
        all_slabs = tiles_dn * per_tile

        def col_slab(*g):
            return jnp.minimum(step_of(*g), ncol - 1)

        def row_slab(*g):
            return jnp.minimum(step_of(*g), all_slabs - 1)

        col_in = pl.BlockSpec((None, None, col_rows, c), lambda *g: (layer, half, col_slab(*g), 0))
        self.args = (w_gate, w_up, w_down)
        self.in_specs = [col_in, col_in,
                         pl.BlockSpec((None, None, row_rows, d),
                                      lambda *g: (layer, half, jnp.minimum(row_slab(*g), self.live_slabs - 1), 0))]
        col_out = pl.BlockSpec((tiles, col_rows, FF_TILE), lambda *g: (0, col_slab(*g), 0))
        self.out_specs = [col_out, col_out,
                          pl.BlockSpec((None, row_rows, d),
                                       lambda *g: (row_slab(*g) // per_tile, row_slab(*g) % per_tile, 0))]
        self.out_shape = [jax.ShapeDtypeStruct((tiles, r, FF_TILE), BF16)] * 2 + [
            jax.ShapeDtypeStruct((tiles_dn, FF_TILE, d), BF16)]


def _rider_body(gate_ref, up_ref, down_ref, gate_out, up_out, down_out, *, live_slabs):
    step = pl.program_id(0) * pl.num_programs(1) + pl.program_id(1)
    _cast_tiles_body(gate_ref, gate_out)
    _cast_tiles_body(up_ref, up_out)
    down_out[...] = jnp.where(step < live_slabs, down_ref[...], 0.0).astype(BF16)


def _ffn_body(x_ref, g_ref, wg_ref, wu_ref, wd_ref, wgn_ref, wun_ref, wdn_ref, fn_ref, o_ref, xn_ref, *, final):
    j = pl.program_id(1)
    last = pl.num_programs(1) - 1

    def tile(wg, wu, wd):
        xn = xn_ref[...]
        gate = _dot(xn, wg[...])
        up = _dot(xn, wu[...])
        act = (gate * jax.nn.sigmoid(gate) * up).astype(BF16)
        return _dot(act, wd[...])

    @pl.when(j == 0)
    def _():
        xn_ref[...] = _rms(x_ref[...], g_ref[...]).astype(BF16)
        o_ref[...] = tile(wg_ref, wu_ref, wd_ref)

    @pl.when(j == NARROW_STEP)
    def _():
        o_ref[...] += tile(wgn_ref, wun_ref, wdn_ref)

    @pl.when((j > 0) & (j < last) & (j != NARROW_STEP))
    def _():
        o_ref[...] += tile(wg_ref, wu_ref, wd_ref)

    @pl.when(j == last)
    def _():
        h = x_ref[...] + 0.5 * (o_ref[...] + tile(wg_ref, wu_ref, wd_ref))
        if final:
            h = _rms(h, fn_ref[...])
        o_ref[...] = h


def _ffn(h, norm_g, wg, wu, wd, final_g, ff, *, final):
    n, d = h.shape
    tm = min(512, n)
    tiles = wg.shape[0]
    narrow = ff - (tiles - 1) * FF_TILE
    assert 0 < NARROW_STEP < tiles - 1 and narrow % 128 == 0

    def streamed(j):
        return jnp.where(j <= NARROW_STEP, j, j - 1)

    once = pl.Buffered(1)
    return pl.pallas_call(
        functools.partial(_ffn_body, final=final),
        grid=(n // tm, tiles),
        in_specs=[
            pl.BlockSpec((tm, d), lambda i, j: (i, 0)),
            pl.BlockSpec((1, d), lambda i, j: (0, 0)),
            pl.BlockSpec((None, d, FF_TILE), lambda i, j: (streamed(j), 0, 0)),
            pl.BlockSpec((None, d, FF_TILE), lambda i, j: (streamed(j), 0, 0)),
            pl.BlockSpec((None, FF_TILE, d), lambda i, j: (streamed(j), 0, 0)),
            pl.BlockSpec((None, d, narrow), lambda i, j: (tiles - 1, 0, 0), pipeline_mode=once),
            pl.BlockSpec((None, d, narrow), lambda i, j: (tiles - 1, 0, 0), pipeline_mode=once),
            pl.BlockSpec((None, narrow, d), lambda i, j: (tiles - 1, 0, 0), pipeline_mode=once),
            pl.BlockSpec((1, d), lambda i, j: (0, 0)),
        ],
        out_specs=pl.BlockSpec((tm, d), lambda i, j: (i, 0)),
        out_shape=jax.ShapeDtypeStruct((n, d), F32),
        scratch_shapes=[pltpu.VMEM((tm, d), BF16)],
        compiler_params=_params("parallel", "arbitrary"),
        name="ffn",
    )(h, norm_g.reshape(1, d), wg, wu, wd, wg, wu, wd, final_g.reshape(1, d))


def _proj_ab_body(x_ref, g_ref, w_ref, cos_ref, sin_ref, *rest, live_slabs):
    riding = len(rest) > 2
    o_ref, xn_ref = rest[3 if riding else 0], rest[-1]
    if riding:
        _rider_body(*rest[:3], *rest[4:7], live_slabs=live_slabs)

    @pl.when(pl.program_id(1) == 0)
    def _():
        xn_ref[...] = _rms(x_ref[...], g_ref[...]).astype(BF16)

    col = pl.multiple_of(pl.program_id(1) * PROJ_TN, PROJ_TN)
    y = _dot(xn_ref[...], w_ref[:, pl.ds(col, PROJ_TN)])
    c = cos_ref[...]
    s = sin_ref[...]
    half = RET_HEAD_DIM // 2
    parts = []
    for hd in range(PROJ_TN // RET_HEAD_DIM):
        lo = hd * RET_HEAD_DIM
        x1 = y[:, lo:lo + half]
        x2 = y[:, lo + half:lo + 2 * half]
        parts += [x1 * c - x2 * s, x2 * c + x1 * s]
    o_ref[...] = jnp.concatenate(parts, axis=1)


def _proj_ab(h, norm_g, w, cos_tab, sin_tab, seq, ride):
    n, d = h.shape
    nout = w.shape[1]
    tm = min(1024, seq)
    per_seq = seq // tm
    ncols = nout // PROJ_TN
    rider = _Rider(*ride, (n // tm) * ncols, lambda i, j: i * ncols + j)
    extra = rider if rider.ok else None
    tab = pl.BlockSpec((None, tm, RET_HEAD_DIM // 2), lambda i, j: (jnp.minimum(j, 2), i % per_seq, 0))
    out = pl.pallas_call(
        functools.partial(_proj_ab_body, live_slabs=extra.live_slabs if extra else 0),
        grid=(n // tm, nout // PROJ_TN),
        in_specs=[
            pl.BlockSpec((tm, d), lambda i, j: (i, 0)),
            pl.BlockSpec((1, d), lambda i, j: (0, 0)),
            pl.BlockSpec((d, nout), lambda i, j: (0, 0), pipeline_mode=pl.Buffered(1)),
            tab, tab,
        ] + (extra.in_specs if extra else []),
        out_specs=[pl.BlockSpec((tm, PROJ_TN), lambda i, j: (i, j))] + (extra.out_specs if extra else []),
        out_shape=[jax.ShapeDtypeStruct((n, nout), F32)] + (extra.out_shape if extra else []),
        scratch_shapes=[pltpu.VMEM((tm, d), BF16)],
        compiler_params=_params("arbitrary", "arbitrary"),
        name="proj_ab",
    )(h, norm_g.reshape(1, d), w, cos_tab, sin_tab, *(extra.args if extra else ()))
    return out[0], (tuple(out[1:]) if extra else None)


def _cast_qkv_body(w_ref, o_ref):
    dh = DIFF_HEAD_DIM
    old = lax.broadcasted_iota(jnp.int32, (dh, dh), 0)
    new = lax.broadcasted_iota(jnp.int32, (dh, dh), 1)
    src = jnp.where(new < ROT_HALF, new,
                    jnp.where(new < dh // 2, new + ROT_HALF,
                              jnp.where(new < dh // 2 + ROT_HALF, new - (dh // 2 - ROT_HALF), new)))
    src = jnp.where(pl.program_id(1) < 2 * D_MODEL // PROJ_TN, src, new)
    perm = jnp.where(old == src, 1.0, 0.0).astype(BF16)
    w = w_ref[...].astype(BF16)
    parts = [_dot(w[:, hd * dh:(hd + 1) * dh], perm) for hd in range(PROJ_TN // dh)]
    o_ref[...] = jnp.concatenate(parts, axis=1).astype(BF16)


def _cast_qkv(w):
    r, c = w.shape
    spec = pl.BlockSpec((CAST_ROWS, PROJ_TN), lambda i, j: (i, j))
    return pl.pallas_call(
        _cast_qkv_body,
        grid=(r // CAST_ROWS, c // PROJ_TN),
        in_specs=[spec],
        out_specs=spec,
        out_shape=jax.ShapeDtypeStruct(w.shape, BF16),
        compiler_params=_params("parallel", "parallel"),
        name="cast_qkv",
    )(w)


def _proj_c_body(x_ref, g_ref, w_ref, cf_ref, sg_ref, o_ref, xn_ref):
    def column_tile(w_tile):
        y = _dot(xn_ref[...], w_tile)
        cf = cf_ref[...]
        sg = sg_ref[...]
        parts = []
        for hd in range(PROJ_TN // DIFF_HEAD_DIM):
            lo = hd * DIFF_HEAD_DIM
            yh = y[:, lo:lo + DIFF_HEAD_DIM]
            r = yh * cf + pltpu.roll(yh, DIFF_HEAD_DIM // 2, axis=1) * sg
            parts.append(r.astype(BF16))
        o_ref[...] = jnp.concatenate(parts, axis=1)

    @pl.when(pl.program_id(1) == 0)
    def _():
        xn_ref[...] = _rms(x_ref[...], g_ref[...]).astype(BF16)
        column_tile(w_ref[:, 0:PROJ_TN])

    @pl.when(pl.program_id(1) > 0)
    def _():
        col = pl.multiple_of(pl.program_id(1) * PROJ_TN, PROJ_TN)
        column_tile(w_ref[:, pl.ds(col, PROJ_TN)])


def _proj_c(h, norm_g, w, cf, sg, seq):
    n, d = h.shape
    nout = w.shape[1]
    tm = min(1024, seq)
    per_seq = seq // tm
    qk_tiles = 2 * D_MODEL // PROJ_TN
    tab = pl.BlockSpec((None, tm, DIFF_HEAD_DIM), lambda i, j: (j // qk_tiles, i % per_seq, 0))
    return pl.pallas_call(
        _proj_c_body,
        grid=(n // tm, nout // PROJ_TN),
        in_specs=[
            pl.BlockSpec((tm, d), lambda i, j: (i, 0)),
            pl.BlockSpec((1, d), lambda i, j: (0, 0)),
            pl.BlockSpec((d, nout), lambda i, j: (0, 0), pipeline_mode=pl.Buffered(1)),
            tab, tab,
        ],
        out_specs=pl.BlockSpec((tm, PROJ_TN), lambda i, j: (i, j)),
        out_shape=jax.ShapeDtypeStruct((n, nout), BF16),
        scratch_shapes=[pltpu.VMEM((tm, d), BF16)],
        compiler_params=_params("parallel", "arbitrary"),
        name="proj_c",
    )(h, norm_g.reshape(1, d), w, cf, sg)


def _ret_body(q_ref, k_ref, v_ref, g_ref, dm_ref, qd_ref, kd_ref, cd_ref, o_ref, *, nblk):
    t = RET_BLOCK
    dmask = dm_ref[...]
    qdec = qd_ref[...]
    kdec = kd_ref[...]
    cdec = cd_ref[...]

    def rows(ref, n):
        return ref[pl.ds(pl.multiple_of(n * t, t), t), :]

    def step(n, state):
        q = rows(q_ref, n)
        k = rows(k_ref, n)
        vb = rows(v_ref, n).astype(BF16)
        scores = _dot_nt(q.astype(BF16), k.astype(BF16)) * dmask
        intra = _dot(scores.astype(BF16), vb)
        inter = _dot((q * qdec).astype(BF16), state.astype(BF16))
        new_state = state * cdec + _dot_tn((k * kdec).astype(BF16), vb)
        o = intra + inter
        o = o * lax.rsqrt(jnp.mean(o * o, axis=-1, keepdims=True) + NORM_EPS)
        gate = rows(g_ref, n)
        r0 = pl.multiple_of(n * t, t)
        o_ref[pl.ds(r0, t), :] = (o * (gate * jax.nn.sigmoid(gate))).astype(BF16)
        return new_state

    lax.fori_loop(0, nblk, step, jnp.zeros((RET_HEAD_DIM, RET_HEAD_DIM), F32), unroll=True)


def _retention(proj, batch, seq):
    n = proj.shape[0]
    t = RET_BLOCK
    hd = RET_HEAD_DIM
    log_g = np.log(1.0 - 2.0 ** (-5.0 - np.arange(RET_HEADS, dtype=np.float64)))
    idx = np.arange(t, dtype=np.float64)
    chunk = np.arange(t) // CHUNK
    visible = chunk[:, None] >= chunk[None, :]
    dist = np.abs(idx[:, None] - idx[None, :])
    dmask = np.where(visible[None], np.exp(log_g[:, None, None] * dist[None]), 0.0)
    f32 = np.float32
    dmask = dmask.astype(f32)
    qdec = np.broadcast_to(np.exp(log_g[:, None] * (idx[None] + 1.0))[:, :, None], (RET_HEADS, t, hd)).astype(f32)
    kdec = np.broadcast_to(np.exp(log_g[:, None] * (t - 1.0 - idx[None]))[:, :, None], (RET_HEADS, t, hd)).astype(f32)
    cdec = np.broadcast_to(np.exp(log_g * t)[:, None, None], (RET_HEADS, 1, hd)).astype(f32)

    def col(off):
        return pl.BlockSpec((seq, hd), lambda b, h: (b, off + h))

    def tab(rows, cols):
        return pl.BlockSpec((None, rows, cols), lambda b, h: (h, 0, 0))

    return pl.pallas_call(
        functools.partial(_ret_body, nblk=seq // t),
        grid=(batch, RET_HEADS),
        in_specs=[col(0), col(RET_HEADS), col(2 * RET_HEADS), col(3 * RET_HEADS),
                  tab(t, t), tab(t, hd), tab(t, hd), tab(1, hd)],
        out_specs=pl.BlockSpec((seq, hd), lambda b, h: (b, h)),
        out_shape=jax.ShapeDtypeStruct((n, RET_WIDTH), BF16),
        compiler_params=_params("parallel", "parallel"),
        name="retention",
    )(proj, proj, proj, proj, dmask, qdec, kdec, cdec)


def _s5_weights(lam_re, lam_im, log_step, b_re, b_im, c_re, c_im):
    hp = lax.Precision.HIGH
    step = jnp.exp(log_step)[:, None]
    mag = jnp.exp(lam_re * step)
    a_re = mag * jnp.cos(lam_im * step)
    a_im = mag * jnp.sin(lam_im * step)
    den = lam_re * lam_re + lam_im * lam_im
    nr = a_re - 1.0
    f_re = (nr * lam_re + a_im * lam_im) / den
    f_im = (a_im * lam_re - nr * lam_im) / den
    bb_re = f_re[..., None] * b_re - f_im[..., None] * b_im
    bb_im = f_re[..., None] * b_im + f_im[..., None] * b_re
    prs, pis = [jnp.ones_like(a_re)], [jnp.zeros_like(a_im)]
    for _ in range(SSM_T):
        prs.append(prs[-1] * a_re - pis[-1] * a_im)
        pis.append(prs[-2] * a_im + pis[-1] * a_re)
    pr = jnp.stack(prs)
    pi = jnp.stack(pis)
    ca_re = c_re[None] * pr[:, :, None, :] - c_im[None] * pi[:, :, None, :]
    ca_im = c_re[None] * pi[:, :, None, :] + c_im[None] * pr[:, :, None, :]
    kern = jnp.einsum('kgpm,gmq->gkpq', jnp.concatenate([ca_re[:SSM_T], -ca_im[:SSM_T]], axis=-1),
                      jnp.concatenate([bb_re, bb_im], axis=1), precision=hp)
    nj = SSM_NJ
    ks = kern.transpose(0, 3, 1, 2).reshape(nj, 128, SSM_T * SSM_GROUP)
    ro = jnp.stack([ca_re[1:], -ca_im[1:]])
    ro = ro.reshape(2, SSM_T, nj, SSM_GPB, SSM_GROUP, SSM_STATE).transpose(2, 0, 3, 5, 1, 4)
    wo = ro.reshape(nj, 2 * SSM_GPB * SSM_STATE, SSM_T * SSM_GROUP)
    rev_re = jnp.stack([prs[SSM_T - 1 - s] for s in range(SSM_T)])
    rev_im = jnp.stack([pis[SSM_T - 1 - s] for s in range(SSM_T)])
    rev_re, rev_im = rev_re[:, :, None, :], rev_im[:, :, None, :]
    bt_re, bt_im = bb_re.transpose(0, 2, 1)[None], bb_im.transpose(0, 2, 1)[None]
    wi = jnp.concatenate([rev_re * bt_re - rev_im * bt_im, rev_re * bt_im + rev_im * bt_re], axis=-1)
    wi = wi.reshape(SSM_T, nj, 128, 2 * SSM_STATE).transpose(1, 0, 2, 3).reshape(nj, SSM_T * 128, 2 * SSM_STATE)
    a_t = jnp.stack([pr[SSM_T], pi[SSM_T]]).reshape(2, nj, SSM_GPB * SSM_STATE)
    a_t = jnp.moveaxis(a_t, 0, 1)
    return ks, wo, wi, a_t


def _s5_expand_body(ks_ref, wo_ref, wi_ref, wy_ref, win_ref):
    tp = SSM_T * SSM_GROUP
    wide = SSM_T * 128
    half = SSM_GPB * SSM_STATE

    def iota(shape, dim):
        return lax.broadcasted_iota(jnp.int32, shape, dim)

    r = iota((tp, wide), 0)
    c = iota((tp, wide), 1)
    rep = jnp.where((r // SSM_GROUP == c // 128) & (r % SSM_GROUP == c % SSM_GROUP), 1.0, 0.0).astype(BF16)
    col_b = (iota((1, wide), 1) % 128) // SSM_GROUP

    row_a = iota((128, 1), 0) // SSM_GROUP
    bdk = jnp.where(row_a == col_b, _dot(ks_ref[...].astype(BF16), rep), 0.0).astype(BF16)
    for s in range(SSM_T):
        if s:
            wy_ref[s * 128:(s + 1) * 128, :s * 128] = jnp.zeros((128, s * 128), BF16)
        wy_ref[s * 128:(s + 1) * 128, s * 128:] = bdk[:, :wide - s * 128]

    row_a = (iota((2 * half, 1), 0) % half) // SSM_STATE
    ro = _dot(wo_ref[...].astype(BF16), rep)
    wy_ref[wide:, :] = jnp.where(row_a == col_b, ro, 0.0).astype(BF16)

    r = iota((2 * SSM_STATE, 2 * half), 0)
    c = iota((2 * SSM_STATE, 2 * half), 1)
    rep_in = jnp.where((r // SSM_STATE == c // half) & (r % SSM_STATE == c % SSM_STATE), 1.0, 0.0).astype(BF16)
    row_a = (iota((wide, 1), 0) % 128) // SSM_GROUP
    col_b = (iota((1, 2 * half), 1) % half) // SSM_STATE
    win = _dot(wi_ref[...].astype(BF16), rep_in)
    win_ref[...] = jnp.where(row_a == col_b, win, 0.0).astype(BF16)


def _s5_expand(ks, wo, wi):
    nj = ks.shape[0]
    wide = SSM_T * 128
    half2 = 2 * SSM_GPB * SSM_STATE

    def blk(a):
        return pl.BlockSpec((None,) + a.shape[1:], lambda j: (j, 0, 0))

    return pl.pallas_call(
        _s5_expand_body,
        grid=(nj,),
        in_specs=[blk(ks), blk(wo), blk(wi)],
        out_specs=[pl.BlockSpec((None, wide + half2, wide), lambda j: (j, 0, 0)),
                   pl.BlockSpec((None, wide, half2), lambda j: (j, 0, 0))],
        out_shape=[jax.ShapeDtypeStruct((nj, wide + half2, wide), BF16),
                   jax.ShapeDtypeStruct((nj, wide, half2), BF16)],
        compiler_params=_params("parallel"),
        name="s5_expand",
    )(ks, wo, wi)


def _s5_body(u_ref, wy_ref, win_ref, at_ref, y_ref, s_scr, hp_scr, *, nb, nchunk):
    half = SSM_GPB * SSM_STATE
    r = nb * nchunk
    u = jnp.concatenate([u_ref[pl.ds(s, r, stride=SSM_T), :] for s in range(SSM_T)],
                        axis=1).astype(BF16)
    nslab = half // 128
    s_all = _dot(u, win_ref[...])
    for k in range(nslab):
        s_scr[k, 0:r, :] = s_all[:, k * 128:(k + 1) * 128]
        s_scr[k, r:2 * r, :] = s_all[:, half + k * 128:half + (k + 1) * 128]
    is_re = lax.broadcasted_iota(jnp.int32, (2 * nb, 128), 0) < nb
    coef_same = [jnp.broadcast_to(at_ref[0:1, k * 128:(k + 1) * 128], (2 * nb, 128)) for k in range(nslab)]
    coef_swap = [jnp.where(is_re, -at_ref[1:2, k * 128:(k + 1) * 128], at_ref[1:2, k * 128:(k + 1) * 128])
                 for k in range(nslab)]
    h = [jnp.zeros((2 * nb, 128), F32)] * nslab
    for c in range(nchunk):
        chunk_rows = pl.ds(c, 2 * nb, stride=nchunk)
        for k in range(nslab):
            hp_scr[k, chunk_rows, :] = h[k]
            h[k] = (coef_same[k] * h[k] + coef_swap[k] * pltpu.roll(h[k], nb, axis=0)
                    + s_scr[k, chunk_rows, :])
    hp = jnp.concatenate([hp_scr[k, 0:r, :] for k in range(nslab)]
                         + [hp_scr[k, r:2 * r, :] for k in range(nslab)], axis=1).astype(BF16)
    wide = SSM_T * 128
    cb = 256
    for lo in range(0, wide, cb):
        y = (_dot(u[:, :lo + cb], wy_ref[:lo + cb, lo:lo + cb])
             + _dot(hp, wy_ref[wide:, lo:lo + cb]))
        for t in range(lo // 128, (lo + cb) // 128):
            y_ref[pl.ds(t, r, stride=SSM_T), :] = y[:, t * 128 - lo:(t + 1) * 128 - lo]


def _s5(proj, w_y, w_in, a_t, batch, seq):
    n, width = proj.shape
    nchunk = seq // SSM_T
    nb = min(4, batch)
    r = nb * nchunk
    ucol0 = (width - SSM_WIDTH) // 128
    half2 = 2 * SSM_GPB * SSM_STATE
    once = pl.Buffered(1)
    return pl.pallas_call(
        functools.partial(_s5_body, nb=nb, nchunk=nchunk),
        grid=(SSM_NJ, batch // nb),
        in_specs=[
            pl.BlockSpec((r * SSM_T, 128), lambda j, i: (i, ucol0 + j)),
            pl.BlockSpec((None, SSM_T * 128 + half2, SSM_T * 128), lambda j, i: (j, 0, 0)),
            pl.BlockSpec((None, SSM_T * 128, half2), lambda j, i: (j, 0, 0)),
            pl.BlockSpec((None, 2, half2 // 2), lambda j, i: (j, 0, 0)),
        ],
        out_specs=pl.BlockSpec((r * SSM_T, 128), lambda j, i: (i, j)),
        out_shape=jax.ShapeDtypeStruct((n, SSM_WIDTH), F32),
        scratch_shapes=[pltpu.VMEM((half2 // 256, 2 * r, 128), F32), pltpu.VMEM((half2 // 256, 2 * r, 128), F32)],
        compiler_params=_params("arbitrary", "arbitrary"),
        name="s5",
    )(proj, w_y, w_in, a_t)


def _gelu(x):
    return 0.5 * x * (1.0 + jnp.tanh(math.sqrt(2.0 / math.pi) * (x + 0.044715 * (x * x * x))))


def _ab_out_body(ya_ref, ys_ref, u_ref, d_ref, wglu_ref, bglu_ref, woa_ref, wob_ref, h_ref, o_ref):
    y = ys_ref[...] + d_ref[...] * u_ref[...]
    z = _gelu(y)
    gate = jax.nn.sigmoid(_dot(z.astype(BF16), wglu_ref[...]) + bglu_ref[...])
    yb = (z * gate).astype(BF16)
    o_ref[...] = h_ref[...] + (_dot(ya_ref[...], woa_ref[...]) + _dot(yb, wob_ref[...]))


def _ab_out(y_a, y_ssm, proj, d_skip, w_glu, b_glu, w_out, h):
    n, d = h.shape
    tm = min(512, n)
    ucol = (proj.shape[1] - SSM_WIDTH) // SSM_WIDTH
    row = lambda i: (i, 0)
    fixed = lambda i: (0, 0)
    return pl.pallas_call(
        _ab_out_body,
        grid=(n // tm,),
        in_specs=[
            pl.BlockSpec((tm, RET_WIDTH), row),
            pl.BlockSpec((tm, SSM_WIDTH), row),
            pl.BlockSpec((tm, SSM_WIDTH), lambda i: (i, ucol)),
            pl.BlockSpec((1, SSM_WIDTH), fixed),
            pl.BlockSpec((SSM_WIDTH, SSM_WIDTH), fixed),
            pl.BlockSpec((1, SSM_WIDTH), fixed),
            pl.BlockSpec((RET_WIDTH, d), fixed),
            pl.BlockSpec((SSM_WIDTH, d), lambda i: (1, 0)),
            pl.BlockSpec((tm, d), row),
        ],
        out_specs=pl.BlockSpec((tm, d), row),
        out_shape=jax.ShapeDtypeStruct((n, d), F32),
        compiler_params=_params("parallel"),
        name="ab_out",
    )(y_a, y_ssm, proj, d_skip.reshape(1, -1), w_glu, b_glu.reshape(1, -1), w_out, w_out, h)


def _att_body(lam_ref, q_ref, k_ref, v_ref, sub_ref, *rest, seq, lambda_init, live_slabs):
    riding = len(rest) > 1
    o_ref = rest[3 if riding else 0]
    if riding:
        _rider_body(*rest[:3], *rest[4:7], live_slabs=live_slabs)
    tq = min(ATT_QBLOCK, seq)
    dh = DIFF_HEAD_DIM
    k2 = dh ** -0.5 * math.log2(math.e)
    lam = lam_ref[0]
    neg = jnp.finfo(F32).min
    rc = lax.broadcasted_iota(jnp.int32, (tq, tq), 0) // CHUNK
    cc = lax.broadcasted_iota(jnp.int32, (tq, tq), 1) // CHUNK
    visible = rc >= cc
    for qb in range(seq // tq):
        q0 = qb * tq
        q = q_ref[q0:q0 + tq, :]
        v_diag = v_ref[q0:q0 + tq, :]
        comps = []
        for c in range(2):
            qc = q[:, c * dh:(c + 1) * dh]
            s_diag = _dot_nt(k_ref[q0:q0 + tq, c * dh:(c + 1) * dh], qc)
            s_diag = jnp.where(rc <= cc, s_diag, neg)
            m = jnp.max(s_diag, axis=0, keepdims=True)
            if qb > 0:
                s_off = _dot_nt(k_ref[0:q0, c * dh:(c + 1) * dh], qc)
                m = jnp.maximum(m, jnp.max(s_off, axis=0, keepdims=True))
            mk = m * k2
            p_diag = jnp.exp2(s_diag * k2 - mk)
            l = jnp.sum(p_diag, axis=0, keepdims=True)
            acc = _dot_tn(p_diag.astype(BF16), v_diag)
            if qb > 0:
                p_off = jnp.exp2(s_off * k2 - mk)
                l = l + jnp.sum(p_off, axis=0, keepdims=True)
                acc = acc + _dot_tn(p_off.astype(BF16), v_ref[0:q0, :])
            comps.append(acc / l.T)
        o = comps[0] - lam * comps[1]
        o = _rms(o, sub_ref[...]) * (1.0 - lambda_init)
        o_ref[q0:q0 + tq, :] = o.astype(BF16)


def _attention(qkv, lam, subln, batch, seq, lambda_init, ride):
    n = qkv.shape[0]
    w = 2 * DIFF_HEAD_DIM
    rider = _Rider(*ride, batch * DIFF_HEADS, lambda b, h: b * DIFF_HEADS + h)
    extra = rider if rider.ok else None

    def col(off):
        return pl.BlockSpec((seq, w), lambda b, h: (b, off + h))

    out = pl.pallas_call(
        functools.partial(_att_body, seq=seq, lambda_init=lambda_init,
                          live_slabs=extra.live_slabs if extra else 0),
        grid=(batch, DIFF_HEADS),
        in_specs=[
            pl.BlockSpec(memory_space=pltpu.SMEM),
            col(0), col(DIFF_HEADS), col(2 * DIFF_HEADS),
            pl.BlockSpec((1, w), lambda b, h: (0, 0)),
        ] + (extra.in_specs if extra else []),
        out_specs=[pl.BlockSpec((seq, w), lambda b, h: (b, h))] + (extra.out_specs if extra else []),
        out_shape=[jax.ShapeDtypeStruct((n, D_MODEL), BF16)] + (extra.out_shape if extra else []),
        compiler_params=_params("arbitrary", "arbitrary"),
        name="diff_attention",
    )(lam.reshape(1), qkv, qkv, qkv, subln.reshape(1, w), *(extra.args if extra else ()))
    return out[0], (tuple(out[1:]) if extra else None)


def _out_proj_body(o_ref, w_ref, h_ref, y_ref):
    y_ref[...] = h_ref[...] + _dot(o_ref[...], w_ref[...])


def _out_proj(o, w, h):
    n, d = h.shape
    tm = min(512, n)
    return pl.pallas_call(
        _out_proj_body,
        grid=(n // tm,),
        in_specs=[
            pl.BlockSpec((tm, d), lambda i: (i, 0)),
            pl.BlockSpec((d, d), lambda i: (0, 0)),
            pl.BlockSpec((tm, d), lambda i: (i, 0)),
        ],
        out_specs=pl.BlockSpec((tm, d), lambda i: (i, 0)),
        out_shape=jax.ShapeDtypeStruct((n, d), F32),
        compiler_params=_params("parallel"),
        name="out_proj",
    )(o, w, h)


def _rope_tables(seq, rot_dim, theta):
    inv = 1.0 / (theta ** (np.arange(0, rot_dim, 2, dtype=np.float64) / rot_dim))
    ang = np.arange(seq, dtype=np.float64)[:, None] * inv[None, :]
    return np.cos(ang), np.sin(ang)


def kernel(x, ffn_norm, ffn_w_gate, ffn_w_up, ffn_w_down, mix_norm, ab_w_in, ab_w_out, ssm_lambda_re, ssm_lambda_im, ssm_log_step, ssm_b_re, ssm_b_im, ssm_c_re, ssm_c_im, ssm_d, ssm_w_glu, ssm_b_glu, c_w_qkv, c_w_out, c_lambda_q1, c_lambda_k1, c_lambda_q2, c_lambda_k2, c_subln, final_norm):
    batch, seq, d = x.shape
    n = batch * seq
    h = x.reshape(n, d)
    d_ff = ffn_w_down.shape[2]
    ffn_w = (ffn_w_gate, ffn_w_up, ffn_w_down)

    def ffn(h, layer, half, weights=None, final=False):
        wg, wu, wd = weights or _cast_ffn_weights(*ffn_w, layer, half)
        return _ffn(h, ffn_norm[layer, half], wg, wu, wd, final_norm, d_ff, final=final)

    h = ffn(h, 0, 0)
    ret_cos, ret_sin = _rope_tables(seq, RET_HEAD_DIM, RET_THETA)
    k_scale = RET_HEAD_DIM ** -0.5
    cos_tab = np.stack([ret_cos, ret_cos * k_scale, np.ones_like(ret_cos)]).astype(np.float32)
    sin_tab = np.stack([ret_sin, ret_sin * k_scale, np.zeros_like(ret_sin)]).astype(np.float32)
    proj, w01 = _proj_ab(h, mix_norm[0], ab_w_in[0].astype(BF16), cos_tab, sin_tab, seq, (*ffn_w, 0, 1))
    y_a = _retention(proj, batch, seq)
    ks, wo, wi, a_t = _s5_weights(ssm_lambda_re[0], ssm_lambda_im[0], ssm_log_step[0], ssm_b_re[0],
                                  ssm_b_im[0], ssm_c_re[0], ssm_c_im[0])
    w_y, w_in = _s5_expand(ks, wo, wi)
    y_ssm = _s5(proj, w_y, w_in, a_t, batch, seq)
    h = _ab_out(y_a, y_ssm, proj, ssm_d[0], ssm_w_glu[0].astype(BF16), ssm_b_glu[0],
                ab_w_out[0].astype(BF16), h)
    h = ffn(h, 0, 1, w01)

    h = ffn(h, 1, 0)
    att_cos, att_sin = _rope_tables(seq, 2 * ROT_HALF, ROPE_THETA)
    rest = np.zeros((seq, DIFF_HEAD_DIM // 2 - ROT_HALF))
    ident = np.zeros((seq, DIFF_HEAD_DIM))
    cf = np.stack([np.concatenate([att_cos, rest + 1.0, att_cos, rest + 1.0], axis=1), ident + 1.0])
    sg = np.stack([np.concatenate([-att_sin, rest, att_sin, rest], axis=1), ident])
    cf, sg = cf.astype(np.float32), sg.astype(np.float32)
    qkv = _proj_c(h, mix_norm[1], _cast_qkv(c_w_qkv[0]), cf, sg, seq)
    lambda_init = 0.8 - 0.6 * math.exp(-0.3 * 1)
    lam = (jnp.exp(jnp.sum(c_lambda_q1[0] * c_lambda_k1[0]))
           - jnp.exp(jnp.sum(c_lambda_q2[0] * c_lambda_k2[0])) + lambda_init)
    o, w11 = _attention(qkv, lam, c_subln[0], batch, seq, lambda_init, (*ffn_w, 1, 1))
    h = _out_proj(o, c_w_out[0].astype(BF16), h)
    h = ffn(h, 1, 1, w11, final=True)
    return h.reshape(batch, seq, d)
```
